```python
import jax, jax.numpy as jnp
from jax import lax
import numpy as np

D_MODEL = 2048
BATCH = 2
SEQ = 8192
DEPTH = 2
DEC_BATCH = 32
DEC_SEQ = 64
PAST_LEN = 4096

CHUNK = 64
N_BRANCH = 4
BRANCH_WIDTH = D_MODEL // 4
N_HEADS = 4
HEAD_DV = BRANCH_WIDTH // N_HEADS
GLA_DK = HEAD_DV // 2
GLA_LOW_RANK = 16
GLA_GATE_NORM = 16.0
HG_DK = HEAD_DV
GDN_DK = HEAD_DV
GDN_CONV = 4
GDN_QKV = 3 * N_HEADS * GDN_DK
RET_DK = HEAD_DV
ROPE_BASE = 10000.0
D_FF = 128 * ((8 * D_MODEL // 3 + 127) // 128)
FFN_CONV = 3
EPS = 1e-6

IN_SIZES = (
    N_HEADS * GLA_DK, N_HEADS * GLA_DK, BRANCH_WIDTH, BRANCH_WIDTH, GLA_LOW_RANK,
    N_HEADS * HG_DK, N_HEADS * HG_DK, BRANCH_WIDTH, BRANCH_WIDTH,
    GDN_QKV, BRANCH_WIDTH, N_HEADS, N_HEADS,
    N_HEADS * RET_DK, N_HEADS * RET_DK, BRANCH_WIDTH, BRANCH_WIDTH,
    N_BRANCH * D_MODEL,
)
N_IN = sum(IN_SIZES)
IN_SPLITS = tuple(int(s) for s in np.cumsum(IN_SIZES)[:-1])

kernel_name = "hybrid_gated_linear_stream_encoder_step"


def rms_norm(x, g):
    xf = x.astype(jnp.float32)
    y = xf * lax.rsqrt(jnp.mean(xf * xf, axis=-1, keepdims=True) + EPS)
    return (y * g.astype(jnp.float32)).astype(x.dtype)


def head_layer_norm(x, g, b):
    xf = x.astype(jnp.float32)
    mu = jnp.mean(xf, axis=-1, keepdims=True)
    xc = xf - mu
    y = xc * lax.rsqrt(jnp.mean(xc * xc, axis=-1, keepdims=True) + EPS)
    return (y * g.astype(jnp.float32) + b.astype(jnp.float32)).astype(x.dtype)


def l2_normalize(x):
    xf = x.astype(jnp.float32)
    return (xf * lax.rsqrt(jnp.sum(xf * xf, axis=-1, keepdims=True) + EPS)).astype(x.dtype)


def split_heads(x, d):
    return x.reshape(x.shape[:-1] + (x.shape[-1] // d, d))


def merge_heads(x):
    return x.reshape(x.shape[:-2] + (-1,))


def to_chunks(x, L):
    B, T = x.shape[:2]
    x = x.astype(jnp.float32).reshape((B, T // L, L) + x.shape[2:])
    return jnp.swapaxes(jnp.swapaxes(x, 0, 1), 2, 3)


def from_chunks(o):
    o = jnp.swapaxes(jnp.swapaxes(o, 2, 3), 0, 1)
    return o.reshape((o.shape[0], -1) + o.shape[3:])


def rotary(x, pos0):
    T, K = x.shape[1], x.shape[-1]
    pos = (pos0 + jnp.arange(T)).astype(jnp.float32)
    inv = 1.0 / (ROPE_BASE ** (jnp.arange(0, K, 2, dtype=jnp.float32) / K))
    ang = pos[:, None] * inv[None, :]
    cos, sin = jnp.cos(ang)[None, :, None, :], jnp.sin(ang)[None, :, None, :]
    xf = x.astype(jnp.float32)
    x1, x2 = xf[..., : K // 2], xf[..., K // 2:]
    return jnp.concatenate([x1 * cos - x2 * sin, x1 * sin + x2 * cos], axis=-1).astype(x.dtype)


def causal_dwconv(x, hist, w):
    W, T = w.shape[0], x.shape[1]
    xe = jnp.concatenate([hist.astype(x.dtype), x], axis=1)
    y = sum(xe[:, j:j + T] * w[j] for j in range(W))
    return y, xe[:, -(W - 1):]


def chunk_gla(q, k, v, log_f, s0):
    T = q.shape[1]
    L = min(CHUNK, T)
    incl = jnp.tril(jnp.ones((L, L), dtype=bool))

    def step(s, blk):
        qb, kb, vb, gb = blk
        b = jnp.cumsum(gb, axis=2)
        dec = jnp.exp(jnp.where(incl[:, :, None], b[:, :, :, None, :] - b[:, :, None, :, :], -jnp.inf))
        att = jnp.einsum("bhtk,bhsk,bhtsk->bhts", qb, kb, dec)
        o = jnp.einsum("bhts,bhsv->bhtv", att, vb) + jnp.einsum("bhtk,bhkv->bhtv", qb * jnp.exp(b), s)
        b_end = b[:, :, -1:, :]
        s = jnp.exp(b_end[:, :, 0, :])[..., None] * s + jnp.einsum("bhsk,bhsv->bhkv", kb * jnp.exp(b_end - b), vb)
        return s, o

    s, o = lax.scan(step, s0.astype(jnp.float32),
                    (to_chunks(q, L), to_chunks(k, L), to_chunks(v, L), to_chunks(log_f, L)))
    return from_chunks(o).astype(v.dtype), s.astype(s0.dtype)


def chunk_retention(q, k, v, log_g, s0):
    T = q.shape[1]
    L = min(CHUNK, T)
    incl = jnp.tril(jnp.ones((L, L), dtype=bool))

    def step(s, blk):
        qb, kb, vb, gb = blk
        b = jnp.cumsum(gb, axis=-1)
        dec = jnp.exp(jnp.where(incl, b[..., :, None] - b[..., None, :], -jnp.inf))
        att = jnp.einsum("bhtk,bhsk->bhts", qb, kb) * dec
        o = jnp.einsum("bhts,bhsv->bhtv", att, vb) + jnp.exp(b)[..., None] * jnp.einsum("bhtk,bhkv->bhtv", qb, s)
        b_end = b[..., -1:]
        s = jnp.exp(b_end)[..., None] * s + jnp.einsum("bhsk,bhsv->bhkv", kb * jnp.exp(b_end - b)[..., None], vb)
        return s, o

    s, o = lax.scan(step, s0.astype(jnp.float32),
                    (to_chunks(q, L), to_chunks(k, L), to_chunks(v, L), to_chunks(log_g, L)))
    return from_chunks(o).astype(v.dtype), s.astype(s0.dtype)


def chunk_gated_delta(q, k, v, beta, log_a, s0):
    T = q.shape[1]
    L = min(CHUNK, T)
    incl = jnp.tril(jnp.ones((L, L), dtype=bool))
    strict = jnp.tril(jnp.ones((L, L), dtype=bool), k=-1)

    def step(s, blk):
        qb, kb, vb, bb, gb = blk
        b = jnp.cumsum(gb, axis=-1)
        dec = jnp.exp(jnp.where(incl, b[..., :, None] - b[..., None, :], -jnp.inf))
        m = jnp.where(strict, jnp.einsum("bhtk,bhsk->bhts", kb, kb) * dec, 0.0) * bb[..., None]
        eb = jnp.exp(b)[..., None]
        rhs = bb[..., None] * (vb - eb * jnp.einsum("bhtk,bhkv->bhtv", kb, s))
        u = lax.linalg.triangular_solve(m, rhs, left_side=True, lower=True, unit_diagonal=True)
        att = jnp.einsum("bhtk,bhsk->bhts", qb, kb) * dec
        o = jnp.einsum("bhts,bhsv->bhtv", att, u) + eb * jnp.einsum("bhtk,bhkv->bhtv", qb, s)
        b_end = b[..., -1:]
        s = jnp.exp(b_end)[..., None] * s + jnp.einsum("bhsk,bhsv->bhkv", kb * jnp.exp(b_end - b)[..., None], u)
        return s, o

    s, o = lax.scan(step, s0.astype(jnp.float32),
                    (to_chunks(q, L), to_chunks(k, L), to_chunks(v, L), to_chunks(beta, L), to_chunks(log_a, L)))
    return from_chunks(o).astype(v.dtype), s.astype(s0.dtype)


def trunk_layer(x, states, p, lb, pos0):
    s_gla, s_hg, s_gdn, s_ret, c_gdn, c_ffn = states
    B, T, _ = x.shape
    f32 = jnp.float32
    h = rms_norm(x, p["norm_mix_g"])
    (gla_q, gla_k, gla_v, gla_g, gla_lr, hg_q, hg_f, hg_i, hg_g,
     gdn_qkv, gdn_z, gdn_b, gdn_a, ret_q, ret_k, ret_v, ret_g, merge) = jnp.split(h @ p["w_in"], IN_SPLITS, axis=-1)

    log_gk = jax.nn.log_sigmoid((gla_lr @ p["gla_w_gk"] + p["gla_b_gk"]).astype(f32)) / GLA_GATE_NORM
    o, s_gla = chunk_gla(split_heads(gla_q, GLA_DK) * GLA_DK ** -0.5, split_heads(gla_k, GLA_DK),
                         split_heads(gla_v, HEAD_DV), split_heads(log_gk, GLA_DK), s_gla)
    o_gla = merge_heads(rms_norm(o, p["gla_norm_g"]) * jax.nn.silu(split_heads(gla_g, HEAD_DV)))

    f = lb + (1.0 - lb) * jax.nn.sigmoid(hg_f.astype(f32))
    o, s_hg = chunk_gla(split_heads(jax.nn.silu(hg_q), HG_DK), split_heads((1.0 - f).astype(x.dtype), HG_DK),
                        split_heads(hg_i, HEAD_DV), split_heads(jnp.log(f), HG_DK), s_hg)
    o_hg = merge_heads(rms_norm(o, p["hgrn_norm_g"]) * jax.nn.silu(split_heads(hg_g, HEAD_DV)))

    qkv, c_gdn = causal_dwconv(gdn_qkv, c_gdn, p["gdn_conv_w"])
    dq, dk, dv = jnp.split(jax.nn.silu(qkv), 3, axis=-1)
    beta = jax.nn.sigmoid(gdn_b.astype(f32))
    log_a = -jnp.exp(p["gdn_a_log"].astype(f32)) * jax.nn.softplus((gdn_a + p["gdn_dt_bias"]).astype(f32))
    o, s_gdn = chunk_gated_delta(l2_normalize(split_heads(dq, GDN_DK)) * GDN_DK ** -0.5,
                                 l2_normalize(split_heads(dk, GDN_DK)), split_heads(dv, HEAD_DV), beta, log_a, s_gdn)
    o_gdn = merge_heads(rms_norm(o, p["gdn_norm_g"]) * jax.nn.silu(split_heads(gdn_z, HEAD_DV)))

    log_gamma = jnp.broadcast_to(jnp.log(1.0 - 2.0 ** (-5.0 - jnp.arange(N_HEADS, dtype=f32))), (B, T, N_HEADS))
    o, s_ret = chunk_retention(rotary(split_heads(ret_q, RET_DK), pos0),
                               rotary(split_heads(ret_k, RET_DK), pos0) * RET_DK ** -0.5,
                               split_heads(ret_v, HEAD_DV), log_gamma, s_ret)
    o_ret = merge_heads(head_layer_norm(o, p["ret_norm_g"], p["ret_norm_b"]) * jax.nn.silu(split_heads(ret_g, HEAD_DV)))

    branches = jnp.einsum("btnc,ncd->btnd", jnp.stack([o_gla, o_hg, o_gdn, o_ret], axis=2), p["w_branch"])
    gates = jax.nn.sigmoid(merge.reshape(B, T, N_BRANCH, D_MODEL))
    x = x + jnp.sum(gates * branches, axis=2) @ p["w_out"]

    h2 = rms_norm(x, p["norm_ffn_g"])
    a, u = jnp.split(h2 @ p["w_ffn_in"], 2, axis=-1)
    a, c_ffn = causal_dwconv(a, c_ffn, p["ffn_conv_w"])
    x = x + (jax.nn.silu(a + p["ffn_conv_b"]) * u) @ p["w_ffn_out"]
    return x, (s_gla, s_hg, s_gdn, s_ret, c_gdn, c_ffn)


def setup_inputs(seed: int = 0) -> dict:
    key = jax.random.key(seed)
    ks = jax.random.split(key, 32)
    nrm = jax.random.normal
    f32 = jnp.float32
    H = N_HEADS
    return {
        "x_prompt": nrm(ks[0], (BATCH, SEQ, D_MODEL), f32),
        "x_sample": nrm(ks[1], (DEC_BATCH, DEC_SEQ, D_MODEL), f32),
        "state_gla": 0.1 * nrm(ks[2], (DEPTH, DEC_BATCH, H, GLA_DK, HEAD_DV), f32),
        "state_hgrn": 0.1 * nrm(ks[3], (DEPTH, DEC_BATCH, H, HG_DK, HEAD_DV), f32),
        "state_gdn": 0.1 * nrm(ks[4], (DEPTH, DEC_BATCH, H, GDN_DK, HEAD_DV), f32),
        "state_ret": 0.1 * nrm(ks[5], (DEPTH, DEC_BATCH, H, RET_DK, HEAD_DV), f32),
        "cache_gdn_conv": nrm(ks[6], (DEPTH, DEC_BATCH, GDN_CONV - 1, GDN_QKV), f32),
        "cache_ffn_conv": nrm(ks[7], (DEPTH, DEC_BATCH, FFN_CONV - 1, D_FF), f32),
        "norm_mix_g": 1.0 + 0.02 * nrm(ks[8], (DEPTH, D_MODEL), f32),
        "w_in": nrm(ks[9], (DEPTH, D_MODEL, N_IN), f32) * D_MODEL ** -0.5,
        "gla_w_gk": nrm(ks[10], (DEPTH, GLA_LOW_RANK, H * GLA_DK), f32) * GLA_LOW_RANK ** -0.5,
        "gla_b_gk": 0.1 * nrm(ks[11], (DEPTH, H * GLA_DK), f32),
        "gla_norm_g": 1.0 + 0.02 * nrm(ks[12], (DEPTH, HEAD_DV), f32),
        "hgrn_lb_logits": 0.1 * nrm(ks[13], (DEPTH, H * HG_DK), f32),
        "hgrn_norm_g": 1.0 + 0.02 * nrm(ks[14], (DEPTH, HEAD_DV), f32),
        "gdn_conv_w": nrm(ks[15], (DEPTH, GDN_CONV, GDN_QKV), f32) * GDN_CONV ** -0.5,
        "gdn_a_log": jnp.log(jax.random.uniform(ks[16], (DEPTH, H), f32, 1.0, 16.0)),
        "gdn_dt_bias": jnp.log(jnp.expm1(jax.random.uniform(ks[17], (DEPTH, H), f32, 1e-3, 0.1))),
        "gdn_norm_g": 1.0 + 0.02 * nrm(ks[18], (DEPTH, HEAD_DV), f32),
        "ret_norm_g": 1.0 + 0.02 * nrm(ks[19], (DEPTH, HEAD_DV), f32),
        "ret_norm_b": 0.02 * nrm(ks[20], (DEPTH, HEAD_DV), f32),
        "w_branch": nrm(ks[21], (DEPTH, N_BRANCH, BRANCH_WIDTH, D_MODEL), f32) * BRANCH_WIDTH ** -0.5,
        "w_out": nrm(ks[22], (DEPTH, D_MODEL, D_MODEL), f32) * D_MODEL ** -0.5,
        "norm_ffn_g": 1.0 + 0.02 * nrm(ks[23], (DEPTH, D_MODEL), f32),
        "w_ffn_in": nrm(ks[24], (DEPTH, D_MODEL, 2 * D_FF), f32) * D_MODEL ** -0.5,
        "ffn_conv_w": nrm(ks[25], (DEPTH, FFN_CONV, D_FF), f32) * FFN_CONV ** -0.5,
        "ffn_conv_b": 0.02 * nrm(ks[26], (DEPTH, D_FF), f32),
        "w_ffn_out": nrm(ks[27], (DEPTH, D_FF, D_MODEL), f32) * D_FF ** -0.5,
        "norm_final_g": 1.0 + 0.02 * nrm(ks[28], (D_MODEL,), f32),
    }


def reference(x_prompt, x_sample, state_gla, state_hgrn, state_gdn, state_ret, cache_gdn_conv, cache_ffn_conv,
              norm_mix_g, w_in, gla_w_gk, gla_b_gk, gla_norm_g, hgrn_lb_logits, hgrn_norm_g,
              gdn_conv_w, gdn_a_log, gdn_dt_bias, gdn_norm_g, ret_norm_g, ret_norm_b,
              w_branch, w_out, norm_ffn_g, w_ffn_in, ffn_conv_w, ffn_conv_b, w_ffn_out, norm_final_g):
    lb_soft = jax.nn.softmax(hgrn_lb_logits.astype(jnp.float32), axis=0)
    lb_all = jnp.cumsum(lb_soft, axis=0) - lb_soft[0]

    dt = x_prompt.dtype
    prompt_states = (
        jnp.zeros((BATCH, N_HEADS, GLA_DK, HEAD_DV), dt),
        jnp.zeros((BATCH, N_HEADS, HG_DK, HEAD_DV), dt),
        jnp.zeros((BATCH, N_HEADS, GDN_DK, HEAD_DV), dt),
        jnp.zeros((BATCH, N_HEADS, RET_DK, HEAD_DV), dt),
        jnp.zeros((BATCH, GDN_CONV - 1, GDN_QKV), dt),
        jnp.zeros((BATCH, FFN_CONV - 1, D_FF), dt),
    )
    xp, xs = x_prompt, x_sample
    new_p, new_s = [], []
    for l in range(DEPTH):
        p = {
            "norm_mix_g": norm_mix_g[l], "w_in": w_in[l], "gla_w_gk": gla_w_gk[l], "gla_b_gk": gla_b_gk[l],
            "gla_norm_g": gla_norm_g[l], "hgrn_norm_g": hgrn_norm_g[l], "gdn_conv_w": gdn_conv_w[l],
            "gdn_a_log": gdn_a_log[l], "gdn_dt_bias": gdn_dt_bias[l], "gdn_norm_g": gdn_norm_g[l],
            "ret_norm_g": ret_norm_g[l], "ret_norm_b": ret_norm_b[l], "w_branch": w_branch[l], "w_out": w_out[l],
            "norm_ffn_g": norm_ffn_g[l], "w_ffn_in": w_ffn_in[l], "ffn_conv_w": ffn_conv_w[l],
            "ffn_conv_b": ffn_conv_b[l], "w_ffn_out": w_ffn_out[l],
        }
        xp, st_p = trunk_layer(xp, prompt_states, p, lb_all[l], 0)
        xs, st_s = trunk_layer(xs, (state_gla[l], state_hgrn[l], state_gdn[l], state_ret[l],
                                    cache_gdn_conv[l], cache_ffn_conv[l]), p, lb_all[l], PAST_LEN)
        new_p.append(st_p)
        new_s.append(st_s)
    y_prompt = rms_norm(xp, norm_final_g)
    y_sample = rms_norm(xs, norm_final_g)
    p_gla, p_hgrn, p_gdn, p_ret, p_gdn_conv, p_ffn_conv = (jnp.stack([st[i] for st in new_p]) for i in range(6))
    s_gla, s_hgrn, s_gdn, s_ret, s_gdn_conv, s_ffn_conv = (jnp.stack([st[i] for st in new_s]) for i in range(6))
    return (y_prompt, y_sample, p_gla, p_hgrn, p_gdn, p_ret, p_gdn_conv, p_ffn_conv,
            s_gla, s_hgrn, s_gdn, s_ret, s_gdn_conv, s_ffn_conv)
```

```python
import functools
import math

import numpy as np
import jax
import jax.numpy as jnp
from jax import lax
from jax.experimental import pallas as pl
from jax.experimental.pallas import tpu as pltpu

F32 = jnp.float32
BF16 = jnp.bfloat16

D_MODEL = 2048
DEPTH = 2
PAST_LEN = 4096
L = 64
N_BRANCH = 4
BRANCH_WIDTH = 512
N_HEADS = 4
HEAD_DV = 128
GLA_DK = 64
GLA_LOW_RANK = 16
GLA_GATE_NORM = 16.0
GDN_CONV = 4
GDN_QKV = 1536
ROPE_BASE = 10000.0
D_FF = 5504
D_FF_PAD = 5632
FFN_CONV = 3
EPS = 1e-6
N_MIX_IN = 7704

LANE = 128
SUBLANE = 8
VMEM_LIMIT = 56 * 1024 * 1024

P_GLA_Q, P_GLA_K, P_GLA_V, P_GLA_G, P_GLA_LR = 0, 512, 1024, 1536, 2048
P_HG_Q, P_HG_F, P_HG_I, P_HG_G = 2176, 2688, 3200, 3712
P_GDN_QKV, P_GDN_Z, P_GDN_BA = 4224, 5760, 6272
P_RET_Q, P_RET_K, P_RET_V, P_RET_G = 6400, 6912, 7424, 7936
N_PACK = 8448
N_LEVELS = 6


def _dot(a, b):
    return jnp.dot(a, b, preferred_element_type=F32)


def _dot_nt(a, b):
    return lax.dot_general(a, b, (((1,), (1,)), ((), ())), preferred_element_type=F32)


def _dot_tn(a, b):
    return lax.dot_general(a, b, (((0,), (0,)), ((), ())), preferred_element_type=F32)


def _split3(x):
    hi = x.astype(BF16)
    r = x - hi.astype(F32)
    mid = r.astype(BF16)
    lo = (r - mid.astype(F32)).astype(BF16)
    return hi, mid, lo


def _split2(x):
    hi = x.astype(BF16)
    lo = (x - hi.astype(F32)).astype(BF16)
    return hi, lo


def _dot_hi(a, b):
    a_hi, a_lo = _split2(a)
    b_hi, b_lo = _split2(b)
    return _dot(a_hi, b_hi) + (_dot(a_hi, b_lo) + _dot(a_lo, b_hi))


def _sigmoid(x):
    return 1.0 / (1.0 + jnp.exp(-x))


def _silu(x):
    return x * _sigmoid(x)


def _softplus(x):
    return jnp.maximum(x, 0.0) + jnp.log(1.0 + jnp.exp(-jnp.abs(x)))


def _compiler_params(semantics):
    return pltpu.CompilerParams(dimension_semantics=semantics, vmem_limit_bytes=VMEM_LIMIT)


def _tile(n, pref):
    t = min(n, pref)
    while n % t:
        t //= 2
    return t


def _rmsnorm_kernel(x_ref, g_ref, h_ref):
    x = x_ref[...]
    y = x * lax.rsqrt(jnp.mean(x * x, axis=-1, keepdims=True) + EPS)
    h_ref[...] = (y * g_ref[...]).astype(h_ref.dtype)


def _rmsnorm(x, g, out_dtype):
    m, d = x.shape
    tm = _tile(m, 512)
    return pl.pallas_call(
        _rmsnorm_kernel,
        grid=(m // tm,),
        in_specs=[pl.BlockSpec((tm, d), lambda i: (i, 0)),
                  pl.BlockSpec((1, d), lambda i: (0, 0))],
        out_specs=pl.BlockSpec((tm, d), lambda i: (i, 0)),
        out_shape=jax.ShapeDtypeStruct((m, d), out_dtype),
        compiler_params=_compiler_params(("parallel",)),
        name="rmsnorm",
    )(x, g.reshape(1, d))


def _mm_kernel(a_ref, w_ref, o_ref):
    o_ref[...] = _dot(a_ref[...], w_ref[...]).astype(o_ref.dtype)


def _matmul(a, w, tn, out_dtype):
    m, k = a.shape
    n = w.shape[1]
    tm = _tile(m, 512)
    return pl.pallas_call(
        _mm_kernel,
        grid=(n // tn, m // tm),
        in_specs=[pl.BlockSpec((tm, k), lambda j, i: (i, 0)),
                  pl.BlockSpec((k, tn), lambda j, i: (0, j))],
        out_specs=pl.BlockSpec((tm, tn), lambda j, i: (i, j)),
        out_shape=jax.ShapeDtypeStruct((m, n), out_dtype),
        compiler_params=_compiler_params(("parallel", "parallel")),
        name="in_proj",
    )(a, w)


def _resid_kernel(emit_x, a_ref, w_ref, x_ref, g_ref, *out_refs):
    x = x_ref[...] + _dot(a_ref[...], w_ref[...])
    y = x * lax.rsqrt(jnp.mean(x * x, axis=-1, keepdims=True) + EPS)
    h_ref = out_refs[-1]
    h_ref[...] = (y * g_ref[...]).astype(h_ref.dtype)
    if emit_x:
        out_refs[0][...] = x


def _resid_matmul_norm(a, w, x, g, tm, h_dtype, emit_x, name):
    m, k = a.shape
    d = w.shape[1]
    tm = _tile(m, tm)
    row = lambda i: (i, 0)
    out_specs = [pl.BlockSpec((tm, d), row)]
    out_shape = [jax.ShapeDtypeStruct((m, d), h_dtype)]
    if emit_x:
        out_specs = [pl.BlockSpec((tm, d), row)] + out_specs
        out_shape = [jax.ShapeDtypeStruct((m, d), F32)] + out_shape
    return pl.pallas_call(
        functools.partial(_resid_kernel, emit_x),
        grid=(m // tm,),
        in_specs=[pl.BlockSpec((tm, k), row),
                  pl.BlockSpec((k, d), lambda i: (0, 0), pipeline_mode=pl.Buffered(1)),
                  pl.BlockSpec((tm, d), row),
                  pl.BlockSpec((1, d), lambda i: (0, 0))],
        out_specs=out_specs,
        out_shape=out_shape,
        compiler_params=_compiler_params(("parallel",)),
        name=name,
    )(a, w, x, g.reshape(1, d))


def _merge_kernel(h_ref, o_ref, wm_ref, wb_ref, y_ref):
    tn = y_ref.shape[1]
    gates = _dot(h_ref[...], wm_ref[...])
    acc = None
    for n in range(N_BRANCH):
        br = _dot(o_ref[:, n * BRANCH_WIDTH:(n + 1) * BRANCH_WIDTH], wb_ref[n])
        term = _sigmoid(gates[:, n * tn:(n + 1) * tn]) * br
        acc = term if acc is None else acc + term
    y_ref[...] = acc.astype(y_ref.dtype)


def _merge(h, o_all, wm_packed, wb, tn):
    m, d = h.shape
    tm = _tile(m, 1024)
    return pl.pallas_call(
        _merge_kernel,
        grid=(d // tn, m // tm),
        in_specs=[pl.BlockSpec((tm, d), lambda j, i: (i, 0)),
                  pl.BlockSpec((tm, d), lambda j, i: (i, 0)),
                  pl.BlockSpec((d, N_BRANCH * tn), lambda j, i: (0, j)),
                  pl.BlockSpec((N_BRANCH, BRANCH_WIDTH, tn), lambda j, i: (0, 0, j))],
        out_specs=pl.BlockSpec((tm, tn), lambda j, i: (i, j)),
        out_shape=jax.ShapeDtypeStruct((m, d), BF16),
        compiler_params=_compiler_params(("parallel", "parallel")),
        name="merge",
    )(h, o_all, wm_packed, wb)


def _ffn_gate(a, prev1, prev2, u, cw_ref, cb_ref):
    conv = cw_ref[2:3, :] * a + cw_ref[1:2, :] * prev1 + cw_ref[0:1, :] * prev2 + cb_ref[...]
    return (_silu(conv) * u)


def _ffn_in_prompt_kernel(tiles_per_stream, h_ref, wa_ref, wu_ref, cw_ref, cb_ref,
                          g_ref, cache_ref, buf_ref):
    tm = h_ref.shape[0]
    i = pl.program_id(1)

    @pl.when(i % tiles_per_stream == 0)
    def _():
        buf_ref[0:SUBLANE, :] = jnp.zeros((SUBLANE, buf_ref.shape[1]), F32)

    h = h_ref[...]
    a = _dot(h, wa_ref[...])
    u = _dot(h, wu_ref[...])
    buf_ref[SUBLANE:SUBLANE + tm, :] = a
    prev1 = buf_ref[SUBLANE - 1:SUBLANE - 1 + tm, :]
    prev2 = buf_ref[SUBLANE - 2:SUBLANE - 2 + tm, :]
    g_ref[...] = _ffn_gate(a, prev1, prev2, u, cw_ref, cb_ref).astype(g_ref.dtype)
    last2 = a[tm - 2:tm, :]
    cache_ref[0] = last2
    buf_ref[SUBLANE - 2:SUBLANE, :] = last2


def _ffn_in_prompt(h, wa, wu, cw, cb, n_streams, tn):
    m, d = h.shape
    seq = m // n_streams
    tm = _tile(seq, 1024)
    tiles_per_stream = seq // tm
    dff = wa.shape[1]
    return pl.pallas_call(
        functools.partial(_ffn_in_prompt_kernel, tiles_per_stream),
        grid=(dff // tn, m // tm),
        in_specs=[pl.BlockSpec((tm, d), lambda j, i: (i, 0)),
                  pl.BlockSpec((d, tn), lambda j, i: (0, j)),
                  pl.BlockSpec((d, tn), lambda j, i: (0, j)),
                  pl.BlockSpec((FFN_CONV, tn), lambda j, i: (0, j)),
                  pl.BlockSpec((1, tn), lambda j, i: (0, j))],
        out_specs=[pl.BlockSpec((tm, tn), lambda j, i: (i, j)),
                   pl.BlockSpec((1, FFN_CONV - 1, tn), lambda j, i: (i // tiles_per_stream, 0, j))],
        out_shape=[jax.ShapeDtypeStruct((m, dff), BF16),
                   jax.ShapeDtypeStruct((n_streams, FFN_CONV - 1, dff), F32)],
        scratch_shapes=[pltpu.VMEM((SUBLANE + tm, tn), F32)],
        compiler_params=_compiler_params(("parallel", "arbitrary")),
        name="ffn_in_prompt",
    )(h, wa, wu, cw, cb)


def _ffn_in_sample_kernel(seq, h_ref, wa_ref, wu_ref, cw_ref, cb_ref, cin_ref,
                          g_ref, cache_ref, buf_ref):
    n_streams = h_ref.shape[0] // seq
    h = h_ref[...]
    a_all = _dot(h, wa_ref[...])
    u_all = _dot(h, wu_ref[...])
    for s in range(n_streams):
        rows = slice(s * seq, (s + 1) * seq)
        a = a_all[rows]
        buf_ref[0:SUBLANE, :] = cin_ref[s]
        buf_ref[SUBLANE:SUBLANE + seq, :] = a
        prev1 = buf_ref[SUBLANE - 1:SUBLANE - 1 + seq, :]
        prev2 = buf_ref[SUBLANE - 2:SUBLANE - 2 + seq, :]
        g_ref[rows, :] = _ffn_gate(a, prev1, prev2, u_all[rows], cw_ref, cb_ref).astype(g_ref.dtype)
        cache_ref[s] = a[seq - 2:seq, :]


def _ffn_in_sample(h, wa, wu, cw, cb, cache_in, seq, tn):
    m, d = h.shape
    n_streams = m // seq
    dff = wa.shape[1]
    return pl.pallas_call(
        functools.partial(_ffn_in_sample_kernel, seq),
        grid=(dff // tn,),
        in_specs=[pl.BlockSpec((m, d), lambda j: (0, 0)),
                  pl.BlockSpec((d, tn), lambda j: (0, j)),
                  pl.BlockSpec((d, tn), lambda j: (0, j)),
                  pl.BlockSpec((FFN_CONV, tn), lambda j: (0, j)),
                  pl.BlockSpec((1, tn), lambda j: (0, j)),
                  pl.BlockSpec((n_streams, SUBLANE, tn), lambda j: (0, 0, j))],
        out_specs=[pl.BlockSpec((m, tn), lambda j: (0, j)),
                   pl.BlockSpec((n_streams, FFN_CONV - 1, tn), lambda j: (0, 0, j))],
        out_shape=[jax.ShapeDtypeStruct((m, dff), BF16),
                   jax.ShapeDtypeStruct((n_streams, FFN_CONV - 1, dff), F32)],
        scratch_shapes=[pltpu.VMEM((SUBLANE + seq, tn), F32)],
        compiler_params=_compiler_params(("parallel",)),
        name="ffn_in_sample",
    )(h, wa, wu, cw, cb, cache_in)


def _level_matrix():
    t = np.arange(L)[:, None]
    u = np.arange(L)[None, :]
    blocks = [(u <= t)]
    for l in range(N_LEVELS):
        n = 1 << l
        r = (t & ~(2 * n - 1)) + n - 1
        upper = (t & n) != 0
        blocks.append(np.where(upper, (u > r) & (u <= t), (u > t) & (u <= r)))
    blocks.append(u > t)
    return np.concatenate(blocks, axis=0).astype(np.float32)


def _pair_masks():
    t = lax.broadcasted_iota(jnp.int32, (L, L), 0)
    s = lax.broadcasted_iota(jnp.int32, (L, L), 1)
    x = jnp.bitwise_xor(t, s)
    lower = s < t
    levels = [jnp.logical_and(lower, jnp.right_shift(x, l) == 1) for l in range(N_LEVELS)]
    return t, s, x == 0, lower, levels


def _head_rms(o, gain, gate):
    y = o * lax.rsqrt(jnp.mean(o * o, axis=-1, keepdims=True) + EPS)
    return y * gain * _silu(gate)


def _gla_type(q, k, v, g, gate, gain, st_ref, mall, masks, o_ref, o_off):
    _, _, diag, _, levels = masks
    g_hi, g_mid, g_lo = _split3(g)
    d3 = _dot(mall, jnp.concatenate([g_hi, g_mid, g_lo], axis=1))
    w = g.shape[1]
    e_all = jnp.exp(d3[:, :w] + (d3[:, w:2 * w] + d3[:, 2 * w:]))
    e_b = e_all[0:L]
    e_end = e_all[(N_LEVELS + 1) * L:(N_LEVELS + 2) * L]
    for h in range(N_HEADS):
        sl = slice(h * HEAD_DV, (h + 1) * HEAD_DV)
        qh, kh, vh = q[:, sl], k[:, sl], v[:, sl]
        att = jnp.where(diag, _dot_nt(qh.astype(BF16), kh.astype(BF16)), 0.0)
        for l in range(N_LEVELS):
            e_l = e_all[(l + 1) * L:(l + 2) * L, sl]
            a_l = _dot_nt((qh * e_l).astype(BF16), (kh * e_l).astype(BF16))
            att = jnp.where(levels[l], a_l, att)
        st = st_ref[0, h]
        vb = vh.astype(BF16)
        o = _dot(att.astype(BF16), vb) + _dot_nt((qh * e_b[:, sl]).astype(BF16), st.astype(BF16))
        o_ref[:, o_off + h * HEAD_DV:o_off + (h + 1) * HEAD_DV] = (
            _head_rms(o, gain, gate[:, sl]).astype(o_ref.dtype))
        decay_end = e_b[L - 1:L, sl]
        st_ref[0, h] = st * decay_end + _dot_tn(vb, (kh * e_end[:, sl]).astype(BF16))


def _mixer_kernel(layer, n_prompt_chunks, chunks_per_stream,
                  proj_ref, sgla_in, shg_in, sgdn_in, sret_in, cgdn_in,
                  cos_ref, sin_ref, mall_ref, wgk_ref, bgk_ref, norms_ref, lb_ref,
                  convw_ref, gdnp_ref,
                  o_ref, sgla_ref, shg_ref, sgdn_ref, sret_ref, cgdn_ref,
                  xe_ref):
    c = pl.program_id(0)
    first = jnp.logical_or(c >= n_prompt_chunks, c % chunks_per_stream == 0)

    @pl.when(first)
    def _():
        sgla_ref[...] = sgla_in[...]
        shg_ref[...] = shg_in[...]
        sgdn_ref[...] = sgdn_in[...]
        sret_ref[...] = sret_in[...]
        xe_ref[0:SUBLANE, :] = cgdn_in[0]

    masks = _pair_masks()
    t_idx, s_idx, diag, lower, _ = masks
    incl = jnp.logical_or(lower, diag)
    mall = mall_ref[...]
    tri = mall[0:L]

    def col(off, width=512):
        return proj_ref[:, off:off + width]

    z = _dot(col(P_GLA_LR, LANE).astype(BF16), wgk_ref[...]) + bgk_ref[...]
    g_gla = (jnp.minimum(z, 0.0) - jnp.log(1.0 + jnp.exp(-jnp.abs(z)))) * (1.0 / GLA_GATE_NORM)
    _gla_type(col(P_GLA_Q) * GLA_DK ** -0.5, col(P_GLA_K), col(P_GLA_V), g_gla, col(P_GLA_G),
              norms_ref[0:1, :], sgla_ref, mall, masks, o_ref, 0)

    if layer == 0:
        lb = jnp.zeros((1, 512), F32)
    else:
        logits = lb_ref[...]
        ex = jnp.exp(logits - jnp.max(logits, axis=0, keepdims=True))
        soft = ex / jnp.sum(ex, axis=0, keepdims=True)
        lb = jnp.sum(soft[1:layer + 1], axis=0, keepdims=True)
    f = lb + (1.0 - lb) * _sigmoid(col(P_HG_F))
    _gla_type(_silu(col(P_HG_Q)), 1.0 - f, col(P_HG_I), jnp.log(f), col(P_HG_G),
              norms_ref[1:2, :], shg_ref, mall, masks, o_ref, BRANCH_WIDTH)

    xe_ref[SUBLANE:SUBLANE + L, :] = col(P_GDN_QKV, GDN_QKV)
    base = SUBLANE - (GDN_CONV - 1)
    conv = convw_ref[0:1, :] * xe_ref[base:base + L, :]
    for j in range(1, GDN_CONV):
        conv = conv + convw_ref[j:j + 1, :] * xe_ref[base + j:base + j + L, :]
    new_hist = xe_ref[SUBLANE + L - (GDN_CONV - 1):SUBLANE + L, :]
    cgdn_ref[0] = new_hist
    xe_ref[base:SUBLANE, :] = new_hist
    act = _silu(conv)
    ba = col(P_GDN_BA, LANE)
    beta_all = _sigmoid(ba)
    log_a = -jnp.exp(gdnp_ref[0:1, :]) * _softplus(ba + gdnp_ref[1:2, :])
    la_hi, la_mid, la_lo = _split3(log_a)
    b_all = _dot(tri, la_hi) + (_dot(tri, la_mid) + _dot(tri, la_lo))
    b_rows = b_all.T
    eye = jnp.where(diag, 1.0, 0.0)
    gdn_gate = col(P_GDN_Z)
    for h in range(N_HEADS):
        sl = slice(h * HEAD_DV, (h + 1) * HEAD_DV)
        dq, dk, dv = act[:, sl], act[:, 512 + h * 128:512 + (h + 1) * 128], act[:, 1024 + h * 128:1024 + (h + 1) * 128]
        qn = dq * lax.rsqrt(jnp.sum(dq * dq, axis=-1, keepdims=True) + EPS) * HEAD_DV ** -0.5
        kn = dk * lax.rsqrt(jnp.sum(dk * dk, axis=-1, keepdims=True) + EPS)
        b_col = b_all[:, N_HEADS + h:N_HEADS + h + 1]
        b_row = b_rows[N_HEADS + h:N_HEADS + h + 1, :]
        beta = beta_all[:, h:h + 1]
        dec = jnp.exp(jnp.where(incl, b_col - b_row, -1e30))
        e_b = jnp.exp(b_col)
        b_end = b_col[L - 1:L, :]
        qb, kb = qn.astype(BF16), kn.astype(BF16)
        st = sgdn_ref[0, h]
        stb = st.astype(BF16)
        m = jnp.where(lower, _dot_nt(kb, kb) * dec, 0.0) * beta
        rhs = beta * (dv - e_b * _dot(kb, stb))
        p = -m
        tinv = eye + p
        for _ in range(N_LEVELS - 1):
            p = _dot_hi(p, p)
            tinv = tinv + _dot_hi(tinv, p)
        u = _dot_hi(tinv, rhs)
        ub = u.astype(BF16)
        att = jnp.where(incl, _dot_nt(qb, kb) * dec, 0.0)
        o = _dot(att.astype(BF16), ub) + e_b * _dot(qb, stb)
        o_ref[:, 2 * BRANCH_WIDTH + h * HEAD_DV:2 * BRANCH_WIDTH + (h + 1) * HEAD_DV] = (
            _head_rms(o, norms_ref[2:3, :], gdn_gate[:, sl]).astype(o_ref.dtype))
        sgdn_ref[0, h] = jnp.exp(b_end) * st + _dot_tn((kn * jnp.exp(b_end - b_col)).astype(BF16), ub)

    cos, sin = cos_ref[...], sin_ref[...]
    rq, rk, rv, rg = col(P_RET_Q), col(P_RET_K), col(P_RET_V), col(P_RET_G)
    tf = t_idx.astype(F32)
    sf = s_idx.astype(F32)
    t_col = tf[:, 0:1]
    for h in range(N_HEADS):
        sl = slice(h * HEAD_DV, (h + 1) * HEAD_DV)
        lg = math.log(1.0 - 2.0 ** (-5.0 - h))
        qh = rq[:, sl]
        kh = rk[:, sl]
        qr = qh * cos + pltpu.roll(qh, HEAD_DV // 2, axis=1) * sin
        kr = (kh * cos + pltpu.roll(kh, HEAD_DV // 2, axis=1) * sin) * HEAD_DV ** -0.5
        dec = jnp.exp(jnp.where(incl, (tf - sf) * lg, -1e30))
        qb, kb, vb = qr.astype(BF16), kr.astype(BF16), rv[:, sl].astype(BF16)
        st = sret_ref[0, h]
        att = _dot_nt(qb, kb) * dec
        o = _dot(att.astype(BF16), vb) + jnp.exp((t_col + 1.0) * lg) * _dot(qb, st.astype(BF16))
        mu = jnp.mean(o, axis=-1, keepdims=True)
        oc = o - mu
        y = oc * lax.rsqrt(jnp.mean(oc * oc, axis=-1, keepdims=True) + EPS)
        y = (y * norms_ref[3:4, :] + norms_ref[4:5, :]) * _silu(rg[:, sl])
        o_ref[:, 3 * BRANCH_WIDTH + h * HEAD_DV:3 * BRANCH_WIDTH + (h + 1) * HEAD_DV] = y.astype(o_ref.dtype)
        k_end = (kr * jnp.exp((L - 1.0 - t_col) * lg)).astype(BF16)
        sret_ref[0, h] = math.exp(L * lg) * st + _dot_tn(k_end, vb)


def _mixers(layer, proj, states_in, cgdn_in, cos_t, sin_t, mall, wgk, bgk, norms, lb_logits,
            convw, gdnp, n_prompt_streams, prompt_seq):
    t_total = proj.shape[0]
    n_chunks = t_total // L
    chunks_per_stream = prompt_seq // L
    n_prompt_chunks = n_prompt_streams * chunks_per_stream
    n_streams = states_in[0].shape[0]
    past_block = PAST_LEN // L

    def stream(c):
        return jnp.where(c < n_prompt_chunks, c // chunks_per_stream, c - n_prompt_chunks + n_prompt_streams)

    def pos_block(c):
        return jnp.where(c < n_prompt_chunks, c % chunks_per_stream, past_block)

    st_spec = pl.BlockSpec((1, N_HEADS, HEAD_DV, HEAD_DV), lambda c: (stream(c), 0, 0, 0))
    const = lambda shape: pl.BlockSpec(shape, lambda c: tuple(0 for _ in shape))
    st_shape = jax.ShapeDtypeStruct((n_streams, N_HEADS, HEAD_DV, HEAD_DV), F32)
    return pl.pallas_call(
        functools.partial(_mixer_kernel, layer, n_prompt_chunks, chunks_per_stream),
        grid=(n_chunks,),
        in_specs=[pl.BlockSpec((L, N_PACK), lambda c: (c, 0)),
                  st_spec, st_spec, st_spec, st_spec,
                  pl.BlockSpec((1, SUBLANE, GDN_QKV), lambda c: (stream(c), 0, 0)),
                  pl.BlockSpec((L, HEAD_DV), lambda c: (pos_block(c), 0)),
                  pl.BlockSpec((L, HEAD_DV), lambda c: (pos_block(c), 0)),
                  const(mall.shape), const(wgk.shape), const(bgk.shape), const(norms.shape),
                  const(lb_logits.shape), const(convw.shape), const(gdnp.shape)],
        out_specs=[pl.BlockSpec((L, D_MODEL), lambda c: (c, 0)),
                   st_spec, st_spec, st_spec, st_spec,
                   pl.BlockSpec((1, GDN_CONV - 1, GDN_QKV), lambda c: (stream(c), 0, 0))],
        out_shape=[jax.ShapeDtypeStruct((t_total, D_MODEL), BF16),
                   st_shape, st_shape, st_shape, st_shape,
                   jax.ShapeDtypeStruct((n_streams, GDN_CONV - 1, GDN_QKV), F32)],
        scratch_shapes=[pltpu.VMEM((SUBLANE + L, GDN_QKV), F32)],
        compiler_params=_compiler_params(("arbitrary",)),
        name="mixers",
    )(proj, *states_in, cgdn_in, cos_t, sin_t, mall, wgk, bgk, norms, lb_logits, convw, gdnp)


def _pad_heads(w, dk):
    lead = w.shape[:-1]
    w = w.reshape(lead + (N_HEADS, dk))
    w = jnp.pad(w, [(0, 0)] * len(lead) + [(0, 0), (0, HEAD_DV - dk)])
    return w.reshape(lead + (N_HEADS * HEAD_DV,))


def _pad_cols(w, width):
    return jnp.pad(w, [(0, 0)] * (w.ndim - 1) + [(0, width - w.shape[-1])])


def _pack_w_in(w):
    def cols(off, n):
        return w[:, off:off + n]
    parts = [
        _pad_heads(cols(0, 256), GLA_DK), _pad_heads(cols(256, 256), GLA_DK), cols(512, 512), cols(1024, 512),
        _pad_cols(cols(1536, GLA_LOW_RANK), LANE),
        cols(1552, 512), cols(2064, 512), cols(2576, 512), cols(3088, 512),
        cols(3600, GDN_QKV), cols(5136, 512), _pad_cols(cols(5648, 2 * N_HEADS), LANE),
        cols(5656, 512), cols(6168, 512), cols(6680, 512), cols(7192, 512),
    ]
    return jnp.concatenate(parts, axis=1).astype(BF16)


def _pack_w_merge(w, tn):
    wm = w[:, N_MIX_IN:].reshape(D_MODEL, N_BRANCH, D_MODEL // tn, tn)
    return jnp.swapaxes(wm, 1, 2).reshape(D_MODEL, N_BRANCH * D_MODEL).astype(BF16)


def _rope_tables(n_pos):
    pos = jnp.arange(n_pos).astype(F32)
    inv = 1.0 / (ROPE_BASE ** (jnp.arange(0, HEAD_DV, 2, dtype=F32) / HEAD_DV))
    ang = pos[:, None] * inv[None, :]
    cos, sin = jnp.cos(ang), jnp.sin(ang)
    return jnp.concatenate([cos, cos], axis=1), jnp.concatenate([-sin, sin], axis=1)


def kernel(x_prompt, x_sample, state_gla, state_hgrn, state_gdn, state_ret, cache_gdn_conv, cache_ffn_conv, norm_mix_g, w_in, gla_w_gk, gla_b_gk, gla_norm_g, hgrn_lb_logits, hgrn_norm_g, gdn_conv_w, gdn_a_log, gdn_dt_bias, gdn_norm_g, ret_norm_g, ret_norm_b, w_branch, w_out, norm_ffn_g, w_ffn_in, ffn_conv_w, ffn_conv_b, w_ffn_out, norm_final_g):
    nb, seq, d = x_prompt.shape
    ns, dseq, _ = x_sample.shape
    assert d == D_MODEL and dseq == L and seq % L == 0
    tp = nb * seq
    x = jnp.concatenate([x_prompt.reshape(tp, d), x_sample.reshape(ns * dseq, d)], axis=0)

    cos_t, sin_t = _rope_tables(max(seq, PAST_LEN + dseq))
    mall = jnp.asarray(_level_matrix(), BF16)
    merge_tn = 512
    ffn_tn = 512

    def with_prompt(s, transpose=False, pad_k=0):
        if pad_k:
            s = jnp.pad(s, ((0, 0), (0, 0), (0, pad_k), (0, 0)))
        if transpose:
            s = jnp.swapaxes(s, 2, 3)
        return jnp.concatenate([jnp.zeros((nb,) + s.shape[1:], F32), s], axis=0)

    h = _rmsnorm(x, norm_mix_g[0], BF16)
    new_states = []
    for l in range(DEPTH):
        w_mix = _pack_w_in(w_in[l])
        proj = _matmul(h, w_mix, 2816, F32)

        states_in = (with_prompt(state_gla[l], transpose=True, pad_k=HEAD_DV - GLA_DK),
                     with_prompt(state_hgrn[l], transpose=True),
                     with_prompt(state_gdn[l]), with_prompt(state_ret[l]))
        cgdn_in = jnp.concatenate([jnp.zeros((nb, GDN_CONV - 1, GDN_QKV), F32), cache_gdn_conv[l]], axis=0)
        cgdn_in = jnp.pad(cgdn_in, ((0, 0), (SUBLANE - (GDN_CONV - 1), 0), (0, 0)))
        wgk = jnp.pad(_pad_heads(gla_w_gk[l], GLA_DK), ((0, LANE - GLA_LOW_RANK), (0, 0))).astype(BF16)
        bgk = _pad_heads(gla_b_gk[l], GLA_DK).reshape(1, 512)
        norms = jnp.stack([gla_norm_g[l], hgrn_norm_g[l], gdn_norm_g[l], ret_norm_g[l], ret_norm_b[l]]
                          + [jnp.zeros((HEAD_DV,), F32)] * 3)
        gdnp = jnp.stack([jnp.pad(gdn_a_log[l], (N_HEADS, LANE - 2 * N_HEADS)),
                          jnp.pad(gdn_dt_bias[l], (N_HEADS, LANE - 2 * N_HEADS))])
        o_all, s_gla, s_hg, s_gdn, s_ret, c_gdn = _mixers(
            l, proj, states_in, cgdn_in, cos_t, sin_t, mall, wgk, bgk, norms, hgrn_lb_logits,
            gdn_conv_w[l], gdnp, nb, seq)

        y = _merge(h, o_all, _pack_w_merge(w_in[l], merge_tn), w_branch[l].astype(BF16), merge_tn)
        x, h2 = _resid_matmul_norm(y, w_out[l].astype(BF16), x, norm_ffn_g[l], 512, BF16, True, "out_proj")

        wa = _pad_cols(w_ffn_in[l][:, :D_FF], D_FF_PAD).astype(BF16)
        wu = _pad_cols(w_ffn_in[l][:, D_FF:], D_FF_PAD).astype(BF16)
        cw = _pad_cols(ffn_conv_w[l], D_FF_PAD)
        cb = _pad_cols(ffn_conv_b[l].reshape(1, D_FF), D_FF_PAD)
        g_p, c_ffn_p = _ffn_in_prompt(h2[:tp], wa, wu, cw, cb, nb, ffn_tn)
        cache_in = jnp.pad(cache_ffn_conv[l], ((0, 0), (SUBLANE - (FFN_CONV - 1), 0), (0, D_FF_PAD - D_FF)))
        g_s, c_ffn_s = _ffn_in_sample(h2[tp:], wa, wu, cw, cb, cache_in, dseq, ffn_tn)
        g = jnp.concatenate([g_p, g_s], axis=0)
        w_dn = jnp.pad(w_ffn_out[l], ((0, D_FF_PAD - D_FF), (0, 0))).astype(BF16)
        if l + 1 < DEPTH:
            x, h = _resid_matmul_norm(g, w_dn, x, norm_mix_g[l + 1], 256, BF16, True, "ffn_out")
        else:
            y_all = _resid_matmul_norm(g, w_dn, x, norm_final_g, 256, F32, False, "ffn_out_final")[0]

        s_gla = jnp.swapaxes(s_gla, 2, 3)[:, :, :GLA_DK]
        s_hg = jnp.swapaxes(s_hg, 2, 3)
        c_ffn = jnp.concatenate([c_ffn_p, c_ffn_s], axis=0)[:, :, :D_FF]
        new_states.append((s_gla, s_hg, s_gdn, s_ret, c_gdn, c_ffn))

    y_prompt = y_all[:tp].reshape(nb, seq, d)
    y_sample = y_all[tp:].reshape(ns, dseq, d)
    p_out = tuple(jnp.stack([st[i][:nb] for st in new_states]) for i in range(6))
    s_out = tuple(jnp.stack([st[i][nb:] for st in new_states]) for i in range(6))
    return (y_prompt, y_sample) + p_out + s_out
```

```python
import functools
import math

import numpy as np
import jax
import jax.numpy as jnp
from jax import lax
from jax.experimental import pallas as pl
from jax.experimental.pallas import tpu as pltpu

F32 = jnp.float32
BF16 = jnp.bfloat16

D_MODEL = 2048
DEPTH = 2
PAST_LEN = 4096
L = 64
N_BRANCH = 4
BRANCH_WIDTH = 512
N_HEADS = 4
HEAD_DV = 128
GLA_DK = 64
GLA_LOW_RANK = 16
GLA_GATE_NORM = 16.0
GDN_CONV = 4
GDN_QKV = 1536
ROPE_BASE = 10000.0
D_FF = 5504
D_FF_PAD = 5632
FFN_CONV = 3
EPS = 1e-6
N_MIX_IN = 7704

LANE = 128
SUBLANE = 8
VMEM_LIMIT = 56 * 1024 * 1024

P_GLA_Q, P_GLA_K, P_GLA_V, P_GLA_G, P_GLA_LR = 0, 512, 1024, 1536, 2048
P_HG_Q, P_HG_F, P_HG_I, P_HG_G = 2176, 2688, 3200, 3712
P_GDN_QKV, P_GDN_Z, P_GDN_BA = 4224, 5760, 6272
P_RET_Q, P_RET_K, P_RET_V, P_RET_G = 6400, 6912, 7424, 7936
N_PACK = 8448
N_LEVELS = 6


def _dot(a, b):
    return jnp.dot(a, b, preferred_element_type=F32)


def _dot_nt(a, b):
    return lax.dot_general(a, b, (((1,), (1,)), ((), ())), preferred_element_type=F32)


def _dot_tn(a, b):
    return lax.dot_general(a, b, (((0,), (0,)), ((), ())), preferred_element_type=F32)


def _split3(x):
    hi = x.astype(BF16)
    r = x - hi.astype(F32)
    mid = r.astype(BF16)
    lo = (r - mid.astype(F32)).astype(BF16)
    return hi, mid, lo


def _split2(x):
    hi = x.astype(BF16)
    lo = (x - hi.astype(F32)).astype(BF16)
    return hi, lo


def _dot_hi(a, b):
    a_hi, a_lo = _split2(a)
    b_hi, b_lo = _split2(b)
    return _dot(a_hi, b_hi) + (_dot(a_hi, b_lo) + _dot(a_lo, b_hi))


def _sigmoid(x):
    return 1.0 / (1.0 + jnp.exp(-x))


def _silu(x):
    return x * _sigmoid(x)


def _softplus(x):
    return jnp.maximum(x, 0.0) + jnp.log(1.0 + jnp.exp(-jnp.abs(x)))


def _compiler_params(semantics):
    return pltpu.CompilerParams(dimension_semantics=semantics, vmem_limit_bytes=VMEM_LIMIT)


def _tile(n, pref):
    t = min(n, pref)
    while n % t:
        t //= 2
    return t


def _split_specs(tm, d, n_p):
    return (pl.BlockSpec((tm, d), lambda i: (jnp.minimum(i, n_p - 1), 0)),
            pl.BlockSpec((tm, d), lambda i: (jnp.maximum(i - n_p, 0), 0)))


def _rmsnorm_kernel(n_p, xp_ref, xs_ref, g_ref, h_ref):
    x = jnp.where(pl.program_id(0) < n_p, xp_ref[...], xs_ref[...])
    y = x * lax.rsqrt(jnp.mean(x * x, axis=-1, keepdims=True) + EPS)
    h_ref[...] = (y * g_ref[...]).astype(h_ref.dtype)


def _rmsnorm(xp, xs, g, out_dtype):
    d = xp.shape[1]
    m = xp.shape[0] + xs.shape[0]
    tm = _tile(math.gcd(xp.shape[0], xs.shape[0]), 512)
    n_p = xp.shape[0] // tm
    return pl.pallas_call(
        functools.partial(_rmsnorm_kernel, n_p),
        grid=(m // tm,),
        in_specs=[*_split_specs(tm, d, n_p), pl.BlockSpec((1, d), lambda i: (0, 0))],
        out_specs=pl.BlockSpec((tm, d), lambda i: (i, 0)),
        out_shape=jax.ShapeDtypeStruct((m, d), out_dtype),
        compiler_params=_compiler_params(("arbitrary",)),
        name="rmsnorm",
    )(xp, xs, g.reshape(1, d))


def _mm_kernel(a_ref, w_ref, o_ref):
    o_ref[...] = _dot(a_ref[...], w_ref[...]).astype(o_ref.dtype)


def _matmul(a, w, tn, out_dtype):
    m, k = a.shape
    n = w.shape[1]
    tm = _tile(m, 512)
    return pl.pallas_call(
        _mm_kernel,
        grid=(n // tn, m // tm),
        in_specs=[pl.BlockSpec((tm, k), lambda j, i: (i, 0)),
                  pl.BlockSpec((k, tn), lambda j, i: (0, j))],
        out_specs=pl.BlockSpec((tm, tn), lambda j, i: (i, j)),
        out_shape=jax.ShapeDtypeStruct((m, n), out_dtype),
        compiler_params=_compiler_params(("parallel", "parallel")),
        name="in_proj",
    )(a, w)


def _resid_kernel(n_p, split_in, final, a_ref, w_ref, *refs):
    n_x = 2 if split_in else 1
    x_refs, g_ref, out_refs = refs[:n_x], refs[n_x], refs[n_x + 1:]
    is_prompt = pl.program_id(0) < n_p
    x_old = jnp.where(is_prompt, x_refs[0][...], x_refs[1][...]) if split_in else x_refs[0][...]
    x = x_old + _dot(a_ref[...], w_ref[...])
    h = x * lax.rsqrt(jnp.mean(x * x, axis=-1, keepdims=True) + EPS) * g_ref[...]
    if final:
        hp_ref, hs_ref = out_refs

        @pl.when(is_prompt)
        def _():
            hp_ref[...] = h

        @pl.when(jnp.logical_not(is_prompt))
        def _():
            hs_ref[...] = h
    else:
        out_refs[0][...] = x
        out_refs[1][...] = h.astype(out_refs[1].dtype)


def _resid_matmul_norm(a, w, x, g, m_p, tm, final, name):
    m, k = a.shape
    d = w.shape[1]
    tm = _tile(math.gcd(m_p, m - m_p), tm)
    n_p = m_p // tm
    row = lambda i: (i, 0)
    split_in = isinstance(x, tuple)
    x_specs = list(_split_specs(tm, d, n_p)) if split_in else [pl.BlockSpec((tm, d), row)]
    x_args = list(x) if split_in else [x]
    if final:
        out_specs = list(_split_specs(tm, d, n_p))
        out_shape = [jax.ShapeDtypeStruct((m_p, d), F32), jax.ShapeDtypeStruct((m - m_p, d), F32)]
    else:
        out_specs = [pl.BlockSpec((tm, d), row), pl.BlockSpec((tm, d), row)]
        out_shape = [jax.ShapeDtypeStruct((m, d), F32), jax.ShapeDtypeStruct((m, d), BF16)]
    return pl.pallas_call(
        functools.partial(_resid_kernel, n_p, split_in, final),
        grid=(m // tm,),
        in_specs=[pl.BlockSpec((tm, k), row),
                  pl.BlockSpec((k, d), lambda i: (0, 0), pipeline_mode=pl.Buffered(1)),
                  *x_specs,
                  pl.BlockSpec((1, d), lambda i: (0, 0))],
        out_specs=out_specs,
        out_shape=out_shape,
        compiler_params=_compiler_params(("arbitrary",)),
        name=name,
    )(a, w, *x_args, g.reshape(1, d))


def _merge_kernel(h_ref, o_ref, wm_ref, wb_ref, y_ref):
    tn = y_ref.shape[1]
    gates = _dot(h_ref[...], wm_ref[...])
    acc = None
    for n in range(N_BRANCH):
        br = _dot(o_ref[:, n * BRANCH_WIDTH:(n + 1) * BRANCH_WIDTH], wb_ref[n])
        term = _sigmoid(gates[:, n * tn:(n + 1) * tn]) * br
        acc = term if acc is None else acc + term
    y_ref[...] = acc.astype(y_ref.dtype)


def _merge(h, o_all, wm_packed, wb, tn):
    m, d = h.shape
    tm = _tile(m, 1024)
    return pl.pallas_call(
        _merge_kernel,
        grid=(d // tn, m // tm),
        in_specs=[pl.BlockSpec((tm, d), lambda j, i: (i, 0)),
                  pl.BlockSpec((tm, d), lambda j, i: (i, 0)),
                  pl.BlockSpec((d, N_BRANCH * tn), lambda j, i: (0, j)),
                  pl.BlockSpec((N_BRANCH, BRANCH_WIDTH, tn), lambda j, i: (0, 0, j))],
        out_specs=pl.BlockSpec((tm, tn), lambda j, i: (i, j)),
        out_shape=jax.ShapeDtypeStruct((m, d), BF16),
        compiler_params=_compiler_params(("parallel", "parallel")),
        name="merge",
    )(h, o_all, wm_packed, wb)


def _ffn_gate(a, prev1, prev2, u, cw_ref, cb_ref):
    conv = cw_ref[2:3, :] * a + cw_ref[1:2, :] * prev1 + cw_ref[0:1, :] * prev2 + cb_ref[...]
    return (_silu(conv) * u)


def _ffn_in_kernel(n_p, tiles_per_stream, seq_s, h_ref, wa_ref, wu_ref, cw_ref, cb_ref, cin_ref,
                   g_ref, cache_p_ref, cache_s_ref, buf_ref):
    tm = h_ref.shape[0]
    i = pl.program_id(1)
    h = h_ref[...]
    a = _dot(h, wa_ref[...])
    u = _dot(h, wu_ref[...])

    @pl.when(i < n_p)
    def _():
        @pl.when(i % tiles_per_stream == 0)
        def _():
            buf_ref[0:SUBLANE, :] = jnp.zeros((SUBLANE, buf_ref.shape[1]), F32)

        buf_ref[SUBLANE:SUBLANE + tm, :] = a
        prev1 = buf_ref[SUBLANE - 1:SUBLANE - 1 + tm, :]
        prev2 = buf_ref[SUBLANE - 2:SUBLANE - 2 + tm, :]
        g_ref[...] = _ffn_gate(a, prev1, prev2, u, cw_ref, cb_ref).astype(g_ref.dtype)
        last2 = a[tm - 2:tm, :]
        cache_p_ref[0] = last2
        buf_ref[SUBLANE - 2:SUBLANE, :] = last2

    @pl.when(i >= n_p)
    def _():
        for s in range(tm // seq_s):
            rows = slice(s * seq_s, (s + 1) * seq_s)
            a_s = a[rows]
            buf_ref[0:SUBLANE, :] = cin_ref[s]
            buf_ref[SUBLANE:SUBLANE + seq_s, :] = a_s
            prev1 = buf_ref[SUBLANE - 1:SUBLANE - 1 + seq_s, :]
            prev2 = buf_ref[SUBLANE - 2:SUBLANE - 2 + seq_s, :]
            g_ref[rows, :] = _ffn_gate(a_s, prev1, prev2, u[rows], cw_ref, cb_ref).astype(g_ref.dtype)
            cache_s_ref[s] = a_s[seq_s - 2:seq_s, :]


def _ffn_in(h, wa, wu, cw, cb, cache_in, n_prompt_streams, prompt_seq, seq_s, tn):
    m, d = h.shape
    m_p = n_prompt_streams * prompt_seq
    n_s_streams = (m - m_p) // seq_s
    tm = _tile(math.gcd(prompt_seq, m - m_p), 1024)
    tiles_per_stream = prompt_seq // tm
    n_p = m_p // tm
    spt = tm // seq_s
    dff = wa.shape[1]
    s_blk = lambda j, i: (jnp.maximum(i - n_p, 0), 0, j)
    return pl.pallas_call(
        functools.partial(_ffn_in_kernel, n_p, tiles_per_stream, seq_s),
        grid=(dff // tn, m // tm),
        in_specs=[pl.BlockSpec((tm, d), lambda j, i: (i, 0)),
                  pl.BlockSpec((d, tn), lambda j, i: (0, j)),
                  pl.BlockSpec((d, tn), lambda j, i: (0, j)),
                  pl.BlockSpec((FFN_CONV, tn), lambda j, i: (0, j)),
                  pl.BlockSpec((1, tn), lambda j, i: (0, j)),
                  pl.BlockSpec((spt, SUBLANE, tn), s_blk)],
        out_specs=[pl.BlockSpec((tm, tn), lambda j, i: (i, j)),
                   pl.BlockSpec((1, FFN_CONV - 1, tn),
                                lambda j, i: (jnp.minimum(i, n_p - 1) // tiles_per_stream, 0, j)),
                   pl.BlockSpec((spt, FFN_CONV - 1, tn), s_blk)],
        out_shape=[jax.ShapeDtypeStruct((m, dff), BF16),
                   jax.ShapeDtypeStruct((n_prompt_streams, FFN_CONV - 1, dff), F32),
                   jax.ShapeDtypeStruct((n_s_streams, FFN_CONV - 1, dff), F32)],
        scratch_shapes=[pltpu.VMEM((SUBLANE + tm, tn), F32)],
        compiler_params=_compiler_params(("arbitrary", "arbitrary")),
        name="ffn_in",
    )(h, wa, wu, cw, cb, cache_in)


def _level_matrix():
    t = np.arange(L)[:, None]
    u = np.arange(L)[None, :]
    blocks = [(u <= t)]
    for l in range(N_LEVELS):
        n = 1 << l
        r = (t & ~(2 * n - 1)) + n - 1
        upper = (t & n) != 0
        blocks.append(np.where(upper, (u > r) & (u <= t), (u > t) & (u <= r)))
    blocks.append(u > t)
    return np.concatenate(blocks, axis=0).astype(np.float32)


def _pair_levels():
    r = np.arange(N_HEADS * L)
    x = r[:, None] ^ r[None, :]
    code = np.full(x.shape, -1, np.int32)
    code[x == 0] = 0
    for l in range(N_LEVELS):
        code[(x >> l) == 1] = l + 1
    code[r[None, :] > r[:, None]] = -1
    return code


def _stack(x):
    return jnp.concatenate([x[:, h * HEAD_DV:(h + 1) * HEAD_DV] for h in range(N_HEADS)], axis=0)


def _stack_cols(x, first):
    return jnp.concatenate([x[:, first + h:first + h + 1] for h in range(N_HEADS)], axis=0)


def _diag_blocks(y):
    return jnp.concatenate(
        [y[h * L:(h + 1) * L, h * HEAD_DV:(h + 1) * HEAD_DV] for h in range(N_HEADS)], axis=0)


def _block_diag(x):
    z = jnp.zeros((L, HEAD_DV), x.dtype)
    rows = [jnp.concatenate([x[h * L:(h + 1) * L] if j == h else z for j in range(N_HEADS)], axis=1)
            for h in range(N_HEADS)]
    return jnp.concatenate(rows, axis=0)


def _store_heads(o_ref, off, y):
    for h in range(N_HEADS):
        o_ref[:, off + h * HEAD_DV:off + (h + 1) * HEAD_DV] = y[h * L:(h + 1) * L].astype(o_ref.dtype)


def _head_rms(o, gain, gate):
    y = o * lax.rsqrt(jnp.mean(o * o, axis=-1, keepdims=True) + EPS)
    return y * gain * _silu(gate)


def _gla_type(q, k, v, g, gate, gain, st_ref, mall, lvl, o_ref, o_off):
    g_hi, g_mid, g_lo = _split3(g)
    d3 = _dot(mall, jnp.concatenate([g_hi, g_mid, g_lo], axis=1))
    w = g.shape[1]
    e_all = jnp.exp(d3[:, :w] + (d3[:, w:2 * w] + d3[:, 2 * w:]))

    def e_rows(r):
        return _stack(e_all[r * L:(r + 1) * L])

    qs, ks, vs = _stack(q), _stack(k), _stack(v)
    att = jnp.where(lvl == 0, _dot_nt(qs.astype(BF16), ks.astype(BF16)), 0.0)
    for l in range(N_LEVELS):
        e_l = e_rows(l + 1)
        a_l = _dot_nt((qs * e_l).astype(BF16), (ks * e_l).astype(BF16))
        att = jnp.where(lvl == l + 1, a_l, att)
    st = [st_ref[0, h] for h in range(N_HEADS)]
    st_rows = jnp.concatenate(st, axis=0).astype(BF16)
    vb = vs.astype(BF16)
    q_state = _diag_blocks(_dot_nt((qs * e_rows(0)).astype(BF16), st_rows))
    o = _dot(att.astype(BF16), vb) + q_state
    _store_heads(o_ref, o_off, _head_rms(o, gain, _stack(gate)))
    upd = _dot_tn(_block_diag(vb), (ks * e_rows(N_LEVELS + 1)).astype(BF16))
    for h in range(N_HEADS):
        decay_end = e_all[L - 1:L, h * HEAD_DV:(h + 1) * HEAD_DV]
        st_ref[0, h] = st[h] * decay_end + upd[h * HEAD_DV:(h + 1) * HEAD_DV]


def _mixer_kernel(layer, n_prompt_chunks, chunks_per_stream,
                  proj_ref, sgla_in, shg_in, sgdn_in, sret_in, cgdn_in,
                  cos_ref, sin_ref, mall_ref, lvl_ref, wgk_ref, bgk_ref, norms_ref, lb_ref,
                  convw_ref, gdnp_ref,
                  o_ref, sgla_ref, shg_ref, sgdn_ref, sret_ref, cgdn_ref,
                  xe_ref):
    c = pl.program_id(0)
    first = jnp.logical_or(c >= n_prompt_chunks, c % chunks_per_stream == 0)

    @pl.when(first)
    def _():
        sgla_ref[...] = sgla_in[...]
        shg_ref[...] = shg_in[...]
        sgdn_ref[...] = sgdn_in[...]
        sret_ref[...] = sret_in[...]
        xe_ref[0:SUBLANE, :] = cgdn_in[0]

    lvl = lvl_ref[...]
    incl = lvl >= 0
    mall = mall_ref[...]
    tri = mall[0:L]
    hl = N_HEADS * L

    def col(off, width=512):
        return proj_ref[:, off:off + width]

    xe_ref[SUBLANE:SUBLANE + L, :] = col(P_GDN_QKV, GDN_QKV)
    base = SUBLANE - (GDN_CONV - 1)
    conv = convw_ref[0:1, :] * xe_ref[base:base + L, :]
    for j in range(1, GDN_CONV):
        conv = conv + convw_ref[j:j + 1, :] * xe_ref[base + j:base + j + L, :]
    new_hist = xe_ref[SUBLANE + L - (GDN_CONV - 1):SUBLANE + L, :]
    cgdn_ref[0] = new_hist
    xe_ref[base:SUBLANE, :] = new_hist
    act = _silu(conv)
    ba = col(P_GDN_BA, LANE)
    beta = _stack_cols(_sigmoid(ba), 0)
    log_a = -jnp.exp(gdnp_ref[0:1, :]) * _softplus(ba + gdnp_ref[1:2, :])
    la_hi, la_mid, la_lo = _split3(log_a)
    b_all = _dot(tri, la_hi) + (_dot(tri, la_mid) + _dot(tri, la_lo))
    b_col = _stack_cols(b_all, N_HEADS)
    b_rows = b_all.T
    b_row = jnp.concatenate([b_rows[N_HEADS + h:N_HEADS + h + 1, :] for h in range(N_HEADS)], axis=1)
    b_end = jnp.concatenate(
        [jnp.broadcast_to(b_all[L - 1:L, N_HEADS + h:N_HEADS + h + 1], (L, 1)) for h in range(N_HEADS)], axis=0)
    dq, dk, dv = _stack(act[:, 0:512]), _stack(act[:, 512:1024]), _stack(act[:, 1024:1536])
    qn = dq * lax.rsqrt(jnp.sum(dq * dq, axis=-1, keepdims=True) + EPS) * HEAD_DV ** -0.5
    kn = dk * lax.rsqrt(jnp.sum(dk * dk, axis=-1, keepdims=True) + EPS)
    qb, kb = qn.astype(BF16), kn.astype(BF16)
    dec = jnp.exp(jnp.where(incl, b_col - b_row, -1e30))
    e_b = jnp.exp(b_col)
    st = [sgdn_ref[0, h] for h in range(N_HEADS)]
    st_cols = jnp.concatenate(st, axis=1).astype(BF16)
    kq_state = _dot(jnp.concatenate([kb, qb], axis=0), st_cols)
    k_state = _diag_blocks(kq_state[0:hl])
    q_state = _diag_blocks(kq_state[hl:2 * hl])
    m = jnp.where(lvl > 0, _dot_nt(kb, kb) * dec, 0.0) * beta
    u = beta * (dv - e_b * k_state)
    p = -m
    for j in range(N_LEVELS):
        u = u + _dot_hi(p, u)
        if j + 1 < N_LEVELS:
            p = _dot_hi(p, p)
    ub = u.astype(BF16)
    att = _dot_nt(qb, kb) * dec
    o = _dot(att.astype(BF16), ub) + e_b * q_state
    _store_heads(o_ref, 2 * BRANCH_WIDTH, _head_rms(o, norms_ref[2:3, :], _stack(col(P_GDN_Z))))
    upd = _dot_tn((kn * jnp.exp(b_end - b_col)).astype(BF16), _block_diag(ub))
    for h in range(N_HEADS):
        decay_end = jnp.exp(b_all[L - 1:L, N_HEADS + h:N_HEADS + h + 1])
        sgdn_ref[0, h] = decay_end * st[h] + upd[:, h * HEAD_DV:(h + 1) * HEAD_DV]

    z = _dot(col(P_GLA_LR, LANE).astype(BF16), wgk_ref[...]) + bgk_ref[...]
    g_gla = (jnp.minimum(z, 0.0) - jnp.log(1.0 + jnp.exp(-jnp.abs(z)))) * (1.0 / GLA_GATE_NORM)
    _gla_type(col(P_GLA_Q) * GLA_DK ** -0.5, col(P_GLA_K), col(P_GLA_V), g_gla, col(P_GLA_G),
              norms_ref[0:1, :], sgla_ref, mall, lvl, o_ref, 0)

    if layer == 0:
        lb = jnp.zeros((1, 512), F32)
    else:
        logits = lb_ref[...]
        ex = jnp.exp(logits - jnp.max(logits, axis=0, keepdims=True))
        soft = ex / jnp.sum(ex, axis=0, keepdims=True)
        lb = jnp.sum(soft[1:layer + 1], axis=0, keepdims=True)
    f = lb + (1.0 - lb) * _sigmoid(col(P_HG_F))
    _gla_type(_silu(col(P_HG_Q)), 1.0 - f, col(P_HG_I), jnp.log(f), col(P_HG_G),
              norms_ref[1:2, :], shg_ref, mall, lvl, o_ref, BRANCH_WIDTH)

    cos = jnp.concatenate([cos_ref[...]] * N_HEADS, axis=0)
    sin = jnp.concatenate([sin_ref[...]] * N_HEADS, axis=0)
    rq, rk = _stack(col(P_RET_Q)), _stack(col(P_RET_K))
    vb = _stack(col(P_RET_V)).astype(BF16)
    row = lax.broadcasted_iota(jnp.int32, (hl, 1), 0)
    t_col = jnp.bitwise_and(row, L - 1).astype(F32)
    lg = jnp.zeros((hl, 1), F32)
    for h in range(N_HEADS):
        lg = jnp.where(jnp.right_shift(row, N_LEVELS) == h, math.log(1.0 - 2.0 ** (-5.0 - h)), lg)
    s_row = lax.broadcasted_iota(jnp.int32, (1, hl), 1)
    t_minus_s = (row - s_row).astype(F32)
    dec = jnp.exp(jnp.where(incl, t_minus_s * lg, -1e30))
    qr = (rq * cos + pltpu.roll(rq, HEAD_DV // 2, axis=1) * sin).astype(BF16)
    kr = (rk * cos + pltpu.roll(rk, HEAD_DV // 2, axis=1) * sin) * HEAD_DV ** -0.5
    kb = kr.astype(BF16)
    st = [sret_ref[0, h] for h in range(N_HEADS)]
    st_cols = jnp.concatenate(st, axis=1).astype(BF16)
    q_state = _diag_blocks(_dot(qr, st_cols))
    att = _dot_nt(qr, kb) * dec
    o = _dot(att.astype(BF16), vb) + jnp.exp((t_col + 1.0) * lg) * q_state
    mu = jnp.mean(o, axis=-1, keepdims=True)
    oc = o - mu
    y = oc * lax.rsqrt(jnp.mean(oc * oc, axis=-1, keepdims=True) + EPS)
    y = (y * norms_ref[3:4, :] + norms_ref[4:5, :]) * _silu(_stack(col(P_RET_G)))
    _store_heads(o_ref, 3 * BRANCH_WIDTH, y)
    k_end = (kr * jnp.exp((L - 1.0 - t_col) * lg)).astype(BF16)
    upd = _dot_tn(k_end, _block_diag(vb))
    for h in range(N_HEADS):
        gamma_l = math.exp(L * math.log(1.0 - 2.0 ** (-5.0 - h)))
        sret_ref[0, h] = gamma_l * st[h] + upd[:, h * HEAD_DV:(h + 1) * HEAD_DV]


def _mixers(layer, proj, states_in, cgdn_in, cos_t, sin_t, mall, lvl, wgk, bgk, norms, lb_logits,
            convw, gdnp, n_prompt_streams, prompt_seq):
    t_total = proj.shape[0]
    n_chunks = t_total // L
    chunks_per_stream = prompt_seq // L
    n_prompt_chunks = n_prompt_streams * chunks_per_stream
    n_streams = states_in[0].shape[0]
    past_block = PAST_LEN // L

    def stream(c):
        return jnp.where(c < n_prompt_chunks, c // chunks_per_stream, c - n_prompt_chunks + n_prompt_streams)

    def pos_block(c):
        return jnp.where(c < n_prompt_chunks, c % chunks_per_stream, past_block)

    st_spec = pl.BlockSpec((1, N_HEADS, HEAD_DV, HEAD_DV), lambda c: (stream(c), 0, 0, 0))
    const = lambda shape: pl.BlockSpec(shape, lambda c: tuple(0 for _ in shape))
    st_shape = jax.ShapeDtypeStruct((n_streams, N_HEADS, HEAD_DV, HEAD_DV), F32)
    return pl.pallas_call(
        functools.partial(_mixer_kernel, layer, n_prompt_chunks, chunks_per_stream),
        grid=(n_chunks,),
        in_specs=[pl.BlockSpec((L, N_PACK), lambda c: (c, 0)),
                  st_spec, st_spec, st_spec, st_spec,
                  pl.BlockSpec((1, SUBLANE, GDN_QKV), lambda c: (stream(c), 0, 0)),
                  pl.BlockSpec((L, HEAD_DV), lambda c: (pos_block(c), 0)),
                  pl.BlockSpec((L, HEAD_DV), lambda c: (pos_block(c), 0)),
                  const(mall.shape), const(lvl.shape), const(wgk.shape), const(bgk.shape),
                  const(norms.shape), const(lb_logits.shape), const(convw.shape), const(gdnp.shape)],
        out_specs=[pl.BlockSpec((L, D_MODEL), lambda c: (c, 0)),
                   st_spec, st_spec, st_spec, st_spec,
                   pl.BlockSpec((1, GDN_CONV - 1, GDN_QKV), lambda c: (stream(c), 0, 0))],
        out_shape=[jax.ShapeDtypeStruct((t_total, D_MODEL), BF16),
                   st_shape, st_shape, st_shape, st_shape,
                   jax.ShapeDtypeStruct((n_streams, GDN_CONV - 1, GDN_QKV), F32)],
        scratch_shapes=[pltpu.VMEM((SUBLANE + L, GDN_QKV), F32)],
        compiler_params=_compiler_params(("arbitrary",)),
        name="mixers",
    )(proj, *states_in, cgdn_in, cos_t, sin_t, mall, lvl, wgk, bgk, norms, lb_logits, convw, gdnp)


def _pad_heads(w, dk):
    lead = w.shape[:-1]
    w = w.reshape(lead + (N_HEADS, dk))
    w = jnp.pad(w, [(0, 0)] * len(lead) + [(0, 0), (0, HEAD_DV - dk)])
    return w.reshape(lead + (N_HEADS * HEAD_DV,))


def _pad_cols(w, width):
    return jnp.pad(w, [(0, 0)] * (w.ndim - 1) + [(0, width - w.shape[-1])])


def _mix_pieces():
    pieces = []
    for h in range(N_HEADS):
        pieces.append((P_GLA_Q + h * HEAD_DV, h * GLA_DK, GLA_DK, HEAD_DV))
        pieces.append((P_GLA_K + h * HEAD_DV, 256 + h * GLA_DK, GLA_DK, HEAD_DV))
    pieces += [(P_GLA_V, 512, 512, 512), (P_GLA_G, 1024, 512, 512), (P_GLA_LR, 1536, GLA_LOW_RANK, LANE),
               (P_HG_Q, 1552, 512, 512), (P_HG_F, 2064, 512, 512), (P_HG_I, 2576, 512, 512),
               (P_HG_G, 3088, 512, 512), (P_GDN_QKV, 3600, GDN_QKV, GDN_QKV), (P_GDN_Z, 5136, 512, 512),
               (P_GDN_BA, 5648, 2 * N_HEADS, LANE), (P_RET_Q, 5656, 512, 512), (P_RET_K, 6168, 512, 512),
               (P_RET_V, 6680, 512, 512), (P_RET_G, 7192, 512, 512)]
    return pieces


def _pack_w_in_kernel(merge_tn, w_ref, mix_ref, wm_ref):
    n_in = w_ref.shape[1]

    def window(off, n):
        a0 = off // LANE * LANE
        a1 = min(-(-(off + n) // LANE) * LANE, n_in)
        return w_ref[:, a0:a1][:, off - a0:off - a0 + n].astype(BF16)

    for dst, src, n, padded in _mix_pieces():
        if padded != n:
            mix_ref[:, dst:dst + padded] = jnp.zeros((mix_ref.shape[0], padded), BF16)
        mix_ref[:, dst:dst + n] = window(src, n)
    for j in range(D_MODEL // merge_tn):
        for n in range(N_BRANCH):
            dst = (j * N_BRANCH + n) * merge_tn
            wm_ref[:, dst:dst + merge_tn] = window(N_MIX_IN + n * D_MODEL + j * merge_tn, merge_tn)


def _pack_w_in(w, merge_tn):
    k, n_in = w.shape
    tk = 128
    return pl.pallas_call(
        functools.partial(_pack_w_in_kernel, merge_tn),
        grid=(k // tk,),
        in_specs=[pl.BlockSpec((tk, n_in), lambda i: (i, 0))],
        out_specs=[pl.BlockSpec((tk, N_PACK), lambda i: (i, 0)),
                   pl.BlockSpec((tk, N_BRANCH * D_MODEL), lambda i: (i, 0))],
        out_shape=[jax.ShapeDtypeStruct((k, N_PACK), BF16),
                   jax.ShapeDtypeStruct((k, N_BRANCH * D_MODEL), BF16)],
        compiler_params=_compiler_params(("parallel",)),
        name="pack_w_in",
    )(w)


def _rope_tables(n_pos):
    pos = jnp.arange(n_pos).astype(F32)
    inv = 1.0 / (ROPE_BASE ** (jnp.arange(0, HEAD_DV, 2, dtype=F32) / HEAD_DV))
    ang = pos[:, None] * inv[None, :]
    cos, sin = jnp.cos(ang), jnp.sin(ang)
    return jnp.concatenate([cos, cos], axis=1), jnp.concatenate([-sin, sin], axis=1)


def kernel(x_prompt, x_sample, state_gla, state_hgrn, state_gdn, state_ret, cache_gdn_conv, cache_ffn_conv, norm_mix_g, w_in, gla_w_gk, gla_b_gk, gla_norm_g, hgrn_lb_logits, hgrn_norm_g, gdn_conv_w, gdn_a_log, gdn_dt_bias, gdn_norm_g, ret_norm_g, ret_norm_b, w_branch, w_out, norm_ffn_g, w_ffn_in, ffn_conv_w, ffn_conv_b, w_ffn_out, norm_final_g):
    nb, seq, d = x_prompt.shape
    ns, dseq, _ = x_sample.shape
    assert d == D_MODEL and dseq == L and seq % L == 0
    tp = nb * seq
    x = (x_prompt.reshape(tp, d), x_sample.reshape(ns * dseq, d))

    cos_t, sin_t = _rope_tables(max(seq, PAST_LEN + dseq))
    mall = jnp.asarray(_level_matrix(), BF16)
    lvl = jnp.asarray(_pair_levels())
    merge_tn = 512
    ffn_tn = 512

    def with_prompt(s, transpose=False, pad_k=0):
        if pad_k:
            s = jnp.pad(s, ((0, 0), (0, 0), (0, pad_k), (0, 0)))
        if transpose:
            s = jnp.swapaxes(s, 2, 3)
        return jnp.concatenate([jnp.zeros((nb,) + s.shape[1:], F32), s], axis=0)

    h = _rmsnorm(x[0], x[1], norm_mix_g[0], BF16)
    new_states = []
    for l in range(DEPTH):
        w_mix, w_merge = _pack_w_in(w_in[l], merge_tn)
        proj = _matmul(h, w_mix, 2816, F32)

        states_in = (with_prompt(state_gla[l], transpose=True, pad_k=HEAD_DV - GLA_DK),
                     with_prompt(state_hgrn[l], transpose=True),
                     with_prompt(state_gdn[l]), with_prompt(state_ret[l]))
        cgdn_in = jnp.concatenate([jnp.zeros((nb, GDN_CONV - 1, GDN_QKV), F32), cache_gdn_conv[l]], axis=0)
        cgdn_in = jnp.pad(cgdn_in, ((0, 0), (SUBLANE - (GDN_CONV - 1), 0), (0, 0)))
        wgk = jnp.pad(_pad_heads(gla_w_gk[l], GLA_DK), ((0, LANE - GLA_LOW_RANK), (0, 0))).astype(BF16)
        bgk = _pad_heads(gla_b_gk[l], GLA_DK).reshape(1, 512)
        norms = jnp.stack([gla_norm_g[l], hgrn_norm_g[l], gdn_norm_g[l], ret_norm_g[l], ret_norm_b[l]]
                          + [jnp.zeros((HEAD_DV,), F32)] * 3)
        gdnp = jnp.stack([jnp.pad(gdn_a_log[l], (N_HEADS, LANE - 2 * N_HEADS)),
                          jnp.pad(gdn_dt_bias[l], (N_HEADS, LANE - 2 * N_HEADS))])
        o_all, s_gla, s_hg, s_gdn, s_ret, c_gdn = _mixers(
            l, proj, states_in, cgdn_in, cos_t, sin_t, mall, lvl, wgk, bgk, norms, hgrn_lb_logits,
            gdn_conv_w[l], gdnp, nb, seq)

        y = _merge(h, o_all, w_merge, w_branch[l].astype(BF16), merge_tn)
        x, h2 = _resid_matmul_norm(y, w_out[l].astype(BF16), x, norm_ffn_g[l], tp, 512, False, "out_proj")

        wa = _pad_cols(w_ffn_in[l][:, :D_FF], D_FF_PAD).astype(BF16)
        wu = _pad_cols(w_ffn_in[l][:, D_FF:], D_FF_PAD).astype(BF16)
        cw = _pad_cols(ffn_conv_w[l], D_FF_PAD)
        cb = _pad_cols(ffn_conv_b[l].reshape(1, D_FF), D_FF_PAD)
        cache_in = jnp.pad(cache_ffn_conv[l], ((0, 0), (SUBLANE - (FFN_CONV - 1), 0), (0, D_FF_PAD - D_FF)))
        g, c_ffn_p, c_ffn_s = _ffn_in(h2, wa, wu, cw, cb, cache_in, nb, seq, dseq, ffn_tn)
        w_dn = jnp.pad(w_ffn_out[l], ((0, D_FF_PAD - D_FF), (0, 0))).astype(BF16)
        if l + 1 < DEPTH:
            x, h = _resid_matmul_norm(g, w_dn, x, norm_mix_g[l + 1], tp, 256, False, "ffn_out")
        else:
            y_p, y_s = _resid_matmul_norm(g, w_dn, x, norm_final_g, tp, 256, True, "ffn_out_final")

        s_gla = jnp.swapaxes(s_gla, 2, 3)[:, :, :GLA_DK]
        s_hg = jnp.swapaxes(s_hg, 2, 3)
        c_ffn = jnp.concatenate([c_ffn_p, c_ffn_s], axis=0)[:, :, :D_FF]
        new_states.append((s_gla, s_hg, s_gdn, s_ret, c_gdn, c_ffn))

    y_prompt = y_p.reshape(nb, seq, d)
    y_sample = y_s.reshape(ns, dseq, d)
    p_out = tuple(jnp.stack([st[i][:nb] for st in new_states]) for i in range(6))
    s_out = tuple(jnp.stack([st[i][nb:] for st in new_states]) for i in range(6))
    return (y_prompt, y_sample) + p_out + s_out
```

```python
import functools
import math

import numpy as np
import jax
import jax.numpy as jnp
from jax import lax
from jax.experimental import pallas as pl
from jax.experimental.pallas import tpu as pltpu

F32 = jnp.float32
BF16 = jnp.bfloat16

D_MODEL = 2048
DEPTH = 2
PAST_LEN = 4096
L = 64
N_BRANCH = 4
BRANCH_WIDTH = 512
N_HEADS = 4
HEAD_DV = 128
GLA_DK = 64
GLA_LOW_RANK = 16
GLA_GATE_NORM = 16.0
GDN_CONV = 4
GDN_QKV = 1536
ROPE_BASE = 10000.0
D_FF = 5504
D_FF_PAD = 5632
FFN_CONV = 3
EPS = 1e-6
N_MIX_IN = 7704

LANE = 128
SUBLANE = 8
VMEM_LIMIT = 56 * 1024 * 1024

P_GLA_Q, P_GLA_K, P_GLA_V, P_GLA_G, P_GLA_LR = 0, 512, 1024, 1536, 2048
P_HG_Q, P_HG_F, P_HG_I, P_HG_G = 2176, 2688, 3200, 3712
P_GDN_QKV, P_GDN_Z, P_GDN_BA = 4224, 5760, 6272
P_RET_Q, P_RET_K, P_RET_V, P_RET_G = 6400, 6912, 7424, 7936
N_PACK = 8448
N_LEVELS = 6
GROUP = 2
LOG2_E = math.log2(math.e)


def _dot(a, b):
    return jnp.dot(a, b, preferred_element_type=F32)


def _dot_nt(a, b):
    return lax.dot_general(a, b, (((1,), (1,)), ((), ())), preferred_element_type=F32)


def _dot_tn(a, b):
    return lax.dot_general(a, b, (((0,), (0,)), ((), ())), preferred_element_type=F32)


def _split3(x):
    hi = x.astype(BF16)
    r = x - hi.astype(F32)
    mid = r.astype(BF16)
    lo = (r - mid.astype(F32)).astype(BF16)
    return hi, mid, lo


def _split2(x):
    hi = x.astype(BF16)
    lo = (x - hi.astype(F32)).astype(BF16)
    return hi, lo


def _dot_hi(a, b):
    a_hi, a_lo = _split2(a)
    b_hi, b_lo = _split2(b)
    return _dot(a_hi, b_hi) + (_dot(a_hi, b_lo) + _dot(a_lo, b_hi))


def _sigmoid(x):
    return 1.0 / (1.0 + jnp.exp(-x))


def _silu(x):
    return x * _sigmoid(x)


def _softplus(x):
    return jnp.maximum(x, 0.0) + jnp.log(1.0 + jnp.exp(-jnp.abs(x)))


def _compiler_params(semantics):
    return pltpu.CompilerParams(dimension_semantics=semantics, vmem_limit_bytes=VMEM_LIMIT)


def _tile(n, pref):
    t = min(n, pref)
    while n % t:
        t //= 2
    return t


def _split_specs(tm, d, n_p):
    return (pl.BlockSpec((tm, d), lambda i: (jnp.minimum(i, n_p - 1), 0)),
            pl.BlockSpec((tm, d), lambda i: (jnp.maximum(i - n_p, 0), 0)))


def _rmsnorm_kernel(n_p, xp_ref, xs_ref, g_ref, h_ref):
    x = jnp.where(pl.program_id(0) < n_p, xp_ref[...], xs_ref[...])
    y = x * lax.rsqrt(jnp.mean(x * x, axis=-1, keepdims=True) + EPS)
    h_ref[...] = (y * g_ref[...]).astype(h_ref.dtype)


def _rmsnorm(xp, xs, g, out_dtype):
    d = xp.shape[1]
    m = xp.shape[0] + xs.shape[0]
    tm = _tile(math.gcd(xp.shape[0], xs.shape[0]), 512)
    n_p = xp.shape[0] // tm
    return pl.pallas_call(
        functools.partial(_rmsnorm_kernel, n_p),
        grid=(m // tm,),
        in_specs=[*_split_specs(tm, d, n_p), pl.BlockSpec((1, d), lambda i: (0, 0))],
        out_specs=pl.BlockSpec((tm, d), lambda i: (i, 0)),
        out_shape=jax.ShapeDtypeStruct((m, d), out_dtype),
        compiler_params=_compiler_params(("arbitrary",)),
        name="rmsnorm",
    )(xp, xs, g.reshape(1, d))


def _mm_kernel(a_ref, w_ref, o_ref):
    o_ref[...] = _dot(a_ref[...], w_ref[...]).astype(o_ref.dtype)


def _matmul(a, w, tn, out_dtype):
    m, k = a.shape
    n = w.shape[1]
    tm = _tile(m, 512)
    return pl.pallas_call(
        _mm_kernel,
        grid=(n // tn, m // tm),
        in_specs=[pl.BlockSpec((tm, k), lambda j, i: (i, 0)),
                  pl.BlockSpec((k, tn), lambda j, i: (0, j))],
        out_specs=pl.BlockSpec((tm, tn), lambda j, i: (i, j)),
        out_shape=jax.ShapeDtypeStruct((m, n), out_dtype),
        compiler_params=_compiler_params(("parallel", "parallel")),
        name="in_proj",
    )(a, w)


def _resid_kernel(n_p, split_in, final, a_ref, w_ref, *refs):
    n_x = 2 if split_in else 1
    x_refs, g_ref, out_refs = refs[:n_x], refs[n_x], refs[n_x + 1:]
    is_prompt = pl.program_id(0) < n_p
    x_old = jnp.where(is_prompt, x_refs[0][...], x_refs[1][...]) if split_in else x_refs[0][...]
    x = x_old + _dot(a_ref[...], w_ref[...])
    h = x * lax.rsqrt(jnp.mean(x * x, axis=-1, keepdims=True) + EPS) * g_ref[...]
    if final:
        hp_ref, hs_ref = out_refs

        @pl.when(is_prompt)
        def _():
            hp_ref[...] = h

        @pl.when(jnp.logical_not(is_prompt))
        def _():
            hs_ref[...] = h
    else:
        out_refs[0][...] = x
        out_refs[1][...] = h.astype(out_refs[1].dtype)


def _resid_matmul_norm(a, w, x, g, m_p, tm, final, name):
    m, k = a.shape
    d = w.shape[1]
    tm = _tile(math.gcd(m_p, m - m_p), tm)
    n_p = m_p // tm
    row = lambda i: (i, 0)
    split_in = isinstance(x, tuple)
    x_specs = list(_split_specs(tm, d, n_p)) if split_in else [pl.BlockSpec((tm, d), row)]
    x_args = list(x) if split_in else [x]
    if final:
        out_specs = list(_split_specs(tm, d, n_p))
        out_shape = [jax.ShapeDtypeStruct((m_p, d), F32), jax.ShapeDtypeStruct((m - m_p, d), F32)]
    else:
        out_specs = [pl.BlockSpec((tm, d), row), pl.BlockSpec((tm, d), row)]
        out_shape = [jax.ShapeDtypeStruct((m, d), F32), jax.ShapeDtypeStruct((m, d), BF16)]
    return pl.pallas_call(
        functools.partial(_resid_kernel, n_p, split_in, final),
        grid=(m // tm,),
        in_specs=[pl.BlockSpec((tm, k), row),
                  pl.BlockSpec((k, d), lambda i: (0, 0), pipeline_mode=pl.Buffered(1)),
                  *x_specs,
                  pl.BlockSpec((1, d), lambda i: (0, 0))],
        out_specs=out_specs,
        out_shape=out_shape,
        compiler_params=_compiler_params(("arbitrary",)),
        name=name,
    )(a, w, *x_args, g.reshape(1, d))


def _merge_kernel(h_ref, o_ref, wm_ref, wb_ref, y_ref):
    tn = y_ref.shape[1]
    gates = _dot(h_ref[...], wm_ref[...])
    acc = None
    for n in range(N_BRANCH):
        br = _dot(o_ref[:, n * BRANCH_WIDTH:(n + 1) * BRANCH_WIDTH], wb_ref[n])
        term = _sigmoid(gates[:, n * tn:(n + 1) * tn]) * br
        acc = term if acc is None else acc + term
    y_ref[...] = acc.astype(y_ref.dtype)


def _merge(h, o_all, wm_packed, wb, tn):
    m, d = h.shape
    tm = _tile(m, 1024)
    return pl.pallas_call(
        _merge_kernel,
        grid=(d // tn, m // tm),
        in_specs=[pl.BlockSpec((tm, d), lambda j, i: (i, 0)),
                  pl.BlockSpec((tm, d), lambda j, i: (i, 0)),
                  pl.BlockSpec((d, N_BRANCH * tn), lambda j, i: (0, j)),
                  pl.BlockSpec((N_BRANCH, BRANCH_WIDTH, tn), lambda j, i: (0, 0, j))],
        out_specs=pl.BlockSpec((tm, tn), lambda j, i: (i, j)),
        out_shape=jax.ShapeDtypeStruct((m, d), BF16),
        compiler_params=_compiler_params(("parallel", "parallel")),
        name="merge",
    )(h, o_all, wm_packed, wb)


def _ffn_gate(a, prev1, prev2, u, cw_ref, cb_ref):
    conv = cw_ref[2:3, :] * a + cw_ref[1:2, :] * prev1 + cw_ref[0:1, :] * prev2 + cb_ref[...]
    return (_silu(conv) * u)


def _ffn_in_kernel(n_p, tiles_per_stream, seq_s, h_ref, wa_ref, wu_ref, cw_ref, cb_ref, cin_ref,
                   g_ref, cache_p_ref, cache_s_ref, buf_ref):
    tm = h_ref.shape[0]
    i = pl.program_id(1)

    def up_project():
        h = h_ref[...]
        return _dot(h, wa_ref[...]), _dot(h, wu_ref[...])

    @pl.when(jnp.logical_and(i < n_p, i % tiles_per_stream == 0))
    def _():
        buf_ref[0:SUBLANE, :] = jnp.zeros((SUBLANE, buf_ref.shape[1]), F32)

    @pl.when(i < n_p)
    def _():
        a, u = up_project()
        buf_ref[SUBLANE:SUBLANE + tm, :] = a
        prev1 = buf_ref[SUBLANE - 1:SUBLANE - 1 + tm, :]
        prev2 = buf_ref[SUBLANE - 2:SUBLANE - 2 + tm, :]
        g_ref[...] = _ffn_gate(a, prev1, prev2, u, cw_ref, cb_ref).astype(g_ref.dtype)
        last2 = a[tm - 2:tm, :]
        cache_p_ref[0] = last2
        buf_ref[SUBLANE - 2:SUBLANE, :] = last2

    @pl.when(i >= n_p)
    def _():
        a, u = up_project()
        for s in range(tm // seq_s):
            rows = slice(s * seq_s, (s + 1) * seq_s)
            a_s = a[rows]
            buf_ref[0:SUBLANE, :] = cin_ref[s]
            buf_ref[SUBLANE:SUBLANE + seq_s, :] = a_s
            prev1 = buf_ref[SUBLANE - 1:SUBLANE - 1 + seq_s, :]
            prev2 = buf_ref[SUBLANE - 2:SUBLANE - 2 + seq_s, :]
            g_ref[rows, :] = _ffn_gate(a_s, prev1, prev2, u[rows], cw_ref, cb_ref).astype(g_ref.dtype)
            cache_s_ref[s] = a_s[seq_s - 2:seq_s, :]


def _ffn_in(h, wa, wu, cw, cb, cache_in, n_prompt_streams, prompt_seq, seq_s, tn):
    m, d = h.shape
    m_p = n_prompt_streams * prompt_seq
    n_s_streams = (m - m_p) // seq_s
    tm = _tile(math.gcd(prompt_seq, m - m_p), 1024)
    tiles_per_stream = prompt_seq // tm
    n_p = m_p // tm
    spt = tm // seq_s
    dff = wa.shape[1]
    s_blk = lambda j, i: (jnp.maximum(i - n_p, 0), 0, j)
    return pl.pallas_call(
        functools.partial(_ffn_in_kernel, n_p, tiles_per_stream, seq_s),
        grid=(dff // tn, m // tm),
        in_specs=[pl.BlockSpec((tm, d), lambda j, i: (i, 0)),
                  pl.BlockSpec((d, tn), lambda j, i: (0, j)),
                  pl.BlockSpec((d, tn), lambda j, i: (0, j)),
                  pl.BlockSpec((FFN_CONV, tn), lambda j, i: (0, j)),
                  pl.BlockSpec((1, tn), lambda j, i: (0, j)),
                  pl.BlockSpec((spt, SUBLANE, tn), s_blk)],
        out_specs=[pl.BlockSpec((tm, tn), lambda j, i: (i, j)),
                   pl.BlockSpec((1, FFN_CONV - 1, tn),
                                lambda j, i: (jnp.minimum(i, n_p - 1) // tiles_per_stream, 0, j)),
                   pl.BlockSpec((spt, FFN_CONV - 1, tn), s_blk)],
        out_shape=[jax.ShapeDtypeStruct((m, dff), BF16),
                   jax.ShapeDtypeStruct((n_prompt_streams, FFN_CONV - 1, dff), F32),
                   jax.ShapeDtypeStruct((n_s_streams, FFN_CONV - 1, dff), F32)],
        scratch_shapes=[pltpu.VMEM((SUBLANE + tm, tn), F32)],
        compiler_params=_compiler_params(("arbitrary", "arbitrary")),
        name="ffn_in",
    )(h, wa, wu, cw, cb, cache_in)


def _level_matrix():
    t = np.arange(L)[:, None]
    u = np.arange(L)[None, :]
    blocks = [(u <= t)]
    for l in range(N_LEVELS):
        n = 1 << l
        r = (t & ~(2 * n - 1)) + n - 1
        upper = (t & n) != 0
        blocks.append(np.where(upper, (u > r) & (u <= t), (u > t) & (u <= r)))
    blocks.append(u > t)
    return np.concatenate(blocks, axis=0).astype(np.float32)


def _pair_levels():
    r = np.arange(GROUP * L)
    x = r[:, None] ^ r[None, :]
    code = np.full(x.shape, -1, np.int32)
    code[x == 0] = 0
    for l in range(N_LEVELS):
        code[(x >> l) == 1] = l + 1
    code[r[None, :] > r[:, None]] = -1
    return code


def _stack(x):
    return jnp.concatenate([x[:, h * HEAD_DV:(h + 1) * HEAD_DV] for h in range(N_HEADS)], axis=0)


def _stack_cols(x, first):
    return jnp.concatenate([x[:, first + h:first + h + 1] for h in range(N_HEADS)], axis=0)


def _groups(x):
    gl = GROUP * L
    return [x[g * gl:(g + 1) * gl] for g in range(N_HEADS // GROUP)]


def _pair_product(a, b):
    return [_dot_nt(ag, bg) for ag, bg in zip(_groups(a), _groups(b))]


def _block_diag(x):
    z = jnp.zeros((L, HEAD_DV), x.dtype)
    rows = [jnp.concatenate([x[h * L:(h + 1) * L] if j == h else z for j in range(N_HEADS)], axis=1)
            for h in range(N_HEADS)]
    return jnp.concatenate(rows, axis=0)


def _store_heads(o_ref, off, y):
    for h in range(N_HEADS):
        o_ref[:, off + h * HEAD_DV:off + (h + 1) * HEAD_DV] = y[h * L:(h + 1) * L].astype(o_ref.dtype)


def _head_rms(o, gain, gate):
    y = o * lax.rsqrt(jnp.mean(o * o, axis=-1, keepdims=True) + EPS)
    return y * gain * _silu(gate)


def _gla_type(q, k, v, g, gate, gain, st_ref, mall, lvl, o_ref, o_off):
    g_hi, g_lo = _split2(g * LOG2_E)
    d2 = _dot(mall, jnp.concatenate([g_hi, g_lo], axis=1))
    w = g.shape[1]
    e_all = jnp.exp2(d2[:, :w] + d2[:, w:])

    def e_rows(r):
        return _stack(e_all[r * L:(r + 1) * L])

    qs, ks, vs = _stack(q), _stack(k), _stack(v)
    att = [jnp.where(lvl == 0, a, 0.0) for a in _pair_product(qs.astype(BF16), ks.astype(BF16))]
    for l in range(N_LEVELS):
        yield
        e_l = e_rows(l + 1)
        a_l = _pair_product((qs * e_l).astype(BF16), (ks * e_l).astype(BF16))
        att = [jnp.where(lvl == l + 1, a, prev) for a, prev in zip(a_l, att)]
    yield
    st = [st_ref[0, h] for h in range(N_HEADS)]
    vb = vs.astype(BF16)
    q_decayed = (qs * e_rows(0)).astype(BF16)
    q_state = jnp.concatenate(
        [_dot_nt(q_decayed[h * L:(h + 1) * L], st[h].astype(BF16)) for h in range(N_HEADS)], axis=0)
    o = jnp.concatenate([_dot(a.astype(BF16), vg) for a, vg in zip(att, _groups(vb))], axis=0) + q_state
    _store_heads(o_ref, o_off, _head_rms(o, gain, _stack(gate)))
    upd = _dot_tn(_block_diag(vb), (ks * e_rows(N_LEVELS + 1)).astype(BF16))
    for h in range(N_HEADS):
        decay_end = e_all[L - 1:L, h * HEAD_DV:(h + 1) * HEAD_DV]
        st_ref[0, h] = st[h] * decay_end + upd[h * HEAD_DV:(h + 1) * HEAD_DV]


def _mixer_kernel(layer, n_prompt_chunks, chunks_per_stream,
                  proj_ref, sgla_in, shg_in, sgdn_in, sret_in, cgdn_in,
                  cos_ref, sin_ref, mall_ref, lvl_ref, wgk_ref, bgk_ref, norms_ref, lb_ref,
                  convw_ref, gdnp_ref,
                  o_ref, sgla_ref, shg_ref, sgdn_ref, sret_ref, cgdn_ref,
                  xe_ref):
    c = pl.program_id(0)
    first = jnp.logical_or(c >= n_prompt_chunks, c % chunks_per_stream == 0)

    @pl.when(first)
    def _():
        sgla_ref[...] = sgla_in[...]
        shg_ref[...] = shg_in[...]
        sgdn_ref[...] = sgdn_in[...]
        sret_ref[...] = sret_in[...]
        xe_ref[0:SUBLANE, :] = cgdn_in[0]

    lvl = lvl_ref[...]
    incl = lvl >= 0
    mall = mall_ref[...]
    tri = mall[0:L]
    hl = N_HEADS * L

    def col(off, width=512):
        return proj_ref[:, off:off + width]

    z = _dot(col(P_GLA_LR, LANE).astype(BF16), wgk_ref[...]) + bgk_ref[...]
    g_gla = (jnp.minimum(z, 0.0) - jnp.log(1.0 + jnp.exp(-jnp.abs(z)))) * (1.0 / GLA_GATE_NORM)
    gla = _gla_type(col(P_GLA_Q) * GLA_DK ** -0.5, col(P_GLA_K), col(P_GLA_V), g_gla, col(P_GLA_G),
                    norms_ref[0:1, :], sgla_ref, mall, lvl, o_ref, 0)
    if layer == 0:
        lb = jnp.zeros((1, 512), F32)
    else:
        logits = lb_ref[...]
        ex = jnp.exp(logits - jnp.max(logits, axis=0, keepdims=True))
        soft = ex / jnp.sum(ex, axis=0, keepdims=True)
        lb = jnp.sum(soft[1:layer + 1], axis=0, keepdims=True)
    f = lb + (1.0 - lb) * _sigmoid(col(P_HG_F))
    hgrn = _gla_type(_silu(col(P_HG_Q)), 1.0 - f, col(P_HG_I), jnp.log(f), col(P_HG_G),
                     norms_ref[1:2, :], shg_ref, mall, lvl, o_ref, BRANCH_WIDTH)

    xe_ref[SUBLANE:SUBLANE + L, :] = col(P_GDN_QKV, GDN_QKV)
    base = SUBLANE - (GDN_CONV - 1)
    conv = convw_ref[0:1, :] * xe_ref[base:base + L, :]
    for j in range(1, GDN_CONV):
        conv = conv + convw_ref[j:j + 1, :] * xe_ref[base + j:base + j + L, :]
    new_hist = xe_ref[SUBLANE + L - (GDN_CONV - 1):SUBLANE + L, :]
    cgdn_ref[0] = new_hist
    xe_ref[base:SUBLANE, :] = new_hist
    act = _silu(conv)
    ba = col(P_GDN_BA, LANE)
    beta = _stack_cols(_sigmoid(ba), 0)
    log_a = -jnp.exp(gdnp_ref[0:1, :]) * _softplus(ba + gdnp_ref[1:2, :])
    la_hi, la_mid, la_lo = _split3(log_a)
    b_all = _dot(tri, la_hi) + (_dot(tri, la_mid) + _dot(tri, la_lo))
    b_col = _stack_cols(b_all, N_HEADS)
    b_rows = b_all.T
    b_row = [jnp.concatenate([b_rows[N_HEADS + GROUP * g + i:N_HEADS + GROUP * g + i + 1, :]
                              for i in range(GROUP)], axis=1) for g in range(N_HEADS // GROUP)]
    b_end = jnp.concatenate(
        [jnp.broadcast_to(b_all[L - 1:L, N_HEADS + h:N_HEADS + h + 1], (L, 1)) for h in range(N_HEADS)], axis=0)
    dq, dk, dv = _stack(act[:, 0:512]), _stack(act[:, 512:1024]), _stack(act[:, 1024:1536])
    qn = dq * lax.rsqrt(jnp.sum(dq * dq, axis=-1, keepdims=True) + EPS) * HEAD_DV ** -0.5
    kn = dk * lax.rsqrt(jnp.sum(dk * dk, axis=-1, keepdims=True) + EPS)
    qb, kb = qn.astype(BF16), kn.astype(BF16)
    dec = [jnp.exp(jnp.where(incl, bc - br, -1e30)) for bc, br in zip(_groups(b_col), b_row)]
    e_b = jnp.exp(b_col)
    st = [sgdn_ref[0, h] for h in range(N_HEADS)]
    kq_state = [_dot(jnp.concatenate([kb[h * L:(h + 1) * L], qb[h * L:(h + 1) * L]], axis=0),
                     st[h].astype(BF16)) for h in range(N_HEADS)]
    k_state = jnp.concatenate([x[0:L] for x in kq_state], axis=0)
    q_state = jnp.concatenate([x[L:2 * L] for x in kq_state], axis=0)
    rhs = beta * (dv - e_b * k_state)
    p_groups = [-(jnp.where(lvl > 0, kk * dec_g, 0.0) * beta_g)
                for kk, dec_g, beta_g in zip(_pair_product(kb, kb), dec, _groups(beta))]
    u_groups = _groups(rhs)
    next(gla)
    next(hgrn)
    for j in range(N_LEVELS):
        u_groups = [u + _dot_hi(p, u) for p, u in zip(p_groups, u_groups)]
        if j + 1 < N_LEVELS:
            p_groups = [_dot_hi(p, p) for p in p_groups]
        next(gla)
        next(hgrn)
    ub = jnp.concatenate(u_groups, axis=0).astype(BF16)
    att = [(qk * dec_g).astype(BF16) for qk, dec_g in zip(_pair_product(qb, kb), dec)]
    o = jnp.concatenate([_dot(a, ug) for a, ug in zip(att, _groups(ub))], axis=0) + e_b * q_state
    _store_heads(o_ref, 2 * BRANCH_WIDTH, _head_rms(o, norms_ref[2:3, :], _stack(col(P_GDN_Z))))
    upd = _dot_tn((kn * jnp.exp(b_end - b_col)).astype(BF16), _block_diag(ub))
    for h in range(N_HEADS):
        decay_end = jnp.exp(b_all[L - 1:L, N_HEADS + h:N_HEADS + h + 1])
        sgdn_ref[0, h] = decay_end * st[h] + upd[:, h * HEAD_DV:(h + 1) * HEAD_DV]

    for gen in (gla, hgrn):
        for _ in gen:
            pass

    cos = jnp.concatenate([cos_ref[...]] * N_HEADS, axis=0)
    sin = jnp.concatenate([sin_ref[...]] * N_HEADS, axis=0)
    rq, rk = _stack(col(P_RET_Q)), _stack(col(P_RET_K))
    vb = _stack(col(P_RET_V)).astype(BF16)
    row = lax.broadcasted_iota(jnp.int32, (hl, 1), 0)
    t_col = jnp.bitwise_and(row, L - 1).astype(F32)
    lg = jnp.zeros((hl, 1), F32)
    for h in range(N_HEADS):
        lg = jnp.where(jnp.right_shift(row, N_LEVELS) == h, math.log(1.0 - 2.0 ** (-5.0 - h)), lg)
    gl = GROUP * L
    t_minus_s = (lax.broadcasted_iota(jnp.int32, (gl, 1), 0)
                 - lax.broadcasted_iota(jnp.int32, (1, gl), 1)).astype(F32)
    dec = [jnp.exp(jnp.where(incl, t_minus_s * lg_g, -1e30)) for lg_g in _groups(lg)]
    qr = (rq * cos + pltpu.roll(rq, HEAD_DV // 2, axis=1) * sin).astype(BF16)
    kr = (rk * cos + pltpu.roll(rk, HEAD_DV // 2, axis=1) * sin) * HEAD_DV ** -0.5
    kb = kr.astype(BF16)
    st = [sret_ref[0, h] for h in range(N_HEADS)]
    q_state = jnp.concatenate(
        [_dot(qr[h * L:(h + 1) * L], st[h].astype(BF16)) for h in range(N_HEADS)], axis=0)
    att = [(qk * dec_g).astype(BF16) for qk, dec_g in zip(_pair_product(qr, kb), dec)]
    o = (jnp.concatenate([_dot(a, vg) for a, vg in zip(att, _groups(vb))], axis=0)
         + jnp.exp((t_col + 1.0) * lg) * q_state)
    mu = jnp.mean(o, axis=-1, keepdims=True)
    oc = o - mu
    y = oc * lax.rsqrt(jnp.mean(oc * oc, axis=-1, keepdims=True) + EPS)
    y = (y * norms_ref[3:4, :] + norms_ref[4:5, :]) * _silu(_stack(col(P_RET_G)))
    _store_heads(o_ref, 3 * BRANCH_WIDTH, y)
    k_end = (kr * jnp.exp((L - 1.0 - t_col) * lg)).astype(BF16)
    upd = _dot_tn(k_end, _block_diag(vb))
    for h in range(N_HEADS):
        gamma_l = math.exp(L * math.log(1.0 - 2.0 ** (-5.0 - h)))
        sret_ref[0, h] = gamma_l * st[h] + upd[:, h * HEAD_DV:(h + 1) * HEAD_DV]


def _mixers(layer, proj, states_in, cgdn_in, cos_t, sin_t, mall, lvl, wgk, bgk, norms, lb_logits,
            convw, gdnp, n_prompt_streams, prompt_seq):
    t_total = proj.shape[0]
    n_chunks = t_total // L
    chunks_per_stream = prompt_seq // L
    n_prompt_chunks = n_prompt_streams * chunks_per_stream
    n_streams = states_in[0].shape[0]
    past_block = PAST_LEN // L

    def stream(c):
        return jnp.where(c < n_prompt_chunks, c // chunks_per_stream, c - n_prompt_chunks + n_prompt_streams)

    def pos_block(c):
        return jnp.where(c < n_prompt_chunks, c % chunks_per_stream, past_block)

    st_spec = pl.BlockSpec((1, N_HEADS, HEAD_DV, HEAD_DV), lambda c: (stream(c), 0, 0, 0))
    const = lambda shape: pl.BlockSpec(shape, lambda c: tuple(0 for _ in shape))
    st_shape = jax.ShapeDtypeStruct((n_streams, N_HEADS, HEAD_DV, HEAD_DV), F32)
    return pl.pallas_call(
        functools.partial(_mixer_kernel, layer, n_prompt_chunks, chunks_per_stream),
        grid=(n_chunks,),
        in_specs=[pl.BlockSpec((L, N_PACK), lambda c: (c, 0)),
                  st_spec, st_spec, st_spec, st_spec,
                  pl.BlockSpec((1, SUBLANE, GDN_QKV), lambda c: (stream(c), 0, 0)),
                  pl.BlockSpec((L, HEAD_DV), lambda c: (pos_block(c), 0)),
                  pl.BlockSpec((L, HEAD_DV), lambda c: (pos_block(c), 0)),
                  const(mall.shape), const(lvl.shape), const(wgk.shape), const(bgk.shape),
                  const(norms.shape), const(lb_logits.shape), const(convw.shape), const(gdnp.shape)],
        out_specs=[pl.BlockSpec((L, D_MODEL), lambda c: (c, 0)),
                   st_spec, st_spec, st_spec, st_spec,
                   pl.BlockSpec((1, GDN_CONV - 1, GDN_QKV), lambda c: (stream(c), 0, 0))],
        out_shape=[jax.ShapeDtypeStruct((t_total, D_MODEL), BF16),
                   st_shape, st_shape, st_shape, st_shape,
                   jax.ShapeDtypeStruct((n_streams, GDN_CONV - 1, GDN_QKV), F32)],
        scratch_shapes=[pltpu.VMEM((SUBLANE + L, GDN_QKV), F32)],
        compiler_params=_compiler_params(("arbitrary",)),
        name="mixers",
    )(proj, *states_in, cgdn_in, cos_t, sin_t, mall, lvl, wgk, bgk, norms, lb_logits, convw, gdnp)


def _pad_heads(w, dk):
    lead = w.shape[:-1]
    w = w.reshape(lead + (N_HEADS, dk))
    w = jnp.pad(w, [(0, 0)] * len(lead) + [(0, 0), (0, HEAD_DV - dk)])
    return w.reshape(lead + (N_HEADS * HEAD_DV,))


def _pad_cols(w, width):
    return jnp.pad(w, [(0, 0)] * (w.ndim - 1) + [(0, width - w.shape[-1])])


def _mix_pieces():
    pieces = []
    for h in range(N_HEADS):
        pieces.append((P_GLA_Q + h * HEAD_DV, h * GLA_DK, GLA_DK, HEAD_DV))
        pieces.append((P_GLA_K + h * HEAD_DV, 256 + h * GLA_DK, GLA_DK, HEAD_DV))
    pieces += [(P_GLA_V, 512, 512, 512), (P_GLA_G, 1024, 512, 512), (P_GLA_LR, 1536, GLA_LOW_RANK, LANE),
               (P_HG_Q, 1552, 512, 512), (P_HG_F, 2064, 512, 512), (P_HG_I, 2576, 512, 512),
               (P_HG_G, 3088, 512, 512), (P_GDN_QKV, 3600, GDN_QKV, GDN_QKV), (P_GDN_Z, 5136, 512, 512),
               (P_GDN_BA, 5648, 2 * N_HEADS, LANE), (P_RET_Q, 5656, 512, 512), (P_RET_K, 6168, 512, 512),
               (P_RET_V, 6680, 512, 512), (P_RET_G, 7192, 512, 512)]
    return pieces


def _pack_w_in_kernel(merge_tn, w_ref, mix_ref, wm_ref):
    n_in = w_ref.shape[1]

    def window(off, n):
        a0 = off // LANE * LANE
        a1 = min(-(-(off + n) // LANE) * LANE, n_in)
        return w_ref[:, a0:a1][:, off - a0:off - a0 + n].astype(BF16)

    for dst, src, n, padded in _mix_pieces():
        if padded != n:
            mix_ref[:, dst:dst + padded] = jnp.zeros((mix_ref.shape[0], padded), BF16)
        mix_ref[:, dst:dst + n] = window(src, n)
    for j in range(D_MODEL // merge_tn):
        for n in range(N_BRANCH):
            dst = (j * N_BRANCH + n) * merge_tn
            wm_ref[:, dst:dst + merge_tn] = window(N_MIX_IN + n * D_MODEL + j * merge_tn, merge_tn)


def _pack_w_in(w, merge_tn):
    k, n_in = w.shape
    tk = 128
    return pl.pallas_call(
        functools.partial(_pack_w_in_kernel, merge_tn),
        grid=(k // tk,),
        in_specs=[pl.BlockSpec((tk, n_in), lambda i: (i, 0))],
        out_specs=[pl.BlockSpec((tk, N_PACK), lambda i: (i, 0)),
                   pl.BlockSpec((tk, N_BRANCH * D_MODEL), lambda i: (i, 0))],
        out_shape=[jax.ShapeDtypeStruct((k, N_PACK), BF16),
                   jax.ShapeDtypeStruct((k, N_BRANCH * D_MODEL), BF16)],
        compiler_params=_compiler_params(("parallel",)),
        name="pack_w_in",
    )(w)


def _rope_tables(n_pos):
    pos = jnp.arange(n_pos).astype(F32)
    inv = 1.0 / (ROPE_BASE ** (jnp.arange(0, HEAD_DV, 2, dtype=F32) / HEAD_DV))
    ang = pos[:, None] * inv[None, :]
    cos, sin = jnp.cos(ang), jnp.sin(ang)
    return jnp.concatenate([cos, cos], axis=1), jnp.concatenate([-sin, sin], axis=1)


def kernel(x_prompt, x_sample, state_gla, state_hgrn, state_gdn, state_ret, cache_gdn_conv, cache_ffn_conv, norm_mix_g, w_in, gla_w_gk, gla_b_gk, gla_norm_g, hgrn_lb_logits, hgrn_norm_g, gdn_conv_w, gdn_a_log, gdn_dt_bias, gdn_norm_g, ret_norm_g, ret_norm_b, w_branch, w_out, norm_ffn_g, w_ffn_in, ffn_conv_w, ffn_conv_b, w_ffn_out, norm_final_g):
    nb, seq, d = x_prompt.shape
    ns, dseq, _ = x_sample.shape
    assert d == D_MODEL and dseq == L and seq % L == 0
    tp = nb * seq
    x = (x_prompt.reshape(tp, d), x_sample.reshape(ns * dseq, d))

    cos_t, sin_t = _rope_tables(max(seq, PAST_LEN + dseq))
    mall = jnp.asarray(_level_matrix(), BF16)
    lvl = jnp.asarray(_pair_levels())
    merge_tn = 512
    ffn_tn = 512

    def with_prompt(s, transpose=False, pad_k=0):
        if pad_k:
            s = jnp.pad(s, ((0, 0), (0, 0), (0, pad_k), (0, 0)))
        if transpose:
            s = jnp.swapaxes(s, 2, 3)
        return jnp.concatenate([jnp.zeros((nb,) + s.shape[1:], F32), s], axis=0)

    h = _rmsnorm(x[0], x[1], norm_mix_g[0], BF16)
    new_states = []
    for l in range(DEPTH):
        w_mix, w_merge = _pack_w_in(w_in[l], merge_tn)
        proj = _matmul(h, w_mix, 2816, F32)

        states_in = (with_prompt(state_gla[l], transpose=True, pad_k=HEAD_DV - GLA_DK),
                     with_prompt(state_hgrn[l], transpose=True),
                     with_prompt(state_gdn[l]), with_prompt(state_ret[l]))
        cgdn_in = jnp.concatenate([jnp.zeros((nb, GDN_CONV - 1, GDN_QKV), F32), cache_gdn_conv[l]], axis=0)
        cgdn_in = jnp.pad(cgdn_in, ((0, 0), (SUBLANE - (GDN_CONV - 1), 0), (0, 0)))
        wgk = jnp.pad(_pad_heads(gla_w_gk[l], GLA_DK), ((0, LANE - GLA_LOW_RANK), (0, 0))).astype(BF16)
        bgk = _pad_heads(gla_b_gk[l], GLA_DK).reshape(1, 512)
        norms = jnp.stack([gla_norm_g[l], hgrn_norm_g[l], gdn_norm_g[l], ret_norm_g[l], ret_norm_b[l]]
                          + [jnp.zeros((HEAD_DV,), F32)] * 3)
        gdnp = jnp.stack([jnp.pad(gdn_a_log[l], (N_HEADS, LANE - 2 * N_HEADS)),
                          jnp.pad(gdn_dt_bias[l], (N_HEADS, LANE - 2 * N_HEADS))])
        o_all, s_gla, s_hg, s_gdn, s_ret, c_gdn = _mixers(
            l, proj, states_in, cgdn_in, cos_t, sin_t, mall, lvl, wgk, bgk, norms, hgrn_lb_logits,
            gdn_conv_w[l], gdnp, nb, seq)

        y = _merge(h, o_all, w_merge, w_branch[l].astype(BF16), merge_tn)
        x, h2 = _resid_matmul_norm(y, w_out[l].astype(BF16), x, norm_ffn_g[l], tp, 512, False, "out_proj")

        wa = _pad_cols(w_ffn_in[l][:, :D_FF], D_FF_PAD).astype(BF16)
        wu = _pad_cols(w_ffn_in[l][:, D_FF:], D_FF_PAD).astype(BF16)
        cw = _pad_cols(ffn_conv_w[l], D_FF_PAD)
        cb = _pad_cols(ffn_conv_b[l].reshape(1, D_FF), D_FF_PAD)
        cache_in = jnp.pad(cache_ffn_conv[l], ((0, 0), (SUBLANE - (FFN_CONV - 1), 0), (0, D_FF_PAD - D_FF)))
        g, c_ffn_p, c_ffn_s = _ffn_in(h2, wa, wu, cw, cb, cache_in, nb, seq, dseq, ffn_tn)
        w_dn = jnp.pad(w_ffn_out[l], ((0, D_FF_PAD - D_FF), (0, 0))).astype(BF16)
        if l + 1 < DEPTH:
            x, h = _resid_matmul_norm(g, w_dn, x, norm_mix_g[l + 1], tp, 256, False, "ffn_out")
        else:
            y_p, y_s = _resid_matmul_norm(g, w_dn, x, norm_final_g, tp, 256, True, "ffn_out_final")

        s_gla = jnp.swapaxes(s_gla, 2, 3)[:, :, :GLA_DK]
        s_hg = jnp.swapaxes(s_hg, 2, 3)
        c_ffn = jnp.concatenate([c_ffn_p, c_ffn_s], axis=0)[:, :, :D_FF]
        new_states.append((s_gla, s_hg, s_gdn, s_ret, c_gdn, c_ffn))

    y_prompt = y_p.reshape(nb, seq, d)
    y_sample = y_s.reshape(ns, dseq, d)
    p_out = tuple(jnp.stack([st[i][:nb] for st in new_states]) for i in range(6))
    s_out = tuple(jnp.stack([st[i][nb:] for st in new_states]) for i in range(6))
    return (y_prompt, y_sample) + p_out + s_out
```

```python
import functools
import math

import numpy as np
import jax
import jax.numpy as jnp
from jax import lax
from jax.experimental import pallas as pl
from jax.experimental.pallas import tpu as pltpu

F32 = jnp.float32
BF16 = jnp.bfloat16

D_MODEL = 2048
DEPTH = 2
PAST_LEN = 4096
L = 64
N_BRANCH = 4
BRANCH_WIDTH = 512
N_HEADS = 4
HEAD_DV = 128
GLA_DK = 64
GLA_LOW_RANK = 16
GLA_GATE_NORM = 16.0
GDN_CONV = 4
GDN_QKV = 1536
ROPE_BASE = 10000.0
D_FF = 5504
D_FF_PAD = 5632
FFN_CONV = 3
EPS = 1e-6
N_MIX_IN = 7704

LANE = 128
SUBLANE = 8
VMEM_LIMIT = 56 * 1024 * 1024

P_GLA_Q, P_GLA_K, P_GLA_V, P_GLA_G, P_GLA_LR = 0, 512, 1024, 1536, 2048
P_HG_Q, P_HG_F, P_HG_I, P_HG_G = 2176, 2688, 3200, 3712
P_GDN_QKV, P_GDN_Z, P_GDN_BA = 4224, 5760, 6272
P_RET_Q, P_RET_K, P_RET_V, P_RET_G = 6400, 6912, 7424, 7936
N_PACK = 8448
N_LEVELS = 6
GROUP = 2
LOG2_E = math.log2(math.e)


def _dot(a, b):
    return jnp.dot(a, b, preferred_element_type=F32)


def _dot_nt(a, b):
    return lax.dot_general(a, b, (((1,), (1,)), ((), ())), preferred_element_type=F32)


def _dot_tn(a, b):
    return lax.dot_general(a, b, (((0,), (0,)), ((), ())), preferred_element_type=F32)


def _split3(x):
    hi = x.astype(BF16)
    r = x - hi.astype(F32)
    mid = r.astype(BF16)
    lo = (r - mid.astype(F32)).astype(BF16)
    return hi, mid, lo


def _split2(x):
    hi = x.astype(BF16)
    lo = (x - hi.astype(F32)).astype(BF16)
    return hi, lo


def _dot_hi(a, b):
    a_hi, a_lo = _split2(a)
    b_hi, b_lo = _split2(b)
    return _dot(a_hi, b_hi) + (_dot(a_hi, b_lo) + _dot(a_lo, b_hi))


def _sigmoid(x):
    return 1.0 / (1.0 + jnp.exp(-x))


def _silu(x):
    return x * _sigmoid(x)


def _softplus(x):
    return jnp.maximum(x, 0.0) + jnp.log(1.0 + jnp.exp(-jnp.abs(x)))


def _compiler_params(semantics):
    return pltpu.CompilerParams(dimension_semantics=semantics, vmem_limit_bytes=VMEM_LIMIT)


def _tile(n, pref):
    t = min(n, pref)
    while n % t:
        t //= 2
    return t


def _split_specs(tm, d, n_p):
    return (pl.BlockSpec((tm, d), lambda i: (jnp.minimum(i, n_p - 1), 0)),
            pl.BlockSpec((tm, d), lambda i: (jnp.maximum(i - n_p, 0), 0)))


def _rmsnorm_kernel(n_p, xp_ref, xs_ref, g_ref, h_ref):
    x = jnp.where(pl.program_id(0) < n_p, xp_ref[...], xs_ref[...])
    y = x * lax.rsqrt(jnp.mean(x * x, axis=-1, keepdims=True) + EPS)
    h_ref[...] = (y * g_ref[...]).astype(h_ref.dtype)


def _rmsnorm(xp, xs, g, out_dtype):
    d = xp.shape[1]
    m = xp.shape[0] + xs.shape[0]
    tm = _tile(math.gcd(xp.shape[0], xs.shape[0]), 512)
    n_p = xp.shape[0] // tm
    return pl.pallas_call(
        functools.partial(_rmsnorm_kernel, n_p),
        grid=(m // tm,),
        in_specs=[*_split_specs(tm, d, n_p), pl.BlockSpec((1, d), lambda i: (0, 0))],
        out_specs=pl.BlockSpec((tm, d), lambda i: (i, 0)),
        out_shape=jax.ShapeDtypeStruct((m, d), out_dtype),
        compiler_params=_compiler_params(("arbitrary",)),
        name="rmsnorm",
    )(xp, xs, g.reshape(1, d))


def _mm_kernel(a_ref, w_ref, o_ref):
    o_ref[...] = _dot(a_ref[...], w_ref[...]).astype(o_ref.dtype)


def _matmul(a, w, tn, out_dtype):
    m, k = a.shape
    n = w.shape[1]
    tm = _tile(m, 512)
    return pl.pallas_call(
        _mm_kernel,
        grid=(n // tn, m // tm),
        in_specs=[pl.BlockSpec((tm, k), lambda j, i: (i, 0)),
                  pl.BlockSpec((k, tn), lambda j, i: (0, j))],
        out_specs=pl.BlockSpec((tm, tn), lambda j, i: (i, j)),
        out_shape=jax.ShapeDtypeStruct((m, n), out_dtype),
        compiler_params=_compiler_params(("parallel", "parallel")),
        name="in_proj",
    )(a, w)


def _resid_kernel(n_p, split_in, final, a_ref, w_ref, *refs):
    n_x = 2 if split_in else 1
    x_refs, g_ref, out_refs = refs[:n_x], refs[n_x], refs[n_x + 1:]
    is_prompt = pl.program_id(0) < n_p
    x_old = jnp.where(is_prompt, x_refs[0][...], x_refs[1][...]) if split_in else x_refs[0][...]
    x = x_old + _dot(a_ref[...], w_ref[...])
    h = x * lax.rsqrt(jnp.mean(x * x, axis=-1, keepdims=True) + EPS) * g_ref[...]
    if final:
        hp_ref, hs_ref = out_refs

        @pl.when(is_prompt)
        def _():
            hp_ref[...] = h

        @pl.when(jnp.logical_not(is_prompt))
        def _():
            hs_ref[...] = h
    else:
        out_refs[0][...] = x
        out_refs[1][...] = h.astype(out_refs[1].dtype)


def _resid_matmul_norm(a, w, layer, x, g, m_p, tm, final, name):
    m, k = a.shape
    d = w.shape[2]
    tm = _tile(math.gcd(m_p, m - m_p), tm)
    n_p = m_p // tm
    row = lambda i: (i, 0)
    split_in = isinstance(x, tuple)
    x_specs = list(_split_specs(tm, d, n_p)) if split_in else [pl.BlockSpec((tm, d), row)]
    x_args = list(x) if split_in else [x]
    if final:
        out_specs = list(_split_specs(tm, d, n_p))
        out_shape = [jax.ShapeDtypeStruct((m_p, d), F32), jax.ShapeDtypeStruct((m - m_p, d), F32)]
    else:
        out_specs = [pl.BlockSpec((tm, d), row), pl.BlockSpec((tm, d), row)]
        out_shape = [jax.ShapeDtypeStruct((m, d), F32), jax.ShapeDtypeStruct((m, d), BF16)]
    return pl.pallas_call(
        functools.partial(_resid_kernel, n_p, split_in, final),
        grid=(m // tm,),
        in_specs=[pl.BlockSpec((tm, k), row),
                  pl.BlockSpec((None, k, d), lambda i: (layer, 0, 0), pipeline_mode=pl.Buffered(1)),
                  *x_specs,
                  pl.BlockSpec((1, d), lambda i: (0, 0))],
        out_specs=out_specs,
        out_shape=out_shape,
        compiler_params=_compiler_params(("arbitrary",)),
        name=name,
    )(a, w, *x_args, g.reshape(1, d))


def _merge_kernel(h_ref, o_ref, wm_ref, wb_ref, y_ref):
    tn = y_ref.shape[1]
    gates = _dot(h_ref[...], wm_ref[...])
    acc = None
    for n in range(N_BRANCH):
        br = _dot(o_ref[:, n * BRANCH_WIDTH:(n + 1) * BRANCH_WIDTH], wb_ref[n])
        term = _sigmoid(gates[:, n * tn:(n + 1) * tn]) * br
        acc = term if acc is None else acc + term
    y_ref[...] = acc.astype(y_ref.dtype)


def _merge(h, o_all, wm_packed, wb, layer, tn):
    m, d = h.shape
    tm = _tile(m, 1024)
    return pl.pallas_call(
        _merge_kernel,
        grid=(d // tn, m // tm),
        in_specs=[pl.BlockSpec((tm, d), lambda j, i: (i, 0)),
                  pl.BlockSpec((tm, d), lambda j, i: (i, 0)),
                  pl.BlockSpec((d, N_BRANCH * tn), lambda j, i: (0, j)),
                  pl.BlockSpec((None, N_BRANCH, BRANCH_WIDTH, tn), lambda j, i: (layer, 0, 0, j))],
        out_specs=pl.BlockSpec((tm, tn), lambda j, i: (i, j)),
        out_shape=jax.ShapeDtypeStruct((m, d), BF16),
        compiler_params=_compiler_params(("parallel", "parallel")),
        name="merge",
    )(h, o_all, wm_packed, wb)


def _ffn_gate(a, prev1, prev2, u, cw_ref, cb_ref):
    conv = cw_ref[2:3, :] * a + cw_ref[1:2, :] * prev1 + cw_ref[0:1, :] * prev2 + cb_ref[...]
    return (_silu(conv) * u)


def _ffn_in_kernel(n_p, tiles_per_stream, seq_s, h_ref, wa_ref, wu_ref, cw_ref, cb_ref, cin_ref,
                   g_ref, cache_p_ref, cache_s_ref, buf_ref):
    tm = h_ref.shape[0]
    i = pl.program_id(1)

    def up_project():
        h = h_ref[...]
        return _dot(h, wa_ref[...]), _dot(h, wu_ref[...])

    @pl.when(jnp.logical_and(i < n_p, i % tiles_per_stream == 0))
    def _():
        buf_ref[0:SUBLANE, :] = jnp.zeros((SUBLANE, buf_ref.shape[1]), F32)

    @pl.when(i < n_p)
    def _():
        a, u = up_project()
        buf_ref[SUBLANE:SUBLANE + tm, :] = a
        prev1 = buf_ref[SUBLANE - 1:SUBLANE - 1 + tm, :]
        prev2 = buf_ref[SUBLANE - 2:SUBLANE - 2 + tm, :]
        g_ref[...] = _ffn_gate(a, prev1, prev2, u, cw_ref, cb_ref).astype(g_ref.dtype)
        last2 = a[tm - 2:tm, :]
        cache_p_ref[0] = last2
        buf_ref[SUBLANE - 2:SUBLANE, :] = last2

    @pl.when(i >= n_p)
    def _():
        a, u = up_project()
        for s in range(tm // seq_s):
            rows = slice(s * seq_s, (s + 1) * seq_s)
            a_s = a[rows]
            buf_ref[0:SUBLANE, :] = cin_ref[s]
            buf_ref[SUBLANE:SUBLANE + seq_s, :] = a_s
            prev1 = buf_ref[SUBLANE - 1:SUBLANE - 1 + seq_s, :]
            prev2 = buf_ref[SUBLANE - 2:SUBLANE - 2 + seq_s, :]
            g_ref[rows, :] = _ffn_gate(a_s, prev1, prev2, u[rows], cw_ref, cb_ref).astype(g_ref.dtype)
            cache_s_ref[s] = a_s[seq_s - 2:seq_s, :]


def _ffn_in(h, wa, wu, cw, cb, cache_in, layer, n_prompt_streams, prompt_seq, seq_s, tn):
    m, d = h.shape
    m_p = n_prompt_streams * prompt_seq
    n_s_streams = (m - m_p) // seq_s
    tm = _tile(math.gcd(prompt_seq, m - m_p), 1024)
    tiles_per_stream = prompt_seq // tm
    n_p = m_p // tm
    spt = tm // seq_s
    dff = wa.shape[2]
    s_blk = lambda j, i: (jnp.maximum(i - n_p, 0), 0, j)
    w_blk = lambda j, i: (layer, 0, j)
    return pl.pallas_call(
        functools.partial(_ffn_in_kernel, n_p, tiles_per_stream, seq_s),
        grid=(dff // tn, m // tm),
        in_specs=[pl.BlockSpec((tm, d), lambda j, i: (i, 0)),
                  pl.BlockSpec((None, d, tn), w_blk),
                  pl.BlockSpec((None, d, tn), w_blk),
                  pl.BlockSpec((None, FFN_CONV, tn), w_blk),
                  pl.BlockSpec((None, 1, tn), w_blk),
                  pl.BlockSpec((None, spt, SUBLANE, tn), lambda j, i: (layer,) + s_blk(j, i))],
        out_specs=[pl.BlockSpec((tm, tn), lambda j, i: (i, j)),
                   pl.BlockSpec((1, FFN_CONV - 1, tn),
                                lambda j, i: (jnp.minimum(i, n_p - 1) // tiles_per_stream, 0, j)),
                   pl.BlockSpec((spt, FFN_CONV - 1, tn), s_blk)],
        out_shape=[jax.ShapeDtypeStruct((m, dff), BF16),
                   jax.ShapeDtypeStruct((n_prompt_streams, FFN_CONV - 1, dff), F32),
                   jax.ShapeDtypeStruct((n_s_streams, FFN_CONV - 1, dff), F32)],
        scratch_shapes=[pltpu.VMEM((SUBLANE + tm, tn), F32)],
        compiler_params=_compiler_params(("arbitrary", "arbitrary")),
        name="ffn_in",
    )(h, wa, wu, cw, cb, cache_in)


def _level_matrix():
    t = np.arange(L)[:, None]
    u = np.arange(L)[None, :]
    blocks = [(u <= t)]
    for l in range(N_LEVELS):
        n = 1 << l
        r = (t & ~(2 * n - 1)) + n - 1
        upper = (t & n) != 0
        blocks.append(np.where(upper, (u > r) & (u <= t), (u > t) & (u <= r)))
    blocks.append(u > t)
    return np.concatenate(blocks, axis=0).astype(np.float32)


def _pair_levels():
    r = np.arange(GROUP * L)
    x = r[:, None] ^ r[None, :]
    code = np.full(x.shape, -1, np.int32)
    code[x == 0] = 0
    for l in range(N_LEVELS):
        code[(x >> l) == 1] = l + 1
    code[r[None, :] > r[:, None]] = -1
    return code


def _stack(x):
    return jnp.concatenate([x[:, h * HEAD_DV:(h + 1) * HEAD_DV] for h in range(N_HEADS)], axis=0)


def _stack_cols(x, first):
    return jnp.concatenate([x[:, first + h:first + h + 1] for h in range(N_HEADS)], axis=0)


def _groups(x):
    gl = GROUP * L
    return [x[g * gl:(g + 1) * gl] for g in range(N_HEADS // GROUP)]


def _pair_product(a, b):
    return [_dot_nt(ag, bg) for ag, bg in zip(_groups(a), _groups(b))]


def _block_diag(x):
    z = jnp.zeros((L, HEAD_DV), x.dtype)
    rows = [jnp.concatenate([x[h * L:(h + 1) * L] if j == h else z for j in range(N_HEADS)], axis=1)
            for h in range(N_HEADS)]
    return jnp.concatenate(rows, axis=0)


def _store_heads(o_ref, off, y):
    for h in range(N_HEADS):
        o_ref[:, off + h * HEAD_DV:off + (h + 1) * HEAD_DV] = y[h * L:(h + 1) * L].astype(o_ref.dtype)


def _head_rms(o, gain, gate):
    y = o * lax.rsqrt(jnp.mean(o * o, axis=-1, keepdims=True) + EPS)
    return y * gain * _silu(gate)


def _gla_type(q, k, v, g, gate, gain, st_ref, mall, lvl, o_ref, o_off):
    g_hi, g_lo = _split2(g * LOG2_E)
    d2 = _dot(mall, jnp.concatenate([g_hi, g_lo], axis=1))
    w = g.shape[1]
    e_all = jnp.exp2(d2[:, :w] + d2[:, w:])

    def e_rows(r):
        return _stack(e_all[r * L:(r + 1) * L])

    qs, ks, vs = _stack(q), _stack(k), _stack(v)
    att = [jnp.where(lvl == 0, a, 0.0) for a in _pair_product(qs.astype(BF16), ks.astype(BF16))]
    for l in range(N_LEVELS):
        yield
        e_l = e_rows(l + 1)
        a_l = _pair_product((qs * e_l).astype(BF16), (ks * e_l).astype(BF16))
        att = [jnp.where(lvl == l + 1, a, prev) for a, prev in zip(a_l, att)]
    yield
    st = [st_ref[0, h] for h in range(N_HEADS)]
    vb = vs.astype(BF16)
    q_decayed = (qs * e_rows(0)).astype(BF16)
    q_state = jnp.concatenate(
        [_dot_nt(q_decayed[h * L:(h + 1) * L], st[h].astype(BF16)) for h in range(N_HEADS)], axis=0)
    o = jnp.concatenate([_dot(a.astype(BF16), vg) for a, vg in zip(att, _groups(vb))], axis=0) + q_state
    _store_heads(o_ref, o_off, _head_rms(o, gain, _stack(gate)))
    upd = _dot_tn(_block_diag(vb), (ks * e_rows(N_LEVELS + 1)).astype(BF16))
    for h in range(N_HEADS):
        decay_end = e_all[L - 1:L, h * HEAD_DV:(h + 1) * HEAD_DV]
        st_ref[0, h] = st[h] * decay_end + upd[h * HEAD_DV:(h + 1) * HEAD_DV]


def _mixer_kernel(layer, n_prompt_chunks, chunks_per_stream,
                  proj_ref, sgla_in, shg_in, sgdn_in, sret_in, cgdn_in,
                  cos_ref, sin_ref, mall_ref, lvl_ref, wgk_ref, bgk_ref, norms_ref, lb_ref,
                  convw_ref, gdnp_ref,
                  o_ref, p_gla, p_hg, p_gdn, p_ret, p_cache, s_gla, s_hg, s_gdn, s_ret, s_cache,
                  xe_ref, sgla_ref, shg_ref, sgdn_ref, sret_ref):
    c = pl.program_id(0)
    is_prompt = c < n_prompt_chunks
    is_sample = jnp.logical_not(is_prompt)
    base = SUBLANE - (GDN_CONV - 1)

    @pl.when(jnp.logical_and(is_prompt, c % chunks_per_stream == 0))
    def _():
        for ref in (sgla_ref, shg_ref, sgdn_ref, sret_ref):
            ref[...] = jnp.zeros(ref.shape, F32)
        xe_ref[0:SUBLANE, :] = jnp.zeros((SUBLANE, GDN_QKV), F32)

    @pl.when(is_sample)
    def _():
        key_pad = jnp.zeros((HEAD_DV - GLA_DK, HEAD_DV), F32)
        for h in range(N_HEADS):
            sgla_ref[0, h] = jnp.concatenate([sgla_in[0, h], key_pad], axis=0).T
            shg_ref[0, h] = shg_in[0, h].T
        sgdn_ref[...] = sgdn_in[...]
        sret_ref[...] = sret_in[...]
        xe_ref[base:SUBLANE, :] = cgdn_in[0]

    lvl = lvl_ref[...]
    incl = lvl >= 0
    mall = mall_ref[...]
    tri = mall[0:L]
    hl = N_HEADS * L

    def col(off, width=512):
        return proj_ref[:, off:off + width]

    z = _dot(col(P_GLA_LR, LANE).astype(BF16), wgk_ref[...]) + bgk_ref[...]
    g_gla = (jnp.minimum(z, 0.0) - jnp.log(1.0 + jnp.exp(-jnp.abs(z)))) * (1.0 / GLA_GATE_NORM)
    gla = _gla_type(col(P_GLA_Q) * GLA_DK ** -0.5, col(P_GLA_K), col(P_GLA_V), g_gla, col(P_GLA_G),
                    norms_ref[0:1, :], sgla_ref, mall, lvl, o_ref, 0)
    if layer == 0:
        lb = jnp.zeros((1, 512), F32)
    else:
        logits = lb_ref[...]
        ex = jnp.exp(logits - jnp.max(logits, axis=0, keepdims=True))
        soft = ex / jnp.sum(ex, axis=0, keepdims=True)
        lb = jnp.sum(soft[1:layer + 1], axis=0, keepdims=True)
    f = lb + (1.0 - lb) * _sigmoid(col(P_HG_F))
    hgrn = _gla_type(_silu(col(P_HG_Q)), 1.0 - f, col(P_HG_I), jnp.log(f), col(P_HG_G),
                     norms_ref[1:2, :], shg_ref, mall, lvl, o_ref, BRANCH_WIDTH)

    xe_ref[SUBLANE:SUBLANE + L, :] = col(P_GDN_QKV, GDN_QKV)
    conv = convw_ref[0:1, :] * xe_ref[base:base + L, :]
    for j in range(1, GDN_CONV):
        conv = conv + convw_ref[j:j + 1, :] * xe_ref[base + j:base + j + L, :]
    xe_ref[base:SUBLANE, :] = xe_ref[SUBLANE + L - (GDN_CONV - 1):SUBLANE + L, :]
    act = _silu(conv)
    ba = col(P_GDN_BA, LANE)
    beta = _stack_cols(_sigmoid(ba), 0)
    log_a = -jnp.exp(gdnp_ref[0:1, :]) * _softplus(ba + gdnp_ref[1:2, :])
    la_hi, la_mid, la_lo = _split3(log_a)
    b_all = _dot(tri, la_hi) + (_dot(tri, la_mid) + _dot(tri, la_lo))
    b_col = _stack_cols(b_all, N_HEADS)
    b_rows = b_all.T
    b_row = [jnp.concatenate([b_rows[N_HEADS + GROUP * g + i:N_HEADS + GROUP * g + i + 1, :]
                              for i in range(GROUP)], axis=1) for g in range(N_HEADS // GROUP)]
    b_end = jnp.concatenate(
        [jnp.broadcast_to(b_all[L - 1:L, N_HEADS + h:N_HEADS + h + 1], (L, 1)) for h in range(N_HEADS)], axis=0)
    dq, dk, dv = _stack(act[:, 0:512]), _stack(act[:, 512:1024]), _stack(act[:, 1024:1536])
    qn = dq * lax.rsqrt(jnp.sum(dq * dq, axis=-1, keepdims=True) + EPS) * HEAD_DV ** -0.5
    kn = dk * lax.rsqrt(jnp.sum(dk * dk, axis=-1, keepdims=True) + EPS)
    qb, kb = qn.astype(BF16), kn.astype(BF16)
    dec = [jnp.exp(jnp.where(incl, bc - br, -1e30)) for bc, br in zip(_groups(b_col), b_row)]
    e_b = jnp.exp(b_col)
    st = [sgdn_ref[0, h] for h in range(N_HEADS)]
    kq_state = [_dot(jnp.concatenate([kb[h * L:(h + 1) * L], qb[h * L:(h + 1) * L]], axis=0),
                     st[h].astype(BF16)) for h in range(N_HEADS)]
    k_state = jnp.concatenate([x[0:L] for x in kq_state], axis=0)
    q_state = jnp.concatenate([x[L:2 * L] for x in kq_state], axis=0)
    rhs = beta * (dv - e_b * k_state)
    p_groups = [-(jnp.where(lvl > 0, kk * dec_g, 0.0) * beta_g)
                for kk, dec_g, beta_g in zip(_pair_product(kb, kb), dec, _groups(beta))]
    u_groups = _groups(rhs)
    next(gla)
    next(hgrn)
    for j in range(N_LEVELS):
        u_groups = [u + _dot_hi(p, u) for p, u in zip(p_groups, u_groups)]
        if j + 1 < N_LEVELS:
            p_groups = [_dot_hi(p, p) for p in p_groups]
        next(gla)
        next(hgrn)
    ub = jnp.concatenate(u_groups, axis=0).astype(BF16)
    att = [(qk * dec_g).astype(BF16) for qk, dec_g in zip(_pair_product(qb, kb), dec)]
    o = jnp.concatenate([_dot(a, ug) for a, ug in zip(att, _groups(ub))], axis=0) + e_b * q_state
    _store_heads(o_ref, 2 * BRANCH_WIDTH, _head_rms(o, norms_ref[2:3, :], _stack(col(P_GDN_Z))))
    upd = _dot_tn((kn * jnp.exp(b_end - b_col)).astype(BF16), _block_diag(ub))
    for h in range(N_HEADS):
        decay_end = jnp.exp(b_all[L - 1:L, N_HEADS + h:N_HEADS + h + 1])
        sgdn_ref[0, h] = decay_end * st[h] + upd[:, h * HEAD_DV:(h + 1) * HEAD_DV]

    for gen in (gla, hgrn):
        for _ in gen:
            pass

    cos = jnp.concatenate([cos_ref[...]] * N_HEADS, axis=0)
    sin = jnp.concatenate([sin_ref[...]] * N_HEADS, axis=0)
    rq, rk = _stack(col(P_RET_Q)), _stack(col(P_RET_K))
    vb = _stack(col(P_RET_V)).astype(BF16)
    row = lax.broadcasted_iota(jnp.int32, (hl, 1), 0)
    t_col = jnp.bitwise_and(row, L - 1).astype(F32)
    lg = jnp.zeros((hl, 1), F32)
    for h in range(N_HEADS):
        lg = jnp.where(jnp.right_shift(row, N_LEVELS) == h, math.log(1.0 - 2.0 ** (-5.0 - h)), lg)
    gl = GROUP * L
    t_minus_s = (lax.broadcasted_iota(jnp.int32, (gl, 1), 0)
                 - lax.broadcasted_iota(jnp.int32, (1, gl), 1)).astype(F32)
    dec = [jnp.exp(jnp.where(incl, t_minus_s * lg_g, -1e30)) for lg_g in _groups(lg)]
    qr = (rq * cos + pltpu.roll(rq, HEAD_DV // 2, axis=1) * sin).astype(BF16)
    kr = (rk * cos + pltpu.roll(rk, HEAD_DV // 2, axis=1) * sin) * HEAD_DV ** -0.5
    kb = kr.astype(BF16)
    st = [sret_ref[0, h] for h in range(N_HEADS)]
    q_state = jnp.concatenate(
        [_dot(qr[h * L:(h + 1) * L], st[h].astype(BF16)) for h in range(N_HEADS)], axis=0)
    att = [(qk * dec_g).astype(BF16) for qk, dec_g in zip(_pair_product(qr, kb), dec)]
    o = (jnp.concatenate([_dot(a, vg) for a, vg in zip(att, _groups(vb))], axis=0)
         + jnp.exp((t_col + 1.0) * lg) * q_state)
    mu = jnp.mean(o, axis=-1, keepdims=True)
    oc = o - mu
    y = oc * lax.rsqrt(jnp.mean(oc * oc, axis=-1, keepdims=True) + EPS)
    y = (y * norms_ref[3:4, :] + norms_ref[4:5, :]) * _silu(_stack(col(P_RET_G)))
    _store_heads(o_ref, 3 * BRANCH_WIDTH, y)
    k_end = (kr * jnp.exp((L - 1.0 - t_col) * lg)).astype(BF16)
    upd = _dot_tn(k_end, _block_diag(vb))
    for h in range(N_HEADS):
        gamma_l = math.exp(L * math.log(1.0 - 2.0 ** (-5.0 - h)))
        sret_ref[0, h] = gamma_l * st[h] + upd[:, h * HEAD_DV:(h + 1) * HEAD_DV]

    def write_states(gla_out, hg_out, gdn_out, ret_out, cache_out):
        for h in range(N_HEADS):
            gla_out[0, h] = sgla_ref[0, h].T[0:GLA_DK]
            hg_out[0, h] = shg_ref[0, h].T
        gdn_out[...] = sgdn_ref[...]
        ret_out[...] = sret_ref[...]
        cache_out[0] = xe_ref[base:SUBLANE, :]

    @pl.when(jnp.logical_and(is_prompt, c % chunks_per_stream == chunks_per_stream - 1))
    def _():
        write_states(p_gla, p_hg, p_gdn, p_ret, p_cache)

    @pl.when(is_sample)
    def _():
        write_states(s_gla, s_hg, s_gdn, s_ret, s_cache)


def _mixers(layer, proj, states, cache, cos_t, sin_t, mall, lvl, wgk, bgk, norms, lb_logits,
            convw, gdnp, n_prompt_streams, prompt_seq):
    t_total = proj.shape[0]
    n_chunks = t_total // L
    chunks_per_stream = prompt_seq // L
    n_prompt_chunks = n_prompt_streams * chunks_per_stream
    n_sample = states[0].shape[1]
    past_block = PAST_LEN // L

    def p_stream(c):
        return jnp.minimum(c // chunks_per_stream, n_prompt_streams - 1)

    def s_stream(c):
        return jnp.maximum(c - n_prompt_chunks, 0)

    def pos_block(c):
        return jnp.where(c < n_prompt_chunks, c % chunks_per_stream, past_block)

    def layer_block(a):
        return pl.BlockSpec((None,) + a.shape[1:], lambda c: (layer,) + (0,) * (a.ndim - 1))

    def state_in(a):
        return pl.BlockSpec((None, 1) + a.shape[2:], lambda c: (layer, s_stream(c)) + (0,) * (a.ndim - 2))

    def state_out(a, stream):
        return pl.BlockSpec((1,) + a.shape[2:], lambda c: (stream(c),) + (0,) * (a.ndim - 2))

    const = lambda a: pl.BlockSpec(a.shape, lambda c: (0,) * a.ndim)
    per_stream = list(states) + [cache]
    out_shapes = lambda n: [jax.ShapeDtypeStruct((n,) + a.shape[2:], F32) for a in per_stream]
    carry = pltpu.VMEM((1, N_HEADS, HEAD_DV, HEAD_DV), F32)
    return pl.pallas_call(
        functools.partial(_mixer_kernel, layer, n_prompt_chunks, chunks_per_stream),
        grid=(n_chunks,),
        in_specs=[pl.BlockSpec((L, N_PACK), lambda c: (c, 0)),
                  *[state_in(a) for a in per_stream],
                  pl.BlockSpec((L, HEAD_DV), lambda c: (pos_block(c), 0)),
                  pl.BlockSpec((L, HEAD_DV), lambda c: (pos_block(c), 0)),
                  const(mall), const(lvl), layer_block(wgk), layer_block(bgk),
                  layer_block(norms), const(lb_logits), layer_block(convw), layer_block(gdnp)],
        out_specs=[pl.BlockSpec((L, D_MODEL), lambda c: (c, 0)),
                   *[state_out(a, p_stream) for a in per_stream],
                   *[state_out(a, s_stream) for a in per_stream]],
        out_shape=[jax.ShapeDtypeStruct((t_total, D_MODEL), BF16),
                   *out_shapes(n_prompt_streams), *out_shapes(n_sample)],
        scratch_shapes=[pltpu.VMEM((SUBLANE + L, GDN_QKV), F32), carry, carry, carry, carry],
        compiler_params=_compiler_params(("arbitrary",)),
        name="mixers",
    )(proj, *per_stream, cos_t, sin_t, mall, lvl, wgk, bgk, norms, lb_logits, convw, gdnp)


def _pad_heads(w, dk):
    lead = w.shape[:-1]
    w = w.reshape(lead + (N_HEADS, dk))
    w = jnp.pad(w, [(0, 0)] * len(lead) + [(0, 0), (0, HEAD_DV - dk)])
    return w.reshape(lead + (N_HEADS * HEAD_DV,))


def _pad_cols(w, width):
    return jnp.pad(w, [(0, 0)] * (w.ndim - 1) + [(0, width - w.shape[-1])])


def _mix_pieces():
    pieces = []
    for h in range(N_HEADS):
        pieces.append((P_GLA_Q + h * HEAD_DV, h * GLA_DK, GLA_DK, HEAD_DV))
        pieces.append((P_GLA_K + h * HEAD_DV, 256 + h * GLA_DK, GLA_DK, HEAD_DV))
    pieces += [(P_GLA_V, 512, 512, 512), (P_GLA_G, 1024, 512, 512), (P_GLA_LR, 1536, GLA_LOW_RANK, LANE),
               (P_HG_Q, 1552, 512, 512), (P_HG_F, 2064, 512, 512), (P_HG_I, 2576, 512, 512),
               (P_HG_G, 3088, 512, 512), (P_GDN_QKV, 3600, GDN_QKV, GDN_QKV), (P_GDN_Z, 5136, 512, 512),
               (P_GDN_BA, 5648, 2 * N_HEADS, LANE), (P_RET_Q, 5656, 512, 512), (P_RET_K, 6168, 512, 512),
               (P_RET_V, 6680, 512, 512), (P_RET_G, 7192, 512, 512)]
    return pieces


def _pack_w_in_kernel(merge_tn, w_ref, mix_ref, wm_ref):
    n_in = w_ref.shape[1]

    def window(off, n):
        a0 = off // LANE * LANE
        a1 = min(-(-(off + n) // LANE) * LANE, n_in)
        return w_ref[:, a0:a1][:, off - a0:off - a0 + n].astype(BF16)

    for dst, src, n, padded in _mix_pieces():
        if padded != n:
            mix_ref[:, dst:dst + padded] = jnp.zeros((mix_ref.shape[0], padded), BF16)
        mix_ref[:, dst:dst + n] = window(src, n)
    for j in range(D_MODEL // merge_tn):
        for n in range(N_BRANCH):
            dst = (j * N_BRANCH + n) * merge_tn
            wm_ref[:, dst:dst + merge_tn] = window(N_MIX_IN + n * D_MODEL + j * merge_tn, merge_tn)


def _pack_w_in(w, layer, merge_tn):
    _, k, n_in = w.shape
    tk = 128
    return pl.pallas_call(
        functools.partial(_pack_w_in_kernel, merge_tn),
        grid=(k // tk,),
        in_specs=[pl.BlockSpec((None, tk, n_in), lambda i: (layer, i, 0))],
        out_specs=[pl.BlockSpec((tk, N_PACK), lambda i: (i, 0)),
                   pl.BlockSpec((tk, N_BRANCH * D_MODEL), lambda i: (i, 0))],
        out_shape=[jax.ShapeDtypeStruct((k, N_PACK), BF16),
                   jax.ShapeDtypeStruct((k, N_BRANCH * D_MODEL), BF16)],
        compiler_params=_compiler_params(("parallel",)),
        name="pack_w_in",
    )(w)


def _cast_rows_kernel(n_valid, w_ref, o_ref):
    i = pl.program_id(1)

    @pl.when(i < n_valid)
    def _():
        o_ref[...] = w_ref[...].astype(o_ref.dtype)

    @pl.when(i >= n_valid)
    def _():
        o_ref[...] = jnp.zeros(o_ref.shape, o_ref.dtype)


def _cast_rows(w, rows_out, tk):
    depth, r, c = w.shape
    n_valid = r // tk
    assert r % tk == 0 and rows_out % tk == 0
    return pl.pallas_call(
        functools.partial(_cast_rows_kernel, n_valid),
        grid=(depth, rows_out // tk),
        in_specs=[pl.BlockSpec((None, tk, c), lambda l, i: (l, jnp.minimum(i, n_valid - 1), 0))],
        out_specs=pl.BlockSpec((None, tk, c), lambda l, i: (l, i, 0)),
        out_shape=jax.ShapeDtypeStruct((depth, rows_out, c), BF16),
        compiler_params=_compiler_params(("parallel", "arbitrary")),
        name="cast_rows",
    )(w)


def _split_ffn_in_kernel(w_ref, a_ref, u_ref):
    pad = jnp.zeros((a_ref.shape[0], D_FF_PAD - D_FF), BF16)
    a_ref[:, 0:D_FF] = w_ref[:, 0:D_FF].astype(BF16)
    a_ref[:, D_FF:D_FF_PAD] = pad
    u_ref[:, 0:D_FF] = w_ref[:, D_FF:2 * D_FF].astype(BF16)
    u_ref[:, D_FF:D_FF_PAD] = pad


def _split_ffn_in(w, tk):
    depth, d, _ = w.shape
    blk = lambda c: pl.BlockSpec((None, tk, c), lambda l, i: (l, i, 0))
    out = jax.ShapeDtypeStruct((depth, d, D_FF_PAD), BF16)
    return pl.pallas_call(
        _split_ffn_in_kernel,
        grid=(depth, d // tk),
        in_specs=[blk(2 * D_FF)],
        out_specs=[blk(D_FF_PAD), blk(D_FF_PAD)],
        out_shape=[out, out],
        compiler_params=_compiler_params(("parallel", "parallel")),
        name="split_ffn_in",
    )(w)


def _rope_tables(n_pos):
    pos = np.arange(n_pos, dtype=np.float64)
    inv = 1.0 / (ROPE_BASE ** (np.arange(0, HEAD_DV, 2, dtype=np.float64) / HEAD_DV))
    ang = pos[:, None] * inv[None, :]
    cos, sin = np.cos(ang), np.sin(ang)
    return (jnp.asarray(np.concatenate([cos, cos], axis=1), F32),
            jnp.asarray(np.concatenate([-sin, sin], axis=1), F32))


def kernel(x_prompt, x_sample, state_gla, state_hgrn, state_gdn, state_ret, cache_gdn_conv, cache_ffn_conv, norm_mix_g, w_in, gla_w_gk, gla_b_gk, gla_norm_g, hgrn_lb_logits, hgrn_norm_g, gdn_conv_w, gdn_a_log, gdn_dt_bias, gdn_norm_g, ret_norm_g, ret_norm_b, w_branch, w_out, norm_ffn_g, w_ffn_in, ffn_conv_w, ffn_conv_b, w_ffn_out, norm_final_g):
    nb, seq, d = x_prompt.shape
    ns, dseq, _ = x_sample.shape
    assert d == D_MODEL and dseq == L and seq % L == 0
    tp = nb * seq
    x = (x_prompt.reshape(tp, d), x_sample.reshape(ns * dseq, d))

    cos_t, sin_t = _rope_tables(max(seq, PAST_LEN + dseq))
    mall = jnp.asarray(_level_matrix(), BF16)
    lvl = jnp.asarray(_pair_levels())
    merge_tn = 512
    ffn_tn = 512

    wgk = jnp.pad(_pad_heads(gla_w_gk, GLA_DK), ((0, 0), (0, LANE - GLA_LOW_RANK), (0, 0))).astype(BF16)
    bgk = _pad_heads(gla_b_gk, GLA_DK)[:, None, :]
    norms = jnp.stack([gla_norm_g, hgrn_norm_g, gdn_norm_g, ret_norm_g, ret_norm_b]
                      + [jnp.zeros_like(gla_norm_g)] * 3, axis=1)
    head_lanes = ((0, 0), (N_HEADS, LANE - 2 * N_HEADS))
    gdnp = jnp.stack([jnp.pad(gdn_a_log, head_lanes), jnp.pad(gdn_dt_bias, head_lanes)], axis=1)
    w_branch_b = _cast_rows(w_branch.reshape(DEPTH, N_BRANCH * BRANCH_WIDTH, d), N_BRANCH * BRANCH_WIDTH,
                            256).reshape(w_branch.shape)
    w_out_b = _cast_rows(w_out, d, 256)
    wa, wu = _split_ffn_in(w_ffn_in, 128)
    w_dn = _cast_rows(w_ffn_out, D_FF_PAD, 128)
    cw = _pad_cols(ffn_conv_w, D_FF_PAD)
    cb = _pad_cols(ffn_conv_b, D_FF_PAD)[:, None, :]
    cache_in = jnp.pad(cache_ffn_conv,
                       ((0, 0), (0, 0), (SUBLANE - (FFN_CONV - 1), 0), (0, D_FF_PAD - D_FF)))
    states = (state_gla, state_hgrn, state_gdn, state_ret)

    h = _rmsnorm(x[0], x[1], norm_mix_g[0], BF16)
    new_p, new_s = [], []
    for l in range(DEPTH):
        w_mix, w_merge = _pack_w_in(w_in, l, merge_tn)
        proj = _matmul(h, w_mix, 2816, F32)
        o_all, *st = _mixers(l, proj, states, cache_gdn_conv, cos_t, sin_t, mall, lvl, wgk, bgk, norms,
                             hgrn_lb_logits, gdn_conv_w, gdnp, nb, seq)
        y = _merge(h, o_all, w_merge, w_branch_b, l, merge_tn)
        x, h2 = _resid_matmul_norm(y, w_out_b, l, x, norm_ffn_g[l], tp, 512, False, "out_proj")
        g, c_ffn_p, c_ffn_s = _ffn_in(h2, wa, wu, cw, cb, cache_in, l, nb, seq, dseq, ffn_tn)
        if l + 1 < DEPTH:
            x, h = _resid_matmul_norm(g, w_dn, l, x, norm_mix_g[l + 1], tp, 256, False, "ffn_out")
        else:
            y_p, y_s = _resid_matmul_norm(g, w_dn, l, x, norm_final_g, tp, 256, True, "ffn_out_final")
        new_p.append(st[:5] + [c_ffn_p[:, :, :D_FF]])
        new_s.append(st[5:] + [c_ffn_s[:, :, :D_FF]])

    y_prompt = y_p.reshape(nb, seq, d)
    y_sample = y_s.reshape(ns, dseq, d)
    p_out = tuple(jnp.stack([st[i] for st in new_p]) for i in range(6))
    s_out = tuple(jnp.stack([st[i] for st in new_s]) for i in range(6))
    return (y_prompt, y_sample) + p_out + s_out
```

```python
import functools
import math

import numpy as np
import jax
import jax.numpy as jnp
from jax import lax
from jax.experimental import pallas as pl
from jax.experimental.pallas import tpu as pltpu

F32 = jnp.float32
BF16 = jnp.bfloat16

D_MODEL = 2048
DEPTH = 2
PAST_LEN = 4096
L = 64
N_BRANCH = 4
BRANCH_WIDTH = 512
N_HEADS = 4
HEAD_DV = 128
GLA_DK = 64
GLA_LOW_RANK = 16
GLA_GATE_NORM = 16.0
GDN_CONV = 4
GDN_QKV = 1536
ROPE_BASE = 10000.0
D_FF = 5504
D_FF_PAD = 5632
FFN_CONV = 3
EPS = 1e-6
N_MIX_IN = 7704

LANE = 128
SUBLANE = 8
VMEM_LIMIT = 56 * 1024 * 1024

P_GLA_Q, P_GLA_K, P_GLA_V, P_GLA_G, P_GLA_LR = 0, 512, 1024, 1536, 2048
P_HG_Q, P_HG_F, P_HG_I, P_HG_G = 2176, 2688, 3200, 3712
P_GDN_QKV, P_GDN_Z, P_GDN_BA = 4224, 5760, 6272
P_RET_Q, P_RET_K, P_RET_V, P_RET_G = 6400, 6912, 7424, 7936
N_PACK = 8448
N_LEVELS = 6
GROUP = 2
LOG2_E = math.log2(math.e)


def _dot(a, b):
    return jnp.dot(a, b, preferred_element_type=F32)


def _dot_nt(a, b):
    return lax.dot_general(a, b, (((1,), (1,)), ((), ())), preferred_element_type=F32)


def _dot_tn(a, b):
    return lax.dot_general(a, b, (((0,), (0,)), ((), ())), preferred_element_type=F32)


def _split3(x):
    hi = x.astype(BF16)
    r = x - hi.astype(F32)
    mid = r.astype(BF16)
    lo = (r - mid.astype(F32)).astype(BF16)
    return hi, mid, lo


def _split2(x):
    hi = x.astype(BF16)
    lo = (x - hi.astype(F32)).astype(BF16)
    return hi, lo


def _dot_hi(a, b):
    a_hi, a_lo = _split2(a)
    b_hi, b_lo = _split2(b)
    return _dot(a_hi, b_hi) + (_dot(a_hi, b_lo) + _dot(a_lo, b_hi))


def _sigmoid(x):
    return 1.0 / (1.0 + jnp.exp(-x))


def _silu(x):
    return x * _sigmoid(x)


def _softplus(x):
    return jnp.maximum(x, 0.0) + jnp.log(1.0 + jnp.exp(-jnp.abs(x)))


def _compiler_params(semantics):
    return pltpu.CompilerParams(dimension_semantics=semantics, vmem_limit_bytes=VMEM_LIMIT)


def _tile(n, pref):
    t = min(n, pref)
    while n % t:
        t //= 2
    return t


def _split_specs(tm, d, n_p):
    return (pl.BlockSpec((tm, d), lambda i: (jnp.minimum(i, n_p - 1), 0)),
            pl.BlockSpec((tm, d), lambda i: (jnp.maximum(i - n_p, 0), 0)))


def _rmsnorm_kernel(n_p, xp_ref, xs_ref, g_ref, h_ref):
    x = jnp.where(pl.program_id(0) < n_p, xp_ref[...], xs_ref[...])
    y = x * lax.rsqrt(jnp.mean(x * x, axis=-1, keepdims=True) + EPS)
    h_ref[...] = (y * g_ref[...]).astype(h_ref.dtype)


def _rmsnorm(xp, xs, g, out_dtype):
    d = xp.shape[1]
    m = xp.shape[0] + xs.shape[0]
    tm = _tile(math.gcd(xp.shape[0], xs.shape[0]), 512)
    n_p = xp.shape[0] // tm
    return pl.pallas_call(
        functools.partial(_rmsnorm_kernel, n_p),
        grid=(m // tm,),
        in_specs=[*_split_specs(tm, d, n_p), pl.BlockSpec((1, d), lambda i: (0, 0))],
        out_specs=pl.BlockSpec((tm, d), lambda i: (i, 0)),
        out_shape=jax.ShapeDtypeStruct((m, d), out_dtype),
        compiler_params=_compiler_params(("arbitrary",)),
        name="rmsnorm",
    )(xp, xs, g.reshape(1, d))


def _mm_kernel(a_ref, w_ref, o_ref):
    o_ref[...] = _dot(a_ref[...], w_ref[...]).astype(o_ref.dtype)


def _matmul(a, w, tn, out_dtype):
    m, k = a.shape
    n = w.shape[1]
    tm = _tile(m, 512)
    return pl.pallas_call(
        _mm_kernel,
        grid=(n // tn, m // tm),
        in_specs=[pl.BlockSpec((tm, k), lambda j, i: (i, 0)),
                  pl.BlockSpec((k, tn), lambda j, i: (0, j))],
        out_specs=pl.BlockSpec((tm, tn), lambda j, i: (i, j)),
        out_shape=jax.ShapeDtypeStruct((m, n), out_dtype),
        compiler_params=_compiler_params(("parallel", "parallel")),
        name="in_proj",
    )(a, w)


def _resid_kernel(n_p, split_in, final, a_ref, w_ref, *refs):
    n_x = 2 if split_in else 1
    x_refs, g_ref, out_refs = refs[:n_x], refs[n_x], refs[n_x + 1:]
    is_prompt = pl.program_id(0) < n_p
    x_old = jnp.where(is_prompt, x_refs[0][...], x_refs[1][...]) if split_in else x_refs[0][...]
    x = x_old + _dot(a_ref[...], w_ref[...])
    h = x * lax.rsqrt(jnp.mean(x * x, axis=-1, keepdims=True) + EPS) * g_ref[...]
    if final:
        hp_ref, hs_ref = out_refs

        @pl.when(is_prompt)
        def _():
            hp_ref[...] = h

        @pl.when(jnp.logical_not(is_prompt))
        def _():
            hs_ref[...] = h
    else:
        out_refs[0][...] = x
        out_refs[1][...] = h.astype(out_refs[1].dtype)


def _resid_matmul_norm(a, w, layer, x, g, m_p, tm, final, name):
    m, k = a.shape
    d = w.shape[2]
    tm = _tile(math.gcd(m_p, m - m_p), tm)
    n_p = m_p // tm
    row = lambda i: (i, 0)
    split_in = isinstance(x, tuple)
    x_specs = list(_split_specs(tm, d, n_p)) if split_in else [pl.BlockSpec((tm, d), row)]
    x_args = list(x) if split_in else [x]
    if final:
        out_specs = list(_split_specs(tm, d, n_p))
        out_shape = [jax.ShapeDtypeStruct((m_p, d), F32), jax.ShapeDtypeStruct((m - m_p, d), F32)]
    else:
        out_specs = [pl.BlockSpec((tm, d), row), pl.BlockSpec((tm, d), row)]
        out_shape = [jax.ShapeDtypeStruct((m, d), F32), jax.ShapeDtypeStruct((m, d), BF16)]
    return pl.pallas_call(
        functools.partial(_resid_kernel, n_p, split_in, final),
        grid=(m // tm,),
        in_specs=[pl.BlockSpec((tm, k), row),
                  pl.BlockSpec((None, k, d), lambda i: (layer, 0, 0), pipeline_mode=pl.Buffered(1)),
                  *x_specs,
                  pl.BlockSpec((1, d), lambda i: (0, 0))],
        out_specs=out_specs,
        out_shape=out_shape,
        compiler_params=_compiler_params(("arbitrary",)),
        name=name,
    )(a, w, *x_args, g.reshape(1, d))


def _merge_kernel(n_p, h_ref, op_ref, os_ref, wm_ref, wb_ref, y_ref):
    tn = y_ref.shape[1]
    gates = _dot(h_ref[...], wm_ref[...])
    o = jnp.where(pl.program_id(1) < n_p, op_ref[...], os_ref[...])
    acc = None
    for n in range(N_BRANCH):
        br = _dot(o[:, n * BRANCH_WIDTH:(n + 1) * BRANCH_WIDTH], wb_ref[n])
        term = _sigmoid(gates[:, n * tn:(n + 1) * tn]) * br
        acc = term if acc is None else acc + term
    y_ref[...] = acc.astype(y_ref.dtype)


def _merge(h, o, wm_packed, wb, layer, tn):
    m, d = h.shape
    o_p, o_s = o
    tm = _tile(math.gcd(o_p.shape[0], o_s.shape[0]), 1024)
    n_p = o_p.shape[0] // tm
    return pl.pallas_call(
        functools.partial(_merge_kernel, n_p),
        grid=(d // tn, m // tm),
        in_specs=[pl.BlockSpec((tm, d), lambda j, i: (i, 0)),
                  pl.BlockSpec((tm, d), lambda j, i: (jnp.minimum(i, n_p - 1), 0)),
                  pl.BlockSpec((tm, d), lambda j, i: (jnp.maximum(i - n_p, 0), 0)),
                  pl.BlockSpec((d, N_BRANCH * tn), lambda j, i: (0, j)),
                  pl.BlockSpec((None, N_BRANCH, BRANCH_WIDTH, tn), lambda j, i: (layer, 0, 0, j))],
        out_specs=pl.BlockSpec((tm, tn), lambda j, i: (i, j)),
        out_shape=jax.ShapeDtypeStruct((m, d), BF16),
        compiler_params=_compiler_params(("arbitrary", "arbitrary")),
        name="merge",
    )(h, o_p, o_s, wm_packed, wb)


def _ffn_gate(a, prev1, prev2, u, cw_ref, cb_ref):
    conv = cw_ref[2:3, :] * a + cw_ref[1:2, :] * prev1 + cw_ref[0:1, :] * prev2 + cb_ref[...]
    return (_silu(conv) * u)


def _ffn_in_kernel(n_p, tiles_per_stream, seq_s, h_ref, wa_ref, wu_ref, cw_ref, cb_ref, cin_ref,
                   g_ref, cache_p_ref, cache_s_ref, buf_ref):
    tm = h_ref.shape[0]
    i = pl.program_id(1)

    def up_project():
        h = h_ref[...]
        return _dot(h, wa_ref[...]), _dot(h, wu_ref[...])

    @pl.when(jnp.logical_and(i < n_p, i % tiles_per_stream == 0))
    def _():
        buf_ref[0:SUBLANE, :] = jnp.zeros((SUBLANE, buf_ref.shape[1]), F32)

    @pl.when(i < n_p)
    def _():
        a, u = up_project()
        buf_ref[SUBLANE:SUBLANE + tm, :] = a
        prev1 = buf_ref[SUBLANE - 1:SUBLANE - 1 + tm, :]
        prev2 = buf_ref[SUBLANE - 2:SUBLANE - 2 + tm, :]
        g_ref[...] = _ffn_gate(a, prev1, prev2, u, cw_ref, cb_ref).astype(g_ref.dtype)
        last2 = a[tm - 2:tm, :]
        cache_p_ref[0] = last2
        buf_ref[SUBLANE - 2:SUBLANE, :] = last2

    @pl.when(i >= n_p)
    def _():
        a, u = up_project()
        for s in range(tm // seq_s):
            rows = slice(s * seq_s, (s + 1) * seq_s)
            a_s = a[rows]
            buf_ref[0:SUBLANE, :] = cin_ref[s]
            buf_ref[SUBLANE:SUBLANE + seq_s, :] = a_s
            prev1 = buf_ref[SUBLANE - 1:SUBLANE - 1 + seq_s, :]
            prev2 = buf_ref[SUBLANE - 2:SUBLANE - 2 + seq_s, :]
            g_ref[rows, :] = _ffn_gate(a_s, prev1, prev2, u[rows], cw_ref, cb_ref).astype(g_ref.dtype)
            cache_s_ref[s] = a_s[seq_s - 2:seq_s, :]


def _ffn_in(h, wa, wu, cw, cb, cache_in, layer, n_prompt_streams, prompt_seq, seq_s, tn):
    m, d = h.shape
    m_p = n_prompt_streams * prompt_seq
    n_s_streams = (m - m_p) // seq_s
    tm = _tile(math.gcd(prompt_seq, m - m_p), 1024)
    tiles_per_stream = prompt_seq // tm
    n_p = m_p // tm
    spt = tm // seq_s
    dff = wa.shape[2]
    s_blk = lambda j, i: (jnp.maximum(i - n_p, 0), 0, j)
    w_blk = lambda j, i: (layer, 0, j)
    return pl.pallas_call(
        functools.partial(_ffn_in_kernel, n_p, tiles_per_stream, seq_s),
        grid=(dff // tn, m // tm),
        in_specs=[pl.BlockSpec((tm, d), lambda j, i: (i, 0)),
                  pl.BlockSpec((None, d, tn), w_blk),
                  pl.BlockSpec((None, d, tn), w_blk),
                  pl.BlockSpec((None, FFN_CONV, tn), w_blk),
                  pl.BlockSpec((None, 1, tn), w_blk),
                  pl.BlockSpec((None, spt, SUBLANE, tn), lambda j, i: (layer,) + s_blk(j, i))],
        out_specs=[pl.BlockSpec((tm, tn), lambda j, i: (i, j)),
                   pl.BlockSpec((1, FFN_CONV - 1, tn),
                                lambda j, i: (jnp.minimum(i, n_p - 1) // tiles_per_stream, 0, j)),
                   pl.BlockSpec((spt, FFN_CONV - 1, tn), s_blk)],
        out_shape=[jax.ShapeDtypeStruct((m, dff), BF16),
                   jax.ShapeDtypeStruct((n_prompt_streams, FFN_CONV - 1, dff), F32),
                   jax.ShapeDtypeStruct((n_s_streams, FFN_CONV - 1, dff), F32)],
        scratch_shapes=[pltpu.VMEM((SUBLANE + tm, tn), F32)],
        compiler_params=_compiler_params(("arbitrary", "arbitrary")),
        name="ffn_in",
    )(h, wa, wu, cw, cb, cache_in)


def _level_matrix():
    t = np.arange(L)[:, None]
    u = np.arange(L)[None, :]
    blocks = [(u <= t)]
    for l in range(N_LEVELS):
        n = 1 << l
        r = (t & ~(2 * n - 1)) + n - 1
        upper = (t & n) != 0
        blocks.append(np.where(upper, (u > r) & (u <= t), (u > t) & (u <= r)))
    blocks.append(u > t)
    return np.concatenate(blocks, axis=0).astype(np.float32)


def _pair_levels():
    r = np.arange(GROUP * L)
    x = r[:, None] ^ r[None, :]
    code = np.full(x.shape, -1, np.int32)
    code[x == 0] = 0
    for l in range(N_LEVELS):
        code[(x >> l) == 1] = l + 1
    code[r[None, :] > r[:, None]] = -1
    return code


def _stack(x):
    return jnp.concatenate([x[:, h * HEAD_DV:(h + 1) * HEAD_DV] for h in range(N_HEADS)], axis=0)


def _stack_cols(x, first):
    return jnp.concatenate([x[:, first + h:first + h + 1] for h in range(N_HEADS)], axis=0)


def _groups(x):
    gl = GROUP * L
    return [x[g * gl:(g + 1) * gl] for g in range(N_HEADS // GROUP)]


def _pair_product(a, b):
    return [_dot_nt(ag, bg) for ag, bg in zip(_groups(a), _groups(b))]


def _block_diag(x):
    z = jnp.zeros((L, HEAD_DV), x.dtype)
    rows = [jnp.concatenate([x[h * L:(h + 1) * L] if j == h else z for j in range(N_HEADS)], axis=1)
            for h in range(N_HEADS)]
    return jnp.concatenate(rows, axis=0)


def _store_heads(o_ref, off, y):
    for h in range(N_HEADS):
        o_ref[:, off + h * HEAD_DV:off + (h + 1) * HEAD_DV] = y[h * L:(h + 1) * L].astype(o_ref.dtype)


def _head_rms(o, gain, gate):
    y = o * lax.rsqrt(jnp.mean(o * o, axis=-1, keepdims=True) + EPS)
    return y * gain * _silu(gate)


def _gla_type(q, k, v, g, gate, gain, st_ref, mall, lvl, o_ref, o_off):
    g_hi, g_lo = _split2(g * LOG2_E)
    d2 = _dot(mall, jnp.concatenate([g_hi, g_lo], axis=1))
    w = g.shape[1]
    e_all = jnp.exp2(d2[:, :w] + d2[:, w:])

    def e_rows(r):
        return _stack(e_all[r * L:(r + 1) * L])

    qs, ks, vs = _stack(q), _stack(k), _stack(v)
    att = [jnp.where(lvl == 0, a, 0.0) for a in _pair_product(qs.astype(BF16), ks.astype(BF16))]
    for l in range(N_LEVELS):
        yield
        e_l = e_rows(l + 1)
        a_l = _pair_product((qs * e_l).astype(BF16), (ks * e_l).astype(BF16))
        att = [jnp.where(lvl == l + 1, a, prev) for a, prev in zip(a_l, att)]
    yield
    st = [st_ref[0, h] for h in range(N_HEADS)]
    vb = vs.astype(BF16)
    q_decayed = (qs * e_rows(0)).astype(BF16)
    q_state = jnp.concatenate(
        [_dot_nt(q_decayed[h * L:(h + 1) * L], st[h].astype(BF16)) for h in range(N_HEADS)], axis=0)
    o = jnp.concatenate([_dot(a.astype(BF16), vg) for a, vg in zip(att, _groups(vb))], axis=0) + q_state
    _store_heads(o_ref, o_off, _head_rms(o, gain, _stack(gate)))
    upd = _dot_tn(_block_diag(vb), (ks * e_rows(N_LEVELS + 1)).astype(BF16))
    for h in range(N_HEADS):
        decay_end = e_all[L - 1:L, h * HEAD_DV:(h + 1) * HEAD_DV]
        st_ref[0, h] = st[h] * decay_end + upd[h * HEAD_DV:(h + 1) * HEAD_DV]


CONV_BASE = SUBLANE - (GDN_CONV - 1)


def _lane(layer, proj_ref, cos_ref, sin_ref, mall_ref, lvl_ref, wgk_ref, bgk_ref, norms_ref, lb_ref,
          convw_ref, gdnp_ref, o_ref, xe_ref, sgla_ref, shg_ref, sgdn_ref, sret_ref):
    base = CONV_BASE
    lvl = lvl_ref[...]
    incl = lvl >= 0
    mall = mall_ref[...]
    tri = mall[0:L]
    hl = N_HEADS * L

    def col(off, width=512):
        return proj_ref[:, off:off + width]

    z = _dot(col(P_GLA_LR, LANE).astype(BF16), wgk_ref[...]) + bgk_ref[...]
    g_gla = (jnp.minimum(z, 0.0) - jnp.log(1.0 + jnp.exp(-jnp.abs(z)))) * (1.0 / GLA_GATE_NORM)
    gla = _gla_type(col(P_GLA_Q) * GLA_DK ** -0.5, col(P_GLA_K), col(P_GLA_V), g_gla, col(P_GLA_G),
                    norms_ref[0:1, :], sgla_ref, mall, lvl, o_ref, 0)
    if layer == 0:
        lb = jnp.zeros((1, 512), F32)
    else:
        logits = lb_ref[...]
        ex = jnp.exp(logits - jnp.max(logits, axis=0, keepdims=True))
        soft = ex / jnp.sum(ex, axis=0, keepdims=True)
        lb = jnp.sum(soft[1:layer + 1], axis=0, keepdims=True)
    f = lb + (1.0 - lb) * _sigmoid(col(P_HG_F))
    hgrn = _gla_type(_silu(col(P_HG_Q)), 1.0 - f, col(P_HG_I), jnp.log(f), col(P_HG_G),
                     norms_ref[1:2, :], shg_ref, mall, lvl, o_ref, BRANCH_WIDTH)

    xe_ref[SUBLANE:SUBLANE + L, :] = col(P_GDN_QKV, GDN_QKV)
    conv = convw_ref[0:1, :] * xe_ref[base:base + L, :]
    for j in range(1, GDN_CONV):
        conv = conv + convw_ref[j:j + 1, :] * xe_ref[base + j:base + j + L, :]
    xe_ref[base:SUBLANE, :] = xe_ref[SUBLANE + L - (GDN_CONV - 1):SUBLANE + L, :]
    act = _silu(conv)
    ba = col(P_GDN_BA, LANE)
    beta = _stack_cols(_sigmoid(ba), 0)
    log_a = -jnp.exp(gdnp_ref[0:1, :]) * _softplus(ba + gdnp_ref[1:2, :])
    la_hi, la_mid, la_lo = _split3(log_a)
    b_all = _dot(tri, la_hi) + (_dot(tri, la_mid) + _dot(tri, la_lo))
    b_col = _stack_cols(b_all, N_HEADS)
    b_rows = b_all.T
    b_row = [jnp.concatenate([b_rows[N_HEADS + GROUP * g + i:N_HEADS + GROUP * g + i + 1, :]
                              for i in range(GROUP)], axis=1) for g in range(N_HEADS // GROUP)]
    b_end = jnp.concatenate(
        [jnp.broadcast_to(b_all[L - 1:L, N_HEADS + h:N_HEADS + h + 1], (L, 1)) for h in range(N_HEADS)], axis=0)
    dq, dk, dv = _stack(act[:, 0:512]), _stack(act[:, 512:1024]), _stack(act[:, 1024:1536])
    qn = dq * lax.rsqrt(jnp.sum(dq * dq, axis=-1, keepdims=True) + EPS) * HEAD_DV ** -0.5
    kn = dk * lax.rsqrt(jnp.sum(dk * dk, axis=-1, keepdims=True) + EPS)
    qb, kb = qn.astype(BF16), kn.astype(BF16)
    dec = [jnp.exp(jnp.where(incl, bc - br, -1e30)) for bc, br in zip(_groups(b_col), b_row)]
    e_b = jnp.exp(b_col)
    st = [sgdn_ref[0, h] for h in range(N_HEADS)]
    kq_state = [_dot(jnp.concatenate([kb[h * L:(h + 1) * L], qb[h * L:(h + 1) * L]], axis=0),
                     st[h].astype(BF16)) for h in range(N_HEADS)]
    k_state = jnp.concatenate([x[0:L] for x in kq_state], axis=0)
    q_state = jnp.concatenate([x[L:2 * L] for x in kq_state], axis=0)
    rhs = beta * (dv - e_b * k_state)
    p_groups = [-(jnp.where(lvl > 0, kk * dec_g, 0.0) * beta_g)
                for kk, dec_g, beta_g in zip(_pair_product(kb, kb), dec, _groups(beta))]
    u_groups = _groups(rhs)
    next(gla)
    next(hgrn)
    for j in range(N_LEVELS):
        yield
        u_groups = [u + _dot_hi(p, u) for p, u in zip(p_groups, u_groups)]
        if j + 1 < N_LEVELS:
            p_groups = [_dot_hi(p, p) for p in p_groups]
        next(gla)
        next(hgrn)
    yield
    ub = jnp.concatenate(u_groups, axis=0).astype(BF16)
    att = [(qk * dec_g).astype(BF16) for qk, dec_g in zip(_pair_product(qb, kb), dec)]
    o = jnp.concatenate([_dot(a, ug) for a, ug in zip(att, _groups(ub))], axis=0) + e_b * q_state
    _store_heads(o_ref, 2 * BRANCH_WIDTH, _head_rms(o, norms_ref[2:3, :], _stack(col(P_GDN_Z))))
    upd = _dot_tn((kn * jnp.exp(b_end - b_col)).astype(BF16), _block_diag(ub))
    for h in range(N_HEADS):
        decay_end = jnp.exp(b_all[L - 1:L, N_HEADS + h:N_HEADS + h + 1])
        sgdn_ref[0, h] = decay_end * st[h] + upd[:, h * HEAD_DV:(h + 1) * HEAD_DV]

    for gen in (gla, hgrn):
        for _ in gen:
            pass

    cos = jnp.concatenate([cos_ref[...]] * N_HEADS, axis=0)
    sin = jnp.concatenate([sin_ref[...]] * N_HEADS, axis=0)
    rq, rk = _stack(col(P_RET_Q)), _stack(col(P_RET_K))
    vb = _stack(col(P_RET_V)).astype(BF16)
    row = lax.broadcasted_iota(jnp.int32, (hl, 1), 0)
    t_col = jnp.bitwise_and(row, L - 1).astype(F32)
    lg = jnp.zeros((hl, 1), F32)
    for h in range(N_HEADS):
        lg = jnp.where(jnp.right_shift(row, N_LEVELS) == h, math.log(1.0 - 2.0 ** (-5.0 - h)), lg)
    gl = GROUP * L
    t_minus_s = (lax.broadcasted_iota(jnp.int32, (gl, 1), 0)
                 - lax.broadcasted_iota(jnp.int32, (1, gl), 1)).astype(F32)
    dec = [jnp.exp(jnp.where(incl, t_minus_s * lg_g, -1e30)) for lg_g in _groups(lg)]
    qr = (rq * cos + pltpu.roll(rq, HEAD_DV // 2, axis=1) * sin).astype(BF16)
    kr = (rk * cos + pltpu.roll(rk, HEAD_DV // 2, axis=1) * sin) * HEAD_DV ** -0.5
    kb = kr.astype(BF16)
    st = [sret_ref[0, h] for h in range(N_HEADS)]
    q_state = jnp.concatenate(
        [_dot(qr[h * L:(h + 1) * L], st[h].astype(BF16)) for h in range(N_HEADS)], axis=0)
    att = [(qk * dec_g).astype(BF16) for qk, dec_g in zip(_pair_product(qr, kb), dec)]
    o = (jnp.concatenate([_dot(a, vg) for a, vg in zip(att, _groups(vb))], axis=0)
         + jnp.exp((t_col + 1.0) * lg) * q_state)
    mu = jnp.mean(o, axis=-1, keepdims=True)
    oc = o - mu
    y = oc * lax.rsqrt(jnp.mean(oc * oc, axis=-1, keepdims=True) + EPS)
    y = (y * norms_ref[3:4, :] + norms_ref[4:5, :]) * _silu(_stack(col(P_RET_G)))
    _store_heads(o_ref, 3 * BRANCH_WIDTH, y)
    k_end = (kr * jnp.exp((L - 1.0 - t_col) * lg)).astype(BF16)
    upd = _dot_tn(k_end, _block_diag(vb))
    for h in range(N_HEADS):
        gamma_l = math.exp(L * math.log(1.0 - 2.0 ** (-5.0 - h)))
        sret_ref[0, h] = gamma_l * st[h] + upd[:, h * HEAD_DV:(h + 1) * HEAD_DV]


N_LANES = 2


def _run_lanes(lanes):
    lanes = list(lanes)
    while lanes:
        for lane in list(lanes):
            if next(lane, StopIteration) is StopIteration:
                lanes.remove(lane)


def _write_states(k, carry, xe_ref, gla_out, hg_out, gdn_out, ret_out, cache_out):
    sgla_ref, shg_ref, sgdn_ref, sret_ref = carry
    for h in range(N_HEADS):
        gla_out[k, h] = sgla_ref[k, 0, h].T[0:GLA_DK]
        hg_out[k, h] = shg_ref[k, 0, h].T
    gdn_out[k] = sgdn_ref[k, 0]
    ret_out[k] = sret_ref[k, 0]
    cache_out[k] = xe_ref[k, CONV_BASE:SUBLANE, :]


def _mixer_prompt_kernel(layer, n_chunks, *refs):
    proj_refs, refs = refs[:N_LANES], refs[N_LANES:]
    consts, refs = refs[:10], refs[10:]
    o_ref, outs, (xe_ref, *carry) = refs[0], refs[1:6], refs[6:]
    c = pl.program_id(0)

    @pl.when(c == 0)
    def _():
        for ref in carry:
            ref[...] = jnp.zeros(ref.shape, F32)
        xe_ref[...] = jnp.zeros(xe_ref.shape, F32)

    _run_lanes(_lane(layer, proj_refs[k], *consts, o_ref.at[k], xe_ref.at[k],
                     *[s.at[k] for s in carry]) for k in range(N_LANES))

    @pl.when(c == n_chunks - 1)
    def _():
        for k in range(N_LANES):
            _write_states(k, carry, xe_ref, *outs)


def _mixer_sample_kernel(layer, proj_ref, sgla_in, shg_in, sgdn_in, sret_in, cgdn_in, *refs):
    consts, refs = refs[:10], refs[10:]
    o_ref, outs, (xe_ref, *carry) = refs[0], refs[1:6], refs[6:]
    sgla_ref, shg_ref, sgdn_ref, sret_ref = carry
    key_pad = jnp.zeros((HEAD_DV - GLA_DK, HEAD_DV), F32)
    for k in range(N_LANES):
        for h in range(N_HEADS):
            sgla_ref[k, 0, h] = jnp.concatenate([sgla_in[k, h], key_pad], axis=0).T
            shg_ref[k, 0, h] = shg_in[k, h].T
        sgdn_ref[k, 0] = sgdn_in[k]
        sret_ref[k, 0] = sret_in[k]
        xe_ref[k, CONV_BASE:SUBLANE, :] = cgdn_in[k]
    _run_lanes(_lane(layer, proj_ref.at[pl.ds(k * L, L)], *consts, o_ref.at[pl.ds(k * L, L)], xe_ref.at[k],
                     *[s.at[k] for s in carry]) for k in range(N_LANES))
    for k in range(N_LANES):
        _write_states(k, carry, xe_ref, *outs)


def _mixers(layer, proj, states, cache, cos_t, sin_t, mall, lvl, wgk, bgk, norms, lb_logits,
            convw, gdnp, n_prompt_streams, prompt_seq):
    t_total = proj.shape[0]
    chunks_per_stream = prompt_seq // L
    t_prompt = n_prompt_streams * prompt_seq
    n_sample = states[0].shape[1]
    past_block = PAST_LEN // L
    assert n_prompt_streams == N_LANES and n_sample % N_LANES == 0

    def layer_block(a):
        return pl.BlockSpec((None,) + a.shape[1:], lambda c: (layer,) + (0,) * (a.ndim - 1))

    const = lambda a: pl.BlockSpec(a.shape, lambda c: (0,) * a.ndim)
    per_stream = list(states) + [cache]
    lanes_out = lambda idx: [pl.BlockSpec((N_LANES,) + a.shape[2:],
                                          lambda c, a=a: (idx(c),) + (0,) * (a.ndim - 2))
                             for a in per_stream]
    out_shapes = lambda n: [jax.ShapeDtypeStruct((n,) + a.shape[2:], F32) for a in per_stream]
    carry = pltpu.VMEM((N_LANES, 1, N_HEADS, HEAD_DV, HEAD_DV), F32)
    scratch = [pltpu.VMEM((N_LANES, SUBLANE + L, GDN_QKV), F32), carry, carry, carry, carry]

    def consts(pos_block):
        specs = [pl.BlockSpec((L, HEAD_DV), lambda c: (pos_block(c), 0)),
                 pl.BlockSpec((L, HEAD_DV), lambda c: (pos_block(c), 0)),
                 const(mall), const(lvl), layer_block(wgk), layer_block(bgk),
                 layer_block(norms), const(lb_logits), layer_block(convw), layer_block(gdnp)]
        return specs, (cos_t, sin_t, mall, lvl, wgk, bgk, norms, lb_logits, convw, gdnp)

    c_specs, c_args = consts(lambda c: c)
    o_p, *st_p = pl.pallas_call(
        functools.partial(_mixer_prompt_kernel, layer, chunks_per_stream),
        grid=(chunks_per_stream,),
        in_specs=[*[pl.BlockSpec((L, N_PACK), lambda c, k=k: (k * chunks_per_stream + c, 0))
                    for k in range(N_LANES)], *c_specs],
        out_specs=[pl.BlockSpec((N_LANES, L, D_MODEL), lambda c: (0, c, 0)), *lanes_out(lambda c: 0)],
        out_shape=[jax.ShapeDtypeStruct((N_LANES, prompt_seq, D_MODEL), BF16), *out_shapes(N_LANES)],
        scratch_shapes=scratch,
        compiler_params=_compiler_params(("arbitrary",)),
        name="mixers_prompt",
    )(*[proj] * N_LANES, *c_args)

    c_specs, c_args = consts(lambda s: past_block)
    first_block = t_prompt // (N_LANES * L)
    o_s, *st_s = pl.pallas_call(
        functools.partial(_mixer_sample_kernel, layer),
        grid=(n_sample // N_LANES,),
        in_specs=[pl.BlockSpec((N_LANES * L, N_PACK), lambda s: (first_block + s, 0)),
                  *[pl.BlockSpec((None, N_LANES) + a.shape[2:], lambda s, a=a: (layer, s) + (0,) * (a.ndim - 2))
                    for a in per_stream],
                  *c_specs],
        out_specs=[pl.BlockSpec((N_LANES * L, D_MODEL), lambda s: (s, 0)), *lanes_out(lambda s: s)],
        out_shape=[jax.ShapeDtypeStruct((t_total - t_prompt, D_MODEL), BF16), *out_shapes(n_sample)],
        scratch_shapes=scratch,
        compiler_params=_compiler_params(("arbitrary",)),
        name="mixers_sample",
    )(proj, *per_stream, *c_args)
    return (o_p.reshape(t_prompt, D_MODEL), o_s), st_p, st_s


def _pad_heads(w, dk):
    lead = w.shape[:-1]
    w = w.reshape(lead + (N_HEADS, dk))
    w = jnp.pad(w, [(0, 0)] * len(lead) + [(0, 0), (0, HEAD_DV - dk)])
    return w.reshape(lead + (N_HEADS * HEAD_DV,))


def _pad_cols(w, width):
    return jnp.pad(w, [(0, 0)] * (w.ndim - 1) + [(0, width - w.shape[-1])])


def _mix_pieces():
    pieces = []
    for h in range(N_HEADS):
        pieces.append((P_GLA_Q + h * HEAD_DV, h * GLA_DK, GLA_DK, HEAD_DV))
        pieces.append((P_GLA_K + h * HEAD_DV, 256 + h * GLA_DK, GLA_DK, HEAD_DV))
    pieces += [(P_GLA_V, 512, 512, 512), (P_GLA_G, 1024, 512, 512), (P_GLA_LR, 1536, GLA_LOW_RANK, LANE),
               (P_HG_Q, 1552, 512, 512), (P_HG_F, 2064, 512, 512), (P_HG_I, 2576, 512, 512),
               (P_HG_G, 3088, 512, 512), (P_GDN_QKV, 3600, GDN_QKV, GDN_QKV), (P_GDN_Z, 5136, 512, 512),
               (P_GDN_BA, 5648, 2 * N_HEADS, LANE), (P_RET_Q, 5656, 512, 512), (P_RET_K, 6168, 512, 512),
               (P_RET_V, 6680, 512, 512), (P_RET_G, 7192, 512, 512)]
    return pieces


def _pack_w_in_kernel(merge_tn, w_ref, mix_ref, wm_ref):
    n_in = w_ref.shape[1]

    def window(off, n):
        a0 = off // LANE * LANE
        a1 = min(-(-(off + n) // LANE) * LANE, n_in)
        return w_ref[:, a0:a1][:, off - a0:off - a0 + n].astype(BF16)

    for dst, src, n, padded in _mix_pieces():
        if padded != n:
            mix_ref[:, dst:dst + padded] = jnp.zeros((mix_ref.shape[0], padded), BF16)
        mix_ref[:, dst:dst + n] = window(src, n)
    for j in range(D_MODEL // merge_tn):
        for n in range(N_BRANCH):
            dst = (j * N_BRANCH + n) * merge_tn
            wm_ref[:, dst:dst + merge_tn] = window(N_MIX_IN + n * D_MODEL + j * merge_tn, merge_tn)


def _pack_w_in(w, layer, merge_tn):
    _, k, n_in = w.shape
    tk = 128
    return pl.pallas_call(
        functools.partial(_pack_w_in_kernel, merge_tn),
        grid=(k // tk,),
        in_specs=[pl.BlockSpec((None, tk, n_in), lambda i: (layer, i, 0))],
        out_specs=[pl.BlockSpec((tk, N_PACK), lambda i: (i, 0)),
                   pl.BlockSpec((tk, N_BRANCH * D_MODEL), lambda i: (i, 0))],
        out_shape=[jax.ShapeDtypeStruct((k, N_PACK), BF16),
                   jax.ShapeDtypeStruct((k, N_BRANCH * D_MODEL), BF16)],
        compiler_params=_compiler_params(("parallel",)),
        name="pack_w_in",
    )(w)


def _cast_rows_kernel(n_valid, w_ref, o_ref):
    i = pl.program_id(1)

    @pl.when(i < n_valid)
    def _():
        o_ref[...] = w_ref[...].astype(o_ref.dtype)

    @pl.when(i >= n_valid)
    def _():
        o_ref[...] = jnp.zeros(o_ref.shape, o_ref.dtype)


def _cast_rows(w, rows_out, tk):
    depth, r, c = w.shape
    n_valid = r // tk
    assert r % tk == 0 and rows_out % tk == 0
    return pl.pallas_call(
        functools.partial(_cast_rows_kernel, n_valid),
        grid=(depth, rows_out // tk),
        in_specs=[pl.BlockSpec((None, tk, c), lambda l, i: (l, jnp.minimum(i, n_valid - 1), 0))],
        out_specs=pl.BlockSpec((None, tk, c), lambda l, i: (l, i, 0)),
        out_shape=jax.ShapeDtypeStruct((depth, rows_out, c), BF16),
        compiler_params=_compiler_params(("parallel", "arbitrary")),
        name="cast_rows",
    )(w)


def _split_ffn_in_kernel(w_ref, a_ref, u_ref):
    pad = jnp.zeros((a_ref.shape[0], D_FF_PAD - D_FF), BF16)
    a_ref[:, 0:D_FF] = w_ref[:, 0:D_FF].astype(BF16)
    a_ref[:, D_FF:D_FF_PAD] = pad
    u_ref[:, 0:D_FF] = w_ref[:, D_FF:2 * D_FF].astype(BF16)
    u_ref[:, D_FF:D_FF_PAD] = pad


def _split_ffn_in(w, tk):
    depth, d, _ = w.shape
    blk = lambda c: pl.BlockSpec((None, tk, c), lambda l, i: (l, i, 0))
    out = jax.ShapeDtypeStruct((depth, d, D_FF_PAD), BF16)
    return pl.pallas_call(
        _split_ffn_in_kernel,
        grid=(depth, d // tk),
        in_specs=[blk(2 * D_FF)],
        out_specs=[blk(D_FF_PAD), blk(D_FF_PAD)],
        out_shape=[out, out],
        compiler_params=_compiler_params(("parallel", "parallel")),
        name="split_ffn_in",
    )(w)


def _rope_tables(n_pos):
    pos = np.arange(n_pos, dtype=np.float64)
    inv = 1.0 / (ROPE_BASE ** (np.arange(0, HEAD_DV, 2, dtype=np.float64) / HEAD_DV))
    ang = pos[:, None] * inv[None, :]
    cos, sin = np.cos(ang), np.sin(ang)
    return (jnp.asarray(np.concatenate([cos, cos], axis=1), F32),
            jnp.asarray(np.concatenate([-sin, sin], axis=1), F32))


def kernel(x_prompt, x_sample, state_gla, state_hgrn, state_gdn, state_ret, cache_gdn_conv, cache_ffn_conv, norm_mix_g, w_in, gla_w_gk, gla_b_gk, gla_norm_g, hgrn_lb_logits, hgrn_norm_g, gdn_conv_w, gdn_a_log, gdn_dt_bias, gdn_norm_g, ret_norm_g, ret_norm_b, w_branch, w_out, norm_ffn_g, w_ffn_in, ffn_conv_w, ffn_conv_b, w_ffn_out, norm_final_g):
    nb, seq, d = x_prompt.shape
    ns, dseq, _ = x_sample.shape
    assert d == D_MODEL and dseq == L and seq % L == 0
    tp = nb * seq
    x = (x_prompt.reshape(tp, d), x_sample.reshape(ns * dseq, d))

    cos_t, sin_t = _rope_tables(max(seq, PAST_LEN + dseq))
    mall = jnp.asarray(_level_matrix(), BF16)
    lvl = jnp.asarray(_pair_levels())
    merge_tn = 512
    ffn_tn = 512

    wgk = jnp.pad(_pad_heads(gla_w_gk, GLA_DK), ((0, 0), (0, LANE - GLA_LOW_RANK), (0, 0))).astype(BF16)
    bgk = _pad_heads(gla_b_gk, GLA_DK)[:, None, :]
    norms = jnp.stack([gla_norm_g, hgrn_norm_g, gdn_norm_g, ret_norm_g, ret_norm_b]
                      + [jnp.zeros_like(gla_norm_g)] * 3, axis=1)
    head_lanes = ((0, 0), (N_HEADS, LANE - 2 * N_HEADS))
    gdnp = jnp.stack([jnp.pad(gdn_a_log, head_lanes), jnp.pad(gdn_dt_bias, head_lanes)], axis=1)
    w_branch_b = _cast_rows(w_branch.reshape(DEPTH, N_BRANCH * BRANCH_WIDTH, d), N_BRANCH * BRANCH_WIDTH,
                            256).reshape(w_branch.shape)
    w_out_b = _cast_rows(w_out, d, 256)
    wa, wu = _split_ffn_in(w_ffn_in, 128)
    w_dn = _cast_rows(w_ffn_out, D_FF_PAD, 128)
    cw = _pad_cols(ffn_conv_w, D_FF_PAD)
    cb = _pad_cols(ffn_conv_b, D_FF_PAD)[:, None, :]
    cache_in = jnp.pad(cache_ffn_conv,
                       ((0, 0), (0, 0), (SUBLANE - (FFN_CONV - 1), 0), (0, D_FF_PAD - D_FF)))
    states = (state_gla, state_hgrn, state_gdn, state_ret)

    h = _rmsnorm(x[0], x[1], norm_mix_g[0], BF16)
    new_p, new_s = [], []
    for l in range(DEPTH):
        w_mix, w_merge = _pack_w_in(w_in, l, merge_tn)
        proj = _matmul(h, w_mix, 2816, F32)
        o, st_p, st_s = _mixers(l, proj, states, cache_gdn_conv, cos_t, sin_t, mall, lvl, wgk, bgk, norms,
                                hgrn_lb_logits, gdn_conv_w, gdnp, nb, seq)
        y = _merge(h, o, w_merge, w_branch_b, l, merge_tn)
        x, h2 = _resid_matmul_norm(y, w_out_b, l, x, norm_ffn_g[l], tp, 512, False, "out_proj")
        g, c_ffn_p, c_ffn_s = _ffn_in(h2, wa, wu, cw, cb, cache_in, l, nb, seq, dseq, ffn_tn)
        if l + 1 < DEPTH:
            x, h = _resid_matmul_norm(g, w_dn, l, x, norm_mix_g[l + 1], tp, 256, False, "ffn_out")
        else:
            y_p, y_s = _resid_matmul_norm(g, w_dn, l, x, norm_final_g, tp, 256, True, "ffn_out_final")
        new_p.append(st_p + [c_ffn_p[:, :, :D_FF]])
        new_s.append(st_s + [c_ffn_s[:, :, :D_FF]])

    y_prompt = y_p.reshape(nb, seq, d)
    y_sample = y_s.reshape(ns, dseq, d)
    p_out = tuple(jnp.stack([st[i] for st in new_p]) for i in range(6))
    s_out = tuple(jnp.stack([st[i] for st in new_s]) for i in range(6))
    return (y_prompt, y_sample) + p_out + s_out
```

```python
import functools
import math

import numpy as np
import jax
import jax.numpy as jnp
from jax import lax
from jax.experimental import pallas as pl
from jax.experimental.pallas import tpu as pltpu

F32 = jnp.float32
BF16 = jnp.bfloat16

D_MODEL = 2048
DEPTH = 2
PAST_LEN = 4096
L = 64
N_BRANCH = 4
BRANCH_WIDTH = 512
N_HEADS = 4
HEAD_DV = 128
GLA_DK = 64
GLA_LOW_RANK = 16
GLA_GATE_NORM = 16.0
GDN_CONV = 4
GDN_QKV = 1536
ROPE_BASE = 10000.0
D_FF = 5504
D_FF_PAD = 5632
FFN_CONV = 3
EPS = 1e-6
N_MIX_IN = 7704

LANE = 128
SUBLANE = 8
VMEM_LIMIT = 56 * 1024 * 1024

P_GLA_Q, P_GLA_K, P_GLA_V, P_GLA_G, P_GLA_LR = 0, 512, 1024, 1536, 2048
P_HG_Q, P_HG_F, P_HG_I, P_HG_G = 2176, 2688, 3200, 3712
P_GDN_QKV, P_GDN_Z, P_GDN_BA = 4224, 5760, 6272
P_RET_Q, P_RET_K, P_RET_V, P_RET_G = 6400, 6912, 7424, 7936
N_PACK = 8448
N_LEVELS = 6
GROUP = 2
LOG2_E = math.log2(math.e)


def _dot(a, b):
    return jnp.dot(a, b, preferred_element_type=F32)


def _dot_nt(a, b):
    return lax.dot_general(a, b, (((1,), (1,)), ((), ())), preferred_element_type=F32)


def _dot_tn(a, b):
    return lax.dot_general(a, b, (((0,), (0,)), ((), ())), preferred_element_type=F32)


def _split3(x):
    hi = x.astype(BF16)
    r = x - hi.astype(F32)
    mid = r.astype(BF16)
    lo = (r - mid.astype(F32)).astype(BF16)
    return hi, mid, lo


def _split2(x):
    hi = x.astype(BF16)
    lo = (x - hi.astype(F32)).astype(BF16)
    return hi, lo


def _dot_hi(a, b):
    a_hi, a_lo = _split2(a)
    b_hi, b_lo = _split2(b)
    return _dot(a_hi, b_hi) + (_dot(a_hi, b_lo) + _dot(a_lo, b_hi))


def _sigmoid(x):
    return 1.0 / (1.0 + jnp.exp(-x))


def _silu(x):
    return x * _sigmoid(x)


def _softplus(x):
    return jnp.maximum(x, 0.0) + jnp.log(1.0 + jnp.exp(-jnp.abs(x)))


def _compiler_params(semantics):
    return pltpu.CompilerParams(dimension_semantics=semantics, vmem_limit_bytes=VMEM_LIMIT)


def _tile(n, pref):
    t = min(n, pref)
    while n % t:
        t //= 2
    return t


def _split_specs(tm, d, n_p):
    return (pl.BlockSpec((tm, d), lambda i: (jnp.minimum(i, n_p - 1), 0)),
            pl.BlockSpec((tm, d), lambda i: (jnp.maximum(i - n_p, 0), 0)))


def _rmsnorm_kernel(n_p, xp_ref, xs_ref, g_ref, h_ref):
    x = jnp.where(pl.program_id(0) < n_p, xp_ref[...], xs_ref[...])
    y = x * lax.rsqrt(jnp.mean(x * x, axis=-1, keepdims=True) + EPS)
    h_ref[...] = (y * g_ref[...]).astype(h_ref.dtype)


def _rmsnorm(xp, xs, g, out_dtype):
    d = xp.shape[1]
    m = xp.shape[0] + xs.shape[0]
    tm = _tile(math.gcd(xp.shape[0], xs.shape[0]), 512)
    n_p = xp.shape[0] // tm
    return pl.pallas_call(
        functools.partial(_rmsnorm_kernel, n_p),
        grid=(m // tm,),
        in_specs=[*_split_specs(tm, d, n_p), pl.BlockSpec((1, d), lambda i: (0, 0))],
        out_specs=pl.BlockSpec((tm, d), lambda i: (i, 0)),
        out_shape=jax.ShapeDtypeStruct((m, d), out_dtype),
        compiler_params=_compiler_params(("arbitrary",)),
        name="rmsnorm",
    )(xp, xs, g.reshape(1, d))


def _mm_kernel(a_ref, w_ref, o_ref):
    o_ref[...] = _dot_nt(a_ref[...], w_ref[...]).astype(o_ref.dtype)


def _matmul(a, w_t, tn, out_dtype):
    m, k = a.shape
    n = w_t.shape[0]
    w = w_t
    tm = _tile(m, 512)
    return pl.pallas_call(
        _mm_kernel,
        grid=(n // tn, m // tm),
        in_specs=[pl.BlockSpec((tm, k), lambda j, i: (i, 0)),
                  pl.BlockSpec((tn, k), lambda j, i: (j, 0))],
        out_specs=pl.BlockSpec((tm, tn), lambda j, i: (i, j)),
        out_shape=jax.ShapeDtypeStruct((m, n), out_dtype),
        compiler_params=_compiler_params(("parallel", "parallel")),
        name="in_proj",
    )(a, w)


def _resid_kernel(n_p, split_in, final, a_ref, w_ref, *refs):
    n_x = 2 if split_in else 1
    x_refs, g_ref, out_refs = refs[:n_x], refs[n_x], refs[n_x + 1:]
    is_prompt = pl.program_id(0) < n_p
    x_old = jnp.where(is_prompt, x_refs[0][...], x_refs[1][...]) if split_in else x_refs[0][...]
    x = x_old + _dot(a_ref[...], w_ref[...])
    h = x * lax.rsqrt(jnp.mean(x * x, axis=-1, keepdims=True) + EPS) * g_ref[...]
    if final:
        hp_ref, hs_ref = out_refs

        @pl.when(is_prompt)
        def _():
            hp_ref[...] = h

        @pl.when(jnp.logical_not(is_prompt))
        def _():
            hs_ref[...] = h
    else:
        out_refs[0][...] = x
        out_refs[1][...] = h.astype(out_refs[1].dtype)


def _resid_matmul_norm(a, w, layer, x, g, m_p, tm, final, name):
    m, k = a.shape
    d = w.shape[2]
    tm = _tile(math.gcd(m_p, m - m_p), tm)
    n_p = m_p // tm
    row = lambda i: (i, 0)
    split_in = isinstance(x, tuple)
    x_specs = list(_split_specs(tm, d, n_p)) if split_in else [pl.BlockSpec((tm, d), row)]
    x_args = list(x) if split_in else [x]
    if final:
        out_specs = list(_split_specs(tm, d, n_p))
        out_shape = [jax.ShapeDtypeStruct((m_p, d), F32), jax.ShapeDtypeStruct((m - m_p, d), F32)]
    else:
        out_specs = [pl.BlockSpec((tm, d), row), pl.BlockSpec((tm, d), row)]
        out_shape = [jax.ShapeDtypeStruct((m, d), F32), jax.ShapeDtypeStruct((m, d), BF16)]
    return pl.pallas_call(
        functools.partial(_resid_kernel, n_p, split_in, final),
        grid=(m // tm,),
        in_specs=[pl.BlockSpec((tm, k), row),
                  pl.BlockSpec((None, k, d), lambda i: (layer, 0, 0), pipeline_mode=pl.Buffered(1)),
                  *x_specs,
                  pl.BlockSpec((1, d), lambda i: (0, 0))],
        out_specs=out_specs,
        out_shape=out_shape,
        compiler_params=_compiler_params(("arbitrary",)),
        name=name,
    )(a, w, *x_args, g.reshape(1, d))


def _merge_kernel(n_p, h_ref, op_ref, os_ref, wm_ref, wb_ref, y_ref):
    tn = y_ref.shape[1]
    gates = _dot_nt(h_ref[...], wm_ref[...])
    o = jnp.where(pl.program_id(1) < n_p, op_ref[...], os_ref[...])
    acc = None
    for n in range(N_BRANCH):
        br = _dot(o[:, n * BRANCH_WIDTH:(n + 1) * BRANCH_WIDTH], wb_ref[n])
        term = _sigmoid(gates[:, n * tn:(n + 1) * tn]) * br
        acc = term if acc is None else acc + term
    y_ref[...] = acc.astype(y_ref.dtype)


def _merge(h, o, wm_packed, wb, layer, tn):
    m, d = h.shape
    o_p, o_s = o
    tm = _tile(math.gcd(o_p.shape[0], o_s.shape[0]), 1024)
    n_p = o_p.shape[0] // tm
    return pl.pallas_call(
        functools.partial(_merge_kernel, n_p),
        grid=(d // tn, m // tm),
        in_specs=[pl.BlockSpec((tm, d), lambda j, i: (i, 0)),
                  pl.BlockSpec((tm, d), lambda j, i: (jnp.minimum(i, n_p - 1), 0)),
                  pl.BlockSpec((tm, d), lambda j, i: (jnp.maximum(i - n_p, 0), 0)),
                  pl.BlockSpec((N_BRANCH * tn, d), lambda j, i: (j, 0)),
                  pl.BlockSpec((None, N_BRANCH, BRANCH_WIDTH, tn), lambda j, i: (layer, 0, 0, j))],
        out_specs=pl.BlockSpec((tm, tn), lambda j, i: (i, j)),
        out_shape=jax.ShapeDtypeStruct((m, d), BF16),
        compiler_params=_compiler_params(("arbitrary", "arbitrary")),
        name="merge",
    )(h, o_p, o_s, wm_packed, wb)


def _ffn_gate(a, prev1, prev2, u, cw_ref, cb_ref):
    conv = cw_ref[2:3, :] * a + cw_ref[1:2, :] * prev1 + cw_ref[0:1, :] * prev2 + cb_ref[...]
    return (_silu(conv) * u)


def _ffn_in_kernel(n_p, tiles_per_stream, seq_s, h_ref, wa_ref, wu_ref, cw_ref, cb_ref, cin_ref,
                   g_ref, cache_p_ref, cache_s_ref, buf_ref):
    tm = h_ref.shape[0]
    i = pl.program_id(1)

    def up_project():
        h = h_ref[...]
        return _dot(h, wa_ref[...]), _dot(h, wu_ref[...])

    @pl.when(jnp.logical_and(i < n_p, i % tiles_per_stream == 0))
    def _():
        buf_ref[0:SUBLANE, :] = jnp.zeros((SUBLANE, buf_ref.shape[1]), F32)

    @pl.when(i < n_p)
    def _():
        a, u = up_project()
        buf_ref[SUBLANE:SUBLANE + tm, :] = a
        prev1 = buf_ref[SUBLANE - 1:SUBLANE - 1 + tm, :]
        prev2 = buf_ref[SUBLANE - 2:SUBLANE - 2 + tm, :]
        g_ref[...] = _ffn_gate(a, prev1, prev2, u, cw_ref, cb_ref).astype(g_ref.dtype)
        last2 = a[tm - 2:tm, :]
        cache_p_ref[0] = last2
        buf_ref[SUBLANE - 2:SUBLANE, :] = last2

    @pl.when(i >= n_p)
    def _():
        a, u = up_project()
        for s in range(tm // seq_s):
            rows = slice(s * seq_s, (s + 1) * seq_s)
            a_s = a[rows]
            buf_ref[0:SUBLANE, :] = cin_ref[s]
            buf_ref[SUBLANE:SUBLANE + seq_s, :] = a_s
            prev1 = buf_ref[SUBLANE - 1:SUBLANE - 1 + seq_s, :]
            prev2 = buf_ref[SUBLANE - 2:SUBLANE - 2 + seq_s, :]
            g_ref[rows, :] = _ffn_gate(a_s, prev1, prev2, u[rows], cw_ref, cb_ref).astype(g_ref.dtype)
            cache_s_ref[s] = a_s[seq_s - 2:seq_s, :]


def _ffn_in(h, wa, wu, cw, cb, cache_in, layer, n_prompt_streams, prompt_seq, seq_s, tn):
    m, d = h.shape
    m_p = n_prompt_streams * prompt_seq
    n_s_streams = (m - m_p) // seq_s
    tm = _tile(math.gcd(prompt_seq, m - m_p), 1024)
    tiles_per_stream = prompt_seq // tm
    n_p = m_p // tm
    spt = tm // seq_s
    dff = wa.shape[2]
    s_blk = lambda j, i: (jnp.maximum(i - n_p, 0), 0, j)
    w_blk = lambda j, i: (layer, 0, j)
    return pl.pallas_call(
        functools.partial(_ffn_in_kernel, n_p, tiles_per_stream, seq_s),
        grid=(dff // tn, m // tm),
        in_specs=[pl.BlockSpec((tm, d), lambda j, i: (i, 0)),
                  pl.BlockSpec((None, d, tn), w_blk),
                  pl.BlockSpec((None, d, tn), w_blk),
                  pl.BlockSpec((None, FFN_CONV, tn), w_blk),
                  pl.BlockSpec((None, 1, tn), w_blk),
                  pl.BlockSpec((None, spt, SUBLANE, tn), lambda j, i: (layer,) + s_blk(j, i))],
        out_specs=[pl.BlockSpec((tm, tn), lambda j, i: (i, j)),
                   pl.BlockSpec((1, FFN_CONV - 1, tn),
                                lambda j, i: (jnp.minimum(i, n_p - 1) // tiles_per_stream, 0, j)),
                   pl.BlockSpec((spt, FFN_CONV - 1, tn), s_blk)],
        out_shape=[jax.ShapeDtypeStruct((m, dff), BF16),
                   jax.ShapeDtypeStruct((n_prompt_streams, FFN_CONV - 1, dff), F32),
                   jax.ShapeDtypeStruct((n_s_streams, FFN_CONV - 1, dff), F32)],
        scratch_shapes=[pltpu.VMEM((SUBLANE + tm, tn), F32)],
        compiler_params=_compiler_params(("arbitrary", "arbitrary")),
        name="ffn_in",
    )(h, wa, wu, cw, cb, cache_in)


def _level_matrix():
    t = np.arange(L)[:, None]
    u = np.arange(L)[None, :]
    blocks = [(u <= t)]
    for l in range(N_LEVELS):
        n = 1 << l
        r = (t & ~(2 * n - 1)) + n - 1
        upper = (t & n) != 0
        blocks.append(np.where(upper, (u > r) & (u <= t), (u > t) & (u <= r)))
    blocks.append(u > t)
    return np.concatenate(blocks, axis=0).astype(np.float32)


def _pair_levels():
    r = np.arange(GROUP * L)
    x = r[:, None] ^ r[None, :]
    code = np.full(x.shape, -1, np.int32)
    code[x == 0] = 0
    for l in range(N_LEVELS):
        code[(x >> l) == 1] = l + 1
    code[r[None, :] > r[:, None]] = -1
    return code


def _stack(x):
    return jnp.concatenate([x[:, h * HEAD_DV:(h + 1) * HEAD_DV] for h in range(N_HEADS)], axis=0)


def _stack_cols(x, first):
    return jnp.concatenate([x[:, first + h:first + h + 1] for h in range(N_HEADS)], axis=0)


def _groups(x):
    gl = GROUP * L
    return [x[g * gl:(g + 1) * gl] for g in range(N_HEADS // GROUP)]


def _pair_product(a, b):
    return [_dot_nt(ag, bg) for ag, bg in zip(_groups(a), _groups(b))]


def _block_diag(x):
    z = jnp.zeros((L, HEAD_DV), x.dtype)
    rows = [jnp.concatenate([x[h * L:(h + 1) * L] if j == h else z for j in range(N_HEADS)], axis=1)
            for h in range(N_HEADS)]
    return jnp.concatenate(rows, axis=0)


def _store_heads(o_ref, off, y):
    for h in range(N_HEADS):
        o_ref[:, off + h * HEAD_DV:off + (h + 1) * HEAD_DV] = y[h * L:(h + 1) * L].astype(o_ref.dtype)


def _head_rms(o, gain, gate):
    y = o * lax.rsqrt(jnp.mean(o * o, axis=-1, keepdims=True) + EPS)
    return y * gain * _silu(gate)


def _gla_type(q, k, v, g, gate, gain, st_ref, mall, lvl, o_ref, o_off):
    g_hi, g_lo = _split2(g * LOG2_E)
    d2 = _dot(mall, jnp.concatenate([g_hi, g_lo], axis=1))
    w = g.shape[1]
    e_all = jnp.exp2(d2[:, :w] + d2[:, w:])

    def e_rows(r):
        return _stack(e_all[r * L:(r + 1) * L])

    qs, ks, vs = _stack(q), _stack(k), _stack(v)
    att = [jnp.where(lvl == 0, a, 0.0) for a in _pair_product(qs.astype(BF16), ks.astype(BF16))]
    for l in range(N_LEVELS):
        yield
        e_l = e_rows(l + 1)
        a_l = _pair_product((qs * e_l).astype(BF16), (ks * e_l).astype(BF16))
        att = [jnp.where(lvl == l + 1, a, prev) for a, prev in zip(a_l, att)]
    yield
    st = [st_ref[0, h] for h in range(N_HEADS)]
    vb = vs.astype(BF16)
    q_decayed = (qs * e_rows(0)).astype(BF16)
    q_state = jnp.concatenate(
        [_dot_nt(q_decayed[h * L:(h + 1) * L], st[h].astype(BF16)) for h in range(N_HEADS)], axis=0)
    o = jnp.concatenate([_dot(a.astype(BF16), vg) for a, vg in zip(att, _groups(vb))], axis=0) + q_state
    _store_heads(o_ref, o_off, _head_rms(o, gain, _stack(gate)))
    upd = _dot_tn(_block_diag(vb), (ks * e_rows(N_LEVELS + 1)).astype(BF16))
    for h in range(N_HEADS):
        decay_end = e_all[L - 1:L, h * HEAD_DV:(h + 1) * HEAD_DV]
        st_ref[0, h] = st[h] * decay_end + upd[h * HEAD_DV:(h + 1) * HEAD_DV]


CONV_BASE = SUBLANE - (GDN_CONV - 1)


def _lane(layer, proj_ref, cos_ref, sin_ref, mall_ref, lvl_ref, wgk_ref, bgk_ref, norms_ref, lb_ref,
          convw_ref, gdnp_ref, o_ref, xe_ref, sgla_ref, shg_ref, sgdn_ref, sret_ref):
    base = CONV_BASE
    lvl = lvl_ref[...]
    incl = lvl >= 0
    mall = mall_ref[...]
    tri = mall[0:L]
    hl = N_HEADS * L

    def col(off, width=512):
        return proj_ref[:, off:off + width]

    z = _dot(col(P_GLA_LR, LANE).astype(BF16), wgk_ref[...]) + bgk_ref[...]
    g_gla = (jnp.minimum(z, 0.0) - jnp.log(1.0 + jnp.exp(-jnp.abs(z)))) * (1.0 / GLA_GATE_NORM)
    gla = _gla_type(col(P_GLA_Q) * GLA_DK ** -0.5, col(P_GLA_K), col(P_GLA_V), g_gla, col(P_GLA_G),
                    norms_ref[0:1, :], sgla_ref, mall, lvl, o_ref, 0)
    if layer == 0:
        lb = jnp.zeros((1, 512), F32)
    else:
        logits = lb_ref[...]
        ex = jnp.exp(logits - jnp.max(logits, axis=0, keepdims=True))
        soft = ex / jnp.sum(ex, axis=0, keepdims=True)
        lb = jnp.sum(soft[1:layer + 1], axis=0, keepdims=True)
    f = lb + (1.0 - lb) * _sigmoid(col(P_HG_F))
    hgrn = _gla_type(_silu(col(P_HG_Q)), 1.0 - f, col(P_HG_I), jnp.log(f), col(P_HG_G),
                     norms_ref[1:2, :], shg_ref, mall, lvl, o_ref, BRANCH_WIDTH)

    xe_ref[SUBLANE:SUBLANE + L, :] = col(P_GDN_QKV, GDN_QKV)
    conv = convw_ref[0:1, :] * xe_ref[base:base + L, :]
    for j in range(1, GDN_CONV):
        conv = conv + convw_ref[j:j + 1, :] * xe_ref[base + j:base + j + L, :]
    xe_ref[base:SUBLANE, :] = xe_ref[SUBLANE + L - (GDN_CONV - 1):SUBLANE + L, :]
    act = _silu(conv)
    ba = col(P_GDN_BA, LANE)
    beta = _stack_cols(_sigmoid(ba), 0)
    log_a = -jnp.exp(gdnp_ref[0:1, :]) * _softplus(ba + gdnp_ref[1:2, :])
    la_hi, la_mid, la_lo = _split3(log_a)
    b_all = _dot(tri, la_hi) + (_dot(tri, la_mid) + _dot(tri, la_lo))
    b_col = _stack_cols(b_all, N_HEADS)
    b_rows = b_all.T
    b_row = [jnp.concatenate([b_rows[N_HEADS + GROUP * g + i:N_HEADS + GROUP * g + i + 1, :]
                              for i in range(GROUP)], axis=1) for g in range(N_HEADS // GROUP)]
    b_end = jnp.concatenate(
        [jnp.broadcast_to(b_all[L - 1:L, N_HEADS + h:N_HEADS + h + 1], (L, 1)) for h in range(N_HEADS)], axis=0)
    dq, dk, dv = _stack(act[:, 0:512]), _stack(act[:, 512:1024]), _stack(act[:, 1024:1536])
    qn = dq * lax.rsqrt(jnp.sum(dq * dq, axis=-1, keepdims=True) + EPS) * HEAD_DV ** -0.5
    kn = dk * lax.rsqrt(jnp.sum(dk * dk, axis=-1, keepdims=True) + EPS)
    qb, kb = qn.astype(BF16), kn.astype(BF16)
    dec = [jnp.exp(jnp.where(incl, bc - br, -1e30)) for bc, br in zip(_groups(b_col), b_row)]
    e_b = jnp.exp(b_col)
    st = [sgdn_ref[0, h] for h in range(N_HEADS)]
    kq_state = [_dot(jnp.concatenate([kb[h * L:(h + 1) * L], qb[h * L:(h + 1) * L]], axis=0),
                     st[h].astype(BF16)) for h in range(N_HEADS)]
    k_state = jnp.concatenate([x[0:L] for x in kq_state], axis=0)
    q_state = jnp.concatenate([x[L:2 * L] for x in kq_state], axis=0)
    rhs = beta * (dv - e_b * k_state)
    p_groups = [-(jnp.where(lvl > 0, kk * dec_g, 0.0) * beta_g)
                for kk, dec_g, beta_g in zip(_pair_product(kb, kb), dec, _groups(beta))]
    u_groups = _groups(rhs)
    next(gla)
    next(hgrn)
    for j in range(N_LEVELS):
        yield
        u_groups = [u + _dot_hi(p, u) for p, u in zip(p_groups, u_groups)]
        if j + 1 < N_LEVELS:
            p_groups = [_dot_hi(p, p) for p in p_groups]
        next(gla)
        next(hgrn)
    yield
    ub = jnp.concatenate(u_groups, axis=0).astype(BF16)
    att = [(qk * dec_g).astype(BF16) for qk, dec_g in zip(_pair_product(qb, kb), dec)]
    o = jnp.concatenate([_dot(a, ug) for a, ug in zip(att, _groups(ub))], axis=0) + e_b * q_state
    _store_heads(o_ref, 2 * BRANCH_WIDTH, _head_rms(o, norms_ref[2:3, :], _stack(col(P_GDN_Z))))
    upd = _dot_tn((kn * jnp.exp(b_end - b_col)).astype(BF16), _block_diag(ub))
    for h in range(N_HEADS):
        decay_end = jnp.exp(b_all[L - 1:L, N_HEADS + h:N_HEADS + h + 1])
        sgdn_ref[0, h] = decay_end * st[h] + upd[:, h * HEAD_DV:(h + 1) * HEAD_DV]

    for gen in (gla, hgrn):
        for _ in gen:
            pass

    cos = jnp.concatenate([cos_ref[...]] * N_HEADS, axis=0)
    sin = jnp.concatenate([sin_ref[...]] * N_HEADS, axis=0)
    rq, rk = _stack(col(P_RET_Q)), _stack(col(P_RET_K))
    vb = _stack(col(P_RET_V)).astype(BF16)
    row = lax.broadcasted_iota(jnp.int32, (hl, 1), 0)
    t_col = jnp.bitwise_and(row, L - 1).astype(F32)
    lg = jnp.zeros((hl, 1), F32)
    for h in range(N_HEADS):
        lg = jnp.where(jnp.right_shift(row, N_LEVELS) == h, math.log(1.0 - 2.0 ** (-5.0 - h)), lg)
    gl = GROUP * L
    t_minus_s = (lax.broadcasted_iota(jnp.int32, (gl, 1), 0)
                 - lax.broadcasted_iota(jnp.int32, (1, gl), 1)).astype(F32)
    dec = [jnp.exp(jnp.where(incl, t_minus_s * lg_g, -1e30)) for lg_g in _groups(lg)]
    qr = (rq * cos + pltpu.roll(rq, HEAD_DV // 2, axis=1) * sin).astype(BF16)
    kr = (rk * cos + pltpu.roll(rk, HEAD_DV // 2, axis=1) * sin) * HEAD_DV ** -0.5
    kb = kr.astype(BF16)
    st = [sret_ref[0, h] for h in range(N_HEADS)]
    q_state = jnp.concatenate(
        [_dot(qr[h * L:(h + 1) * L], st[h].astype(BF16)) for h in range(N_HEADS)], axis=0)
    att = [(qk * dec_g).astype(BF16) for qk, dec_g in zip(_pair_product(qr, kb), dec)]
    o = (jnp.concatenate([_dot(a, vg) for a, vg in zip(att, _groups(vb))], axis=0)
         + jnp.exp((t_col + 1.0) * lg) * q_state)
    mu = jnp.mean(o, axis=-1, keepdims=True)
    oc = o - mu
    y = oc * lax.rsqrt(jnp.mean(oc * oc, axis=-1, keepdims=True) + EPS)
    y = (y * norms_ref[3:4, :] + norms_ref[4:5, :]) * _silu(_stack(col(P_RET_G)))
    _store_heads(o_ref, 3 * BRANCH_WIDTH, y)
    k_end = (kr * jnp.exp((L - 1.0 - t_col) * lg)).astype(BF16)
    upd = _dot_tn(k_end, _block_diag(vb))
    for h in range(N_HEADS):
        gamma_l = math.exp(L * math.log(1.0 - 2.0 ** (-5.0 - h)))
        sret_ref[0, h] = gamma_l * st[h] + upd[:, h * HEAD_DV:(h + 1) * HEAD_DV]


N_LANES = 2


def _run_lanes(lanes):
    lanes = list(lanes)
    while lanes:
        for lane in list(lanes):
            if next(lane, StopIteration) is StopIteration:
                lanes.remove(lane)


def _write_states(k, carry, xe_ref, gla_out, hg_out, gdn_out, ret_out, cache_out):
    sgla_ref, shg_ref, sgdn_ref, sret_ref = carry
    for h in range(N_HEADS):
        gla_out[k, h] = sgla_ref[k, 0, h].T[0:GLA_DK]
        hg_out[k, h] = shg_ref[k, 0, h].T
    gdn_out[k] = sgdn_ref[k, 0]
    ret_out[k] = sret_ref[k, 0]
    cache_out[k] = xe_ref[k, CONV_BASE:SUBLANE, :]


def _mixer_prompt_kernel(layer, n_chunks, *refs):
    proj_refs, refs = refs[:N_LANES], refs[N_LANES:]
    consts, refs = refs[:10], refs[10:]
    o_ref, outs, (xe_ref, *carry) = refs[0], refs[1:6], refs[6:]
    c = pl.program_id(0)

    @pl.when(c == 0)
    def _():
        for ref in carry:
            ref[...] = jnp.zeros(ref.shape, F32)
        xe_ref[...] = jnp.zeros(xe_ref.shape, F32)

    _run_lanes(_lane(layer, proj_refs[k], *consts, o_ref.at[k], xe_ref.at[k],
                     *[s.at[k] for s in carry]) for k in range(N_LANES))

    @pl.when(c == n_chunks - 1)
    def _():
        for k in range(N_LANES):
            _write_states(k, carry, xe_ref, *outs)


def _mixer_sample_kernel(layer, proj_ref, sgla_in, shg_in, sgdn_in, sret_in, cgdn_in, *refs):
    consts, refs = refs[:10], refs[10:]
    o_ref, outs, (xe_ref, *carry) = refs[0], refs[1:6], refs[6:]
    sgla_ref, shg_ref, sgdn_ref, sret_ref = carry
    key_pad = jnp.zeros((HEAD_DV - GLA_DK, HEAD_DV), F32)
    for k in range(N_LANES):
        for h in range(N_HEADS):
            sgla_ref[k, 0, h] = jnp.concatenate([sgla_in[k, h], key_pad], axis=0).T
            shg_ref[k, 0, h] = shg_in[k, h].T
        sgdn_ref[k, 0] = sgdn_in[k]
        sret_ref[k, 0] = sret_in[k]
        xe_ref[k, CONV_BASE:SUBLANE, :] = cgdn_in[k]
    _run_lanes(_lane(layer, proj_ref.at[pl.ds(k * L, L)], *consts, o_ref.at[pl.ds(k * L, L)], xe_ref.at[k],
                     *[s.at[k] for s in carry]) for k in range(N_LANES))
    for k in range(N_LANES):
        _write_states(k, carry, xe_ref, *outs)


def _mixers(layer, proj, states, cache, cos_t, sin_t, mall, lvl, wgk, bgk, norms, lb_logits,
            convw, gdnp, n_prompt_streams, prompt_seq):
    t_total = proj.shape[0]
    chunks_per_stream = prompt_seq // L
    t_prompt = n_prompt_streams * prompt_seq
    n_sample = states[0].shape[1]
    past_block = PAST_LEN // L
    assert n_prompt_streams == N_LANES and n_sample % N_LANES == 0

    def layer_block(a):
        return pl.BlockSpec((None,) + a.shape[1:], lambda c: (layer,) + (0,) * (a.ndim - 1))

    const = lambda a: pl.BlockSpec(a.shape, lambda c: (0,) * a.ndim)
    per_stream = list(states) + [cache]
    lanes_out = lambda idx: [pl.BlockSpec((N_LANES,) + a.shape[2:],
                                          lambda c, a=a: (idx(c),) + (0,) * (a.ndim - 2))
                             for a in per_stream]
    out_shapes = lambda n: [jax.ShapeDtypeStruct((n,) + a.shape[2:], F32) for a in per_stream]
    carry = pltpu.VMEM((N_LANES, 1, N_HEADS, HEAD_DV, HEAD_DV), F32)
    scratch = [pltpu.VMEM((N_LANES, SUBLANE + L, GDN_QKV), F32), carry, carry, carry, carry]

    def consts(pos_block):
        specs = [pl.BlockSpec((L, HEAD_DV), lambda c: (pos_block(c), 0)),
                 pl.BlockSpec((L, HEAD_DV), lambda c: (pos_block(c), 0)),
                 const(mall), const(lvl), layer_block(wgk), layer_block(bgk),
                 layer_block(norms), const(lb_logits), layer_block(convw), layer_block(gdnp)]
        return specs, (cos_t, sin_t, mall, lvl, wgk, bgk, norms, lb_logits, convw, gdnp)

    c_specs, c_args = consts(lambda c: c)
    o_p, *st_p = pl.pallas_call(
        functools.partial(_mixer_prompt_kernel, layer, chunks_per_stream),
        grid=(chunks_per_stream,),
        in_specs=[*[pl.BlockSpec((L, N_PACK), lambda c, k=k: (k * chunks_per_stream + c, 0))
                    for k in range(N_LANES)], *c_specs],
        out_specs=[pl.BlockSpec((N_LANES, L, D_MODEL), lambda c: (0, c, 0)), *lanes_out(lambda c: 0)],
        out_shape=[jax.ShapeDtypeStruct((N_LANES, prompt_seq, D_MODEL), BF16), *out_shapes(N_LANES)],
        scratch_shapes=scratch,
        compiler_params=_compiler_params(("arbitrary",)),
        name="mixers_prompt",
    )(*[proj] * N_LANES, *c_args)

    c_specs, c_args = consts(lambda s: past_block)
    first_block = t_prompt // (N_LANES * L)
    o_s, *st_s = pl.pallas_call(
        functools.partial(_mixer_sample_kernel, layer),
        grid=(n_sample // N_LANES,),
        in_specs=[pl.BlockSpec((N_LANES * L, N_PACK), lambda s: (first_block + s, 0)),
                  *[pl.BlockSpec((None, N_LANES) + a.shape[2:], lambda s, a=a: (layer, s) + (0,) * (a.ndim - 2))
                    for a in per_stream],
                  *c_specs],
        out_specs=[pl.BlockSpec((N_LANES * L, D_MODEL), lambda s: (s, 0)), *lanes_out(lambda s: s)],
        out_shape=[jax.ShapeDtypeStruct((t_total - t_prompt, D_MODEL), BF16), *out_shapes(n_sample)],
        scratch_shapes=scratch,
        compiler_params=_compiler_params(("arbitrary",)),
        name="mixers_sample",
    )(proj, *per_stream, *c_args)
    return (o_p.reshape(t_prompt, D_MODEL), o_s), st_p, st_s


def _pad_heads(w, dk):
    lead = w.shape[:-1]
    w = w.reshape(lead + (N_HEADS, dk))
    w = jnp.pad(w, [(0, 0)] * len(lead) + [(0, 0), (0, HEAD_DV - dk)])
    return w.reshape(lead + (N_HEADS * HEAD_DV,))


def _pad_cols(w, width):
    return jnp.pad(w, [(0, 0)] * (w.ndim - 1) + [(0, width - w.shape[-1])])


def _mix_pieces():
    pieces = []
    for h in range(N_HEADS):
        pieces.append((P_GLA_Q + h * HEAD_DV, h * GLA_DK, GLA_DK, HEAD_DV))
        pieces.append((P_GLA_K + h * HEAD_DV, 256 + h * GLA_DK, GLA_DK, HEAD_DV))
    pieces += [(P_GLA_V, 512, 512, 512), (P_GLA_G, 1024, 512, 512), (P_GLA_LR, 1536, GLA_LOW_RANK, LANE),
               (P_HG_Q, 1552, 512, 512), (P_HG_F, 2064, 512, 512), (P_HG_I, 2576, 512, 512),
               (P_HG_G, 3088, 512, 512), (P_GDN_QKV, 3600, GDN_QKV, GDN_QKV), (P_GDN_Z, 5136, 512, 512),
               (P_GDN_BA, 5648, 2 * N_HEADS, LANE), (P_RET_Q, 5656, 512, 512), (P_RET_K, 6168, 512, 512),
               (P_RET_V, 6680, 512, 512), (P_RET_G, 7192, 512, 512)]
    return pieces


def _mix_units():
    units = [None] * (N_PACK // LANE)
    for dst, src, n, padded in _mix_pieces():
        for q in range(padded // LANE):
            units[dst // LANE + q] = (src + q * LANE, min(LANE, n - q * LANE))
    return units


def _merge_units(merge_tn):
    return [(N_MIX_IN + n * D_MODEL + j * merge_tn + q * LANE, LANE)
            for j in range(D_MODEL // merge_tn) for n in range(N_BRANCH) for q in range(merge_tn // LANE)]


def _pack_rows_kernel(per_step, src_ref, valid_ref, *refs):
    w_refs, o_ref = refs[:per_step], refs[per_step]
    s = pl.program_id(0)
    row = lax.broadcasted_iota(jnp.int32, (LANE, 1), 0)
    for q in range(per_step):
        x = jnp.where(row < valid_ref[s * per_step + q], w_refs[q][...], 0.0)
        o_ref[q * LANE:(q + 1) * LANE, :] = x.astype(o_ref.dtype)


def _pack_rows(w_t, layer, units, per_step, name):
    _, n_in, k = w_t.shape
    assert n_in % SUBLANE == 0 and all(u[0] % SUBLANE == 0 for u in units)
    w_t = w_t.reshape(-1, k)
    src = jnp.asarray([(layer * n_in + u[0]) // SUBLANE for u in units], jnp.int32)
    valid = jnp.asarray([u[1] for u in units], jnp.int32)
    grid_spec = pltpu.PrefetchScalarGridSpec(
        num_scalar_prefetch=2,
        grid=(len(units) // per_step,),
        in_specs=[pl.BlockSpec((pl.Element(LANE), pl.Element(k)),
                               lambda s, src, valid, q=q: (src[s * per_step + q] * SUBLANE, 0))
                  for q in range(per_step)],
        out_specs=pl.BlockSpec((per_step * LANE, k), lambda s, src, valid: (s, 0)))
    return pl.pallas_call(
        functools.partial(_pack_rows_kernel, per_step),
        grid_spec=grid_spec,
        out_shape=jax.ShapeDtypeStruct((len(units) * LANE, k), BF16),
        compiler_params=_compiler_params(("arbitrary",)),
        name=name,
    )(src, valid, *[w_t] * per_step)


def _cast_rows_kernel(n_valid, w_ref, o_ref):
    i = pl.program_id(1)

    @pl.when(i < n_valid)
    def _():
        o_ref[...] = w_ref[...].astype(o_ref.dtype)

    @pl.when(i >= n_valid)
    def _():
        o_ref[...] = jnp.zeros(o_ref.shape, o_ref.dtype)


def _cast_rows(w, rows_out, tk):
    depth, r, c = w.shape
    n_valid = r // tk
    assert r % tk == 0 and rows_out % tk == 0
    return pl.pallas_call(
        functools.partial(_cast_rows_kernel, n_valid),
        grid=(depth, rows_out // tk),
        in_specs=[pl.BlockSpec((None, tk, c), lambda l, i: (l, jnp.minimum(i, n_valid - 1), 0))],
        out_specs=pl.BlockSpec((None, tk, c), lambda l, i: (l, i, 0)),
        out_shape=jax.ShapeDtypeStruct((depth, rows_out, c), BF16),
        compiler_params=_compiler_params(("parallel", "arbitrary")),
        name="cast_rows",
    )(w)


def _split_ffn_in_kernel(w_ref, a_ref, u_ref):
    pad = jnp.zeros((a_ref.shape[0], D_FF_PAD - D_FF), BF16)
    a_ref[:, 0:D_FF] = w_ref[:, 0:D_FF].astype(BF16)
    a_ref[:, D_FF:D_FF_PAD] = pad
    u_ref[:, 0:D_FF] = w_ref[:, D_FF:2 * D_FF].astype(BF16)
    u_ref[:, D_FF:D_FF_PAD] = pad


def _split_ffn_in(w, tk):
    depth, d, _ = w.shape
    blk = lambda c: pl.BlockSpec((None, tk, c), lambda l, i: (l, i, 0))
    out = jax.ShapeDtypeStruct((depth, d, D_FF_PAD), BF16)
    return pl.pallas_call(
        _split_ffn_in_kernel,
        grid=(depth, d // tk),
        in_specs=[blk(2 * D_FF)],
        out_specs=[blk(D_FF_PAD), blk(D_FF_PAD)],
        out_shape=[out, out],
        compiler_params=_compiler_params(("parallel", "parallel")),
        name="split_ffn_in",
    )(w)


def _rope_tables(n_pos):
    pos = np.arange(n_pos, dtype=np.float64)
    inv = 1.0 / (ROPE_BASE ** (np.arange(0, HEAD_DV, 2, dtype=np.float64) / HEAD_DV))
    ang = pos[:, None] * inv[None, :]
    cos, sin = np.cos(ang), np.sin(ang)
    return (jnp.asarray(np.concatenate([cos, cos], axis=1), F32),
            jnp.asarray(np.concatenate([-sin, sin], axis=1), F32))


def kernel(x_prompt, x_sample, state_gla, state_hgrn, state_gdn, state_ret, cache_gdn_conv, cache_ffn_conv, norm_mix_g, w_in, gla_w_gk, gla_b_gk, gla_norm_g, hgrn_lb_logits, hgrn_norm_g, gdn_conv_w, gdn_a_log, gdn_dt_bias, gdn_norm_g, ret_norm_g, ret_norm_b, w_branch, w_out, norm_ffn_g, w_ffn_in, ffn_conv_w, ffn_conv_b, w_ffn_out, norm_final_g):
    nb, seq, d = x_prompt.shape
    ns, dseq, _ = x_sample.shape
    assert d == D_MODEL and dseq == L and seq % L == 0
    tp = nb * seq
    x = (x_prompt.reshape(tp, d), x_sample.reshape(ns * dseq, d))

    cos_t, sin_t = _rope_tables(max(seq, PAST_LEN + dseq))
    mall = jnp.asarray(_level_matrix(), BF16)
    lvl = jnp.asarray(_pair_levels())
    merge_tn = 512
    ffn_tn = 512

    wgk = jnp.pad(_pad_heads(gla_w_gk, GLA_DK), ((0, 0), (0, LANE - GLA_LOW_RANK), (0, 0))).astype(BF16)
    bgk = _pad_heads(gla_b_gk, GLA_DK)[:, None, :]
    norms = jnp.stack([gla_norm_g, hgrn_norm_g, gdn_norm_g, ret_norm_g, ret_norm_b]
                      + [jnp.zeros_like(gla_norm_g)] * 3, axis=1)
    head_lanes = ((0, 0), (N_HEADS, LANE - 2 * N_HEADS))
    gdnp = jnp.stack([jnp.pad(gdn_a_log, head_lanes), jnp.pad(gdn_dt_bias, head_lanes)], axis=1)
    w_branch_b = _cast_rows(w_branch.reshape(DEPTH, N_BRANCH * BRANCH_WIDTH, d), N_BRANCH * BRANCH_WIDTH,
                            256).reshape(w_branch.shape)
    w_out_b = _cast_rows(w_out, d, 256)
    wa, wu = _split_ffn_in(w_ffn_in, 128)
    w_dn = _cast_rows(w_ffn_out, D_FF_PAD, 128)
    cw = _pad_cols(ffn_conv_w, D_FF_PAD)
    cb = _pad_cols(ffn_conv_b, D_FF_PAD)[:, None, :]
    cache_in = jnp.pad(cache_ffn_conv,
                       ((0, 0), (0, 0), (SUBLANE - (FFN_CONV - 1), 0), (0, D_FF_PAD - D_FF)))
    states = (state_gla, state_hgrn, state_gdn, state_ret)
    w_in_t = jnp.swapaxes(w_in, 1, 2)

    h = _rmsnorm(x[0], x[1], norm_mix_g[0], BF16)
    new_p, new_s = [], []
    for l in range(DEPTH):
        w_mix = _pack_rows(w_in_t, l, _mix_units(), 3, "pack_w_mix")
        w_merge = _pack_rows(w_in_t, l, _merge_units(merge_tn), 4, "pack_w_merge")
        proj = _matmul(h, w_mix, 2816, F32)
        o, st_p, st_s = _mixers(l, proj, states, cache_gdn_conv, cos_t, sin_t, mall, lvl, wgk, bgk, norms,
                                hgrn_lb_logits, gdn_conv_w, gdnp, nb, seq)
        y = _merge(h, o, w_merge, w_branch_b, l, merge_tn)
        x, h2 = _resid_matmul_norm(y, w_out_b, l, x, norm_ffn_g[l], tp, 512, False, "out_proj")
        g, c_ffn_p, c_ffn_s = _ffn_in(h2, wa, wu, cw, cb, cache_in, l, nb, seq, dseq, ffn_tn)
        if l + 1 < DEPTH:
            x, h = _resid_matmul_norm(g, w_dn, l, x, norm_mix_g[l + 1], tp, 256, False, "ffn_out")
        else:
            y_p, y_s = _resid_matmul_norm(g, w_dn, l, x, norm_final_g, tp, 256, True, "ffn_out_final")
        new_p.append(st_p + [c_ffn_p[:, :, :D_FF]])
        new_s.append(st_s + [c_ffn_s[:, :, :D_FF]])

    y_prompt = y_p.reshape(nb, seq, d)
    y_sample = y_s.reshape(ns, dseq, d)
    p_out = tuple(jnp.stack([st[i] for st in new_p]) for i in range(6))
    s_out = tuple(jnp.stack([st[i] for st in new_s]) for i in range(6))
    return (y_prompt, y_sample) + p_out + s_out
```

```python
import functools
import math

import numpy as np
import jax
import jax.numpy as jnp
from jax import lax
from jax.experimental import pallas as pl
from jax.experimental.pallas import tpu as pltpu

F32 = jnp.float32
BF16 = jnp.bfloat16

D_MODEL = 2048
DEPTH = 2
PAST_LEN = 4096
L = 64
N_BRANCH = 4
BRANCH_WIDTH = 512
N_HEADS = 4
HEAD_DV = 128
GLA_DK = 64
GLA_LOW_RANK = 16
GLA_GATE_NORM = 16.0
GDN_CONV = 4
GDN_QKV = 1536
ROPE_BASE = 10000.0
D_FF = 5504
D_FF_PAD = 5632
FFN_CONV = 3
EPS = 1e-6
N_MIX_IN = 7704

LANE = 128
SUBLANE = 8
VMEM_LIMIT = 56 * 1024 * 1024

P_GLA_Q, P_GLA_K, P_GLA_V, P_GLA_G, P_GLA_LR = 0, 512, 1024, 1536, 2048
P_HG_Q, P_HG_F, P_HG_I, P_HG_G = 2176, 2688, 3200, 3712
P_GDN_QKV, P_GDN_Z, P_GDN_BA = 4224, 5760, 6272
P_RET_Q, P_RET_K, P_RET_V, P_RET_G = 6400, 6912, 7424, 7936
N_PACK = 8448
N_LEVELS = 6
GROUP = 2
LOG2_E = math.log2(math.e)


def _dot(a, b):
    return jnp.dot(a, b, preferred_element_type=F32)


def _dot_nt(a, b):
    return lax.dot_general(a, b, (((1,), (1,)), ((), ())), preferred_element_type=F32)


def _dot_tn(a, b):
    return lax.dot_general(a, b, (((0,), (0,)), ((), ())), preferred_element_type=F32)


def _split3(x):
    hi = x.astype(BF16)
    r = x - hi.astype(F32)
    mid = r.astype(BF16)
    lo = (r - mid.astype(F32)).astype(BF16)
    return hi, mid, lo


def _split2(x):
    hi = x.astype(BF16)
    lo = (x - hi.astype(F32)).astype(BF16)
    return hi, lo


def _dot_hi(a, b):
    a_hi, a_lo = _split2(a)
    b_hi, b_lo = _split2(b)
    return _dot(a_hi, b_hi) + (_dot(a_hi, b_lo) + _dot(a_lo, b_hi))


def _sigmoid(x):
    return 1.0 / (1.0 + jnp.exp(-x))


def _silu(x):
    return x * _sigmoid(x)


def _softplus(x):
    return jnp.maximum(x, 0.0) + jnp.log(1.0 + jnp.exp(-jnp.abs(x)))


def _compiler_params(semantics):
    return pltpu.CompilerParams(dimension_semantics=semantics, vmem_limit_bytes=VMEM_LIMIT)


def _tile(n, pref):
    t = min(n, pref)
    while n % t:
        t //= 2
    return t


def _split_specs(tm, d, n_p):
    return (pl.BlockSpec((tm, d), lambda i: (jnp.minimum(i, n_p - 1), 0)),
            pl.BlockSpec((tm, d), lambda i: (jnp.maximum(i - n_p, 0), 0)))


def _rmsnorm_kernel(n_p, xp_ref, xs_ref, g_ref, h_ref):
    x = jnp.where(pl.program_id(0) < n_p, xp_ref[...], xs_ref[...])
    y = x * lax.rsqrt(jnp.mean(x * x, axis=-1, keepdims=True) + EPS)
    h_ref[...] = (y * g_ref[...]).astype(h_ref.dtype)


def _rmsnorm(xp, xs, g, out_dtype):
    d = xp.shape[1]
    m = xp.shape[0] + xs.shape[0]
    tm = _tile(math.gcd(xp.shape[0], xs.shape[0]), 1024)
    n_p = xp.shape[0] // tm
    return pl.pallas_call(
        functools.partial(_rmsnorm_kernel, n_p),
        grid=(m // tm,),
        in_specs=[*_split_specs(tm, d, n_p), pl.BlockSpec((1, d), lambda i: (0, 0))],
        out_specs=pl.BlockSpec((tm, d), lambda i: (i, 0)),
        out_shape=jax.ShapeDtypeStruct((m, d), out_dtype),
        compiler_params=_compiler_params(("arbitrary",)),
        name="rmsnorm",
    )(xp, xs, g.reshape(1, d))


def _mm_kernel(a_ref, w_ref, o_ref):
    o_ref[...] = _dot_nt(a_ref[...], w_ref[...]).astype(o_ref.dtype)


def _matmul(a, w_t, tn, out_dtype):
    m, k = a.shape
    n = w_t.shape[0]
    w = w_t
    tm = _tile(m, 512)
    return pl.pallas_call(
        _mm_kernel,
        grid=(n // tn, m // tm),
        in_specs=[pl.BlockSpec((tm, k), lambda j, i: (i, 0)),
                  pl.BlockSpec((tn, k), lambda j, i: (j, 0))],
        out_specs=pl.BlockSpec((tm, tn), lambda j, i: (i, j)),
        out_shape=jax.ShapeDtypeStruct((m, n), out_dtype),
        compiler_params=_compiler_params(("parallel", "parallel")),
        name="in_proj",
    )(a, w)


def _resid_kernel(n_p, split_in, final, a_ref, w_ref, *refs):
    n_x = 2 if split_in else 1
    x_refs, g_ref, out_refs = refs[:n_x], refs[n_x], refs[n_x + 1:]
    is_prompt = pl.program_id(0) < n_p
    x_old = jnp.where(is_prompt, x_refs[0][...], x_refs[1][...]) if split_in else x_refs[0][...]
    x = x_old + _dot(a_ref[...], w_ref[...])
    h = x * lax.rsqrt(jnp.mean(x * x, axis=-1, keepdims=True) + EPS) * g_ref[...]
    if final:
        hp_ref, hs_ref = out_refs

        @pl.when(is_prompt)
        def _():
            hp_ref[...] = h

        @pl.when(jnp.logical_not(is_prompt))
        def _():
            hs_ref[...] = h
    else:
        out_refs[0][...] = x
        out_refs[1][...] = h.astype(out_refs[1].dtype)


def _resid_matmul_norm(a, w, layer, x, g, m_p, tm, final, name):
    m, k = a.shape
    d = w.shape[2]
    tm = _tile(math.gcd(m_p, m - m_p), tm)
    n_p = m_p // tm
    row = lambda i: (i, 0)
    split_in = isinstance(x, tuple)
    x_specs = list(_split_specs(tm, d, n_p)) if split_in else [pl.BlockSpec((tm, d), row)]
    x_args = list(x) if split_in else [x]
    if final:
        out_specs = list(_split_specs(tm, d, n_p))
        out_shape = [jax.ShapeDtypeStruct((m_p, d), F32), jax.ShapeDtypeStruct((m - m_p, d), F32)]
    else:
        out_specs = [pl.BlockSpec((tm, d), row), pl.BlockSpec((tm, d), row)]
        out_shape = [jax.ShapeDtypeStruct((m, d), F32), jax.ShapeDtypeStruct((m, d), BF16)]
    return pl.pallas_call(
        functools.partial(_resid_kernel, n_p, split_in, final),
        grid=(m // tm,),
        in_specs=[pl.BlockSpec((tm, k), row),
                  pl.BlockSpec((None, k, d), lambda i: (layer, 0, 0), pipeline_mode=pl.Buffered(1)),
                  *x_specs,
                  pl.BlockSpec((1, d), lambda i: (0, 0))],
        out_specs=out_specs,
        out_shape=out_shape,
        compiler_params=_compiler_params(("arbitrary",)),
        name=name,
    )(a, w, *x_args, g.reshape(1, d))


def _merge_kernel(n_p, h_ref, op_ref, os_ref, wm_ref, wb_ref, y_ref):
    tn = y_ref.shape[1]
    gates = _dot_nt(h_ref[...], wm_ref[...])
    o = jnp.where(pl.program_id(1) < n_p, op_ref[...], os_ref[...])
    acc = None
    for n in range(N_BRANCH):
        br = _dot(o[:, n * BRANCH_WIDTH:(n + 1) * BRANCH_WIDTH], wb_ref[n])
        term = _sigmoid(gates[:, n * tn:(n + 1) * tn]) * br
        acc = term if acc is None else acc + term
    y_ref[...] = acc.astype(y_ref.dtype)


def _merge(h, o, wm_packed, wb, layer, tn):
    m, d = h.shape
    o_p, o_s = o
    tm = _tile(math.gcd(o_p.shape[0], o_s.shape[0]), 1024)
    n_p = o_p.shape[0] // tm
    return pl.pallas_call(
        functools.partial(_merge_kernel, n_p),
        grid=(d // tn, m // tm),
        in_specs=[pl.BlockSpec((tm, d), lambda j, i: (i, 0)),
                  pl.BlockSpec((tm, d), lambda j, i: (jnp.minimum(i, n_p - 1), 0)),
                  pl.BlockSpec((tm, d), lambda j, i: (jnp.maximum(i - n_p, 0), 0)),
                  pl.BlockSpec((N_BRANCH * tn, d), lambda j, i: (j, 0)),
                  pl.BlockSpec((None, N_BRANCH, BRANCH_WIDTH, tn), lambda j, i: (layer, 0, 0, j))],
        out_specs=pl.BlockSpec((tm, tn), lambda j, i: (i, j)),
        out_shape=jax.ShapeDtypeStruct((m, d), BF16),
        compiler_params=_compiler_params(("arbitrary", "arbitrary")),
        name="merge",
    )(h, o_p, o_s, wm_packed, wb)


def _ffn_gate(a, prev1, prev2, u, cw_ref, cb_ref):
    conv = cw_ref[2:3, :] * a + cw_ref[1:2, :] * prev1 + cw_ref[0:1, :] * prev2 + cb_ref[...]
    return (_silu(conv) * u)


def _ffn_in_kernel(n_p, tiles_per_stream, seq_s, h_ref, wa_ref, wu_ref, cw_ref, cb_ref, cin_ref,
                   g_ref, cache_p_ref, cache_s_ref, buf_ref):
    tm = h_ref.shape[0]
    i = pl.program_id(1)

    def up_project():
        h = h_ref[...]
        return _dot(h, wa_ref[...]), _dot(h, wu_ref[...])

    @pl.when(jnp.logical_and(i < n_p, i % tiles_per_stream == 0))
    def _():
        buf_ref[0:SUBLANE, :] = jnp.zeros((SUBLANE, buf_ref.shape[1]), F32)

    @pl.when(i < n_p)
    def _():
        a, u = up_project()
        buf_ref[SUBLANE:SUBLANE + tm, :] = a
        prev1 = buf_ref[SUBLANE - 1:SUBLANE - 1 + tm, :]
        prev2 = buf_ref[SUBLANE - 2:SUBLANE - 2 + tm, :]
        g_ref[...] = _ffn_gate(a, prev1, prev2, u, cw_ref, cb_ref).astype(g_ref.dtype)
        last2 = a[tm - 2:tm, :]
        cache_p_ref[0] = last2
        buf_ref[SUBLANE - 2:SUBLANE, :] = last2

    @pl.when(i >= n_p)
    def _():
        a, u = up_project()
        for s in range(tm // seq_s):
            rows = slice(s * seq_s, (s + 1) * seq_s)
            a_s = a[rows]
            buf_ref[0:SUBLANE, :] = cin_ref[s]
            buf_ref[SUBLANE:SUBLANE + seq_s, :] = a_s
            prev1 = buf_ref[SUBLANE - 1:SUBLANE - 1 + seq_s, :]
            prev2 = buf_ref[SUBLANE - 2:SUBLANE - 2 + seq_s, :]
            g_ref[rows, :] = _ffn_gate(a_s, prev1, prev2, u[rows], cw_ref, cb_ref).astype(g_ref.dtype)
            cache_s_ref[s] = a_s[seq_s - 2:seq_s, :]


def _ffn_in(h, wa, wu, cw, cb, cache_in, layer, n_prompt_streams, prompt_seq, seq_s, tn):
    m, d = h.shape
    m_p = n_prompt_streams * prompt_seq
    n_s_streams = (m - m_p) // seq_s
    tm = _tile(math.gcd(prompt_seq, m - m_p), 1024)
    tiles_per_stream = prompt_seq // tm
    n_p = m_p // tm
    spt = tm // seq_s
    dff = wa.shape[2]
    s_blk = lambda j, i: (jnp.maximum(i - n_p, 0), 0, j)
    w_blk = lambda j, i: (layer, 0, j)
    return pl.pallas_call(
        functools.partial(_ffn_in_kernel, n_p, tiles_per_stream, seq_s),
        grid=(dff // tn, m // tm),
        in_specs=[pl.BlockSpec((tm, d), lambda j, i: (i, 0)),
                  pl.BlockSpec((None, d, tn), w_blk),
                  pl.BlockSpec((None, d, tn), w_blk),
                  pl.BlockSpec((None, FFN_CONV, tn), w_blk),
                  pl.BlockSpec((None, 1, tn), w_blk),
                  pl.BlockSpec((None, spt, SUBLANE, tn), lambda j, i: (layer,) + s_blk(j, i))],
        out_specs=[pl.BlockSpec((tm, tn), lambda j, i: (i, j)),
                   pl.BlockSpec((1, FFN_CONV - 1, tn),
                                lambda j, i: (jnp.minimum(i, n_p - 1) // tiles_per_stream, 0, j)),
                   pl.BlockSpec((spt, FFN_CONV - 1, tn), s_blk)],
        out_shape=[jax.ShapeDtypeStruct((m, D_FF), BF16),
                   jax.ShapeDtypeStruct((n_prompt_streams, FFN_CONV - 1, D_FF), F32),
                   jax.ShapeDtypeStruct((n_s_streams, FFN_CONV - 1, D_FF), F32)],
        scratch_shapes=[pltpu.VMEM((SUBLANE + tm, tn), F32)],
        compiler_params=_compiler_params(("arbitrary", "arbitrary")),
        name="ffn_in",
    )(h, wa, wu, cw, cb, cache_in)


def _level_matrix():
    t = np.arange(L)[:, None]
    u = np.arange(L)[None, :]
    blocks = [(u <= t)]
    for l in range(N_LEVELS):
        n = 1 << l
        r = (t & ~(2 * n - 1)) + n - 1
        upper = (t & n) != 0
        blocks.append(np.where(upper, (u > r) & (u <= t), (u > t) & (u <= r)))
    blocks.append(u > t)
    return np.concatenate(blocks, axis=0).astype(np.float32)


def _pair_levels():
    r = np.arange(GROUP * L)
    x = r[:, None] ^ r[None, :]
    code = np.full(x.shape, -1, np.int32)
    code[x == 0] = 0
    for l in range(N_LEVELS):
        code[(x >> l) == 1] = l + 1
    code[r[None, :] > r[:, None]] = -1
    return code


def _stack(x):
    return jnp.concatenate([x[:, h * HEAD_DV:(h + 1) * HEAD_DV] for h in range(N_HEADS)], axis=0)


def _stack_cols(x, first):
    return jnp.concatenate([x[:, first + h:first + h + 1] for h in range(N_HEADS)], axis=0)


def _groups(x):
    gl = GROUP * L
    return [x[g * gl:(g + 1) * gl] for g in range(N_HEADS // GROUP)]


def _pair_product(a, b):
    return [_dot_nt(ag, bg) for ag, bg in zip(_groups(a), _groups(b))]


def _block_diag(x):
    z = jnp.zeros((L, HEAD_DV), x.dtype)
    rows = [jnp.concatenate([x[h * L:(h + 1) * L] if j == h else z for j in range(N_HEADS)], axis=1)
            for h in range(N_HEADS)]
    return jnp.concatenate(rows, axis=0)


def _store_heads(o_ref, off, y):
    for h in range(N_HEADS):
        o_ref[:, off + h * HEAD_DV:off + (h + 1) * HEAD_DV] = y[h * L:(h + 1) * L].astype(o_ref.dtype)


def _head_rms(o, gain, gate):
    y = o * lax.rsqrt(jnp.mean(o * o, axis=-1, keepdims=True) + EPS)
    return y * gain * _silu(gate)


def _gla_type(q, k, v, g, gate, gain, st_ref, mall, lvl, o_ref, o_off):
    e_all = jnp.exp2(_dot(mall, (g * LOG2_E).astype(BF16)))

    def e_rows(r):
        return _stack(e_all[r * L:(r + 1) * L])

    qs, ks, vs = _stack(q), _stack(k), _stack(v)
    att = [jnp.where(lvl == 0, a, 0.0) for a in _pair_product(qs.astype(BF16), ks.astype(BF16))]
    for l in range(N_LEVELS):
        yield
        e_l = e_rows(l + 1)
        a_l = _pair_product((qs * e_l).astype(BF16), (ks * e_l).astype(BF16))
        att = [jnp.where(lvl == l + 1, a, prev) for a, prev in zip(a_l, att)]
    yield
    st = [st_ref[0, h] for h in range(N_HEADS)]
    vb = vs.astype(BF16)
    q_decayed = (qs * e_rows(0)).astype(BF16)
    q_state = jnp.concatenate(
        [_dot_nt(q_decayed[h * L:(h + 1) * L], st[h].astype(BF16)) for h in range(N_HEADS)], axis=0)
    o = jnp.concatenate([_dot(a.astype(BF16), vg) for a, vg in zip(att, _groups(vb))], axis=0) + q_state
    _store_heads(o_ref, o_off, _head_rms(o, gain, _stack(gate)))
    upd = _dot_tn(_block_diag(vb), (ks * e_rows(N_LEVELS + 1)).astype(BF16))
    for h in range(N_HEADS):
        decay_end = e_all[L - 1:L, h * HEAD_DV:(h + 1) * HEAD_DV]
        st_ref[0, h] = st[h] * decay_end + upd[h * HEAD_DV:(h + 1) * HEAD_DV]


CONV_BASE = SUBLANE - (GDN_CONV - 1)


def _lane(layer, proj_ref, cos_ref, sin_ref, mall_ref, lvl_ref, wgk_ref, bgk_ref, norms_ref, lb_ref,
          convw_ref, gdnp_ref, o_ref, xe_ref, sgla_ref, shg_ref, sgdn_ref, sret_ref):
    base = CONV_BASE
    lvl = lvl_ref[...]
    incl = lvl >= 0
    mall = mall_ref[...]
    tri = mall[0:L]
    hl = N_HEADS * L

    def col(off, width=512):
        return proj_ref[:, off:off + width]

    z = _dot(col(P_GLA_LR, LANE).astype(BF16), wgk_ref[...]) + bgk_ref[...]
    g_gla = (jnp.minimum(z, 0.0) - jnp.log(1.0 + jnp.exp(-jnp.abs(z)))) * (1.0 / GLA_GATE_NORM)
    gla = _gla_type(col(P_GLA_Q) * GLA_DK ** -0.5, col(P_GLA_K), col(P_GLA_V), g_gla, col(P_GLA_G),
                    norms_ref[0:1, :], sgla_ref, mall, lvl, o_ref, 0)
    if layer == 0:
        lb = jnp.zeros((1, 512), F32)
    else:
        logits = lb_ref[...]
        ex = jnp.exp(logits - jnp.max(logits, axis=0, keepdims=True))
        soft = ex / jnp.sum(ex, axis=0, keepdims=True)
        lb = jnp.sum(soft[1:layer + 1], axis=0, keepdims=True)
    f = lb + (1.0 - lb) * _sigmoid(col(P_HG_F))
    hgrn = _gla_type(_silu(col(P_HG_Q)), 1.0 - f, col(P_HG_I), jnp.log(f), col(P_HG_G),
                     norms_ref[1:2, :], shg_ref, mall, lvl, o_ref, BRANCH_WIDTH)

    xe_ref[SUBLANE:SUBLANE + L, :] = col(P_GDN_QKV, GDN_QKV)
    conv = convw_ref[0:1, :] * xe_ref[base:base + L, :]
    for j in range(1, GDN_CONV):
        conv = conv + convw_ref[j:j + 1, :] * xe_ref[base + j:base + j + L, :]
    xe_ref[base:SUBLANE, :] = xe_ref[SUBLANE + L - (GDN_CONV - 1):SUBLANE + L, :]
    act = _silu(conv)
    ba = col(P_GDN_BA, LANE)
    beta = _stack_cols(_sigmoid(ba), 0)
    log_a = -jnp.exp(gdnp_ref[0:1, :]) * _softplus(ba + gdnp_ref[1:2, :])
    la_hi, la_mid, la_lo = _split3(log_a)
    b_all = _dot(tri, la_hi) + (_dot(tri, la_mid) + _dot(tri, la_lo))
    b_col = _stack_cols(b_all, N_HEADS)
    b_rows = b_all.T
    b_row = [jnp.concatenate([b_rows[N_HEADS + GROUP * g + i:N_HEADS + GROUP * g + i + 1, :]
                              for i in range(GROUP)], axis=1) for g in range(N_HEADS // GROUP)]
    b_end = jnp.concatenate(
        [jnp.broadcast_to(b_all[L - 1:L, N_HEADS + h:N_HEADS + h + 1], (L, 1)) for h in range(N_HEADS)], axis=0)
    dq, dk, dv = _stack(act[:, 0:512]), _stack(act[:, 512:1024]), _stack(act[:, 1024:1536])
    qn = dq * lax.rsqrt(jnp.sum(dq * dq, axis=-1, keepdims=True) + EPS) * HEAD_DV ** -0.5
    kn = dk * lax.rsqrt(jnp.sum(dk * dk, axis=-1, keepdims=True) + EPS)
    qb, kb = qn.astype(BF16), kn.astype(BF16)
    dec = [jnp.exp(jnp.where(incl, bc - br, -1e30)) for bc, br in zip(_groups(b_col), b_row)]
    e_b = jnp.exp(b_col)
    st = [sgdn_ref[0, h] for h in range(N_HEADS)]
    kq_state = [_dot(jnp.concatenate([kb[h * L:(h + 1) * L], qb[h * L:(h + 1) * L]], axis=0),
                     st[h].astype(BF16)) for h in range(N_HEADS)]
    k_state = jnp.concatenate([x[0:L] for x in kq_state], axis=0)
    q_state = jnp.concatenate([x[L:2 * L] for x in kq_state], axis=0)
    rhs = beta * (dv - e_b * k_state)
    p_groups = [-(jnp.where(lvl > 0, kk * dec_g, 0.0) * beta_g)
                for kk, dec_g, beta_g in zip(_pair_product(kb, kb), dec, _groups(beta))]
    u_groups = _groups(rhs)
    next(gla)
    next(hgrn)
    for j in range(N_LEVELS):
        yield
        u_groups = [u + _dot_hi(p, u) for p, u in zip(p_groups, u_groups)]
        if j + 1 < N_LEVELS:
            p_groups = [_dot_hi(p, p) for p in p_groups]
        next(gla)
        next(hgrn)
    yield
    ub = jnp.concatenate(u_groups, axis=0).astype(BF16)
    att = [(qk * dec_g).astype(BF16) for qk, dec_g in zip(_pair_product(qb, kb), dec)]
    o = jnp.concatenate([_dot(a, ug) for a, ug in zip(att, _groups(ub))], axis=0) + e_b * q_state
    _store_heads(o_ref, 2 * BRANCH_WIDTH, _head_rms(o, norms_ref[2:3, :], _stack(col(P_GDN_Z))))
    upd = _dot_tn((kn * jnp.exp(b_end - b_col)).astype(BF16), _block_diag(ub))
    for h in range(N_HEADS):
        decay_end = jnp.exp(b_all[L - 1:L, N_HEADS + h:N_HEADS + h + 1])
        sgdn_ref[0, h] = decay_end * st[h] + upd[:, h * HEAD_DV:(h + 1) * HEAD_DV]

    for gen in (gla, hgrn):
        for _ in gen:
            pass

    cos = jnp.concatenate([cos_ref[...]] * N_HEADS, axis=0)
    sin = jnp.concatenate([sin_ref[...]] * N_HEADS, axis=0)
    rq, rk = _stack(col(P_RET_Q)), _stack(col(P_RET_K))
    vb = _stack(col(P_RET_V)).astype(BF16)
    row = lax.broadcasted_iota(jnp.int32, (hl, 1), 0)
    t_col = jnp.bitwise_and(row, L - 1).astype(F32)
    lg = jnp.zeros((hl, 1), F32)
    for h in range(N_HEADS):
        lg = jnp.where(jnp.right_shift(row, N_LEVELS) == h, math.log(1.0 - 2.0 ** (-5.0 - h)), lg)
    gl = GROUP * L
    t_minus_s = (lax.broadcasted_iota(jnp.int32, (gl, 1), 0)
                 - lax.broadcasted_iota(jnp.int32, (1, gl), 1)).astype(F32)
    dec = [jnp.exp(jnp.where(incl, t_minus_s * lg_g, -1e30)) for lg_g in _groups(lg)]
    qr = (rq * cos + pltpu.roll(rq, HEAD_DV // 2, axis=1) * sin).astype(BF16)
    kr = (rk * cos + pltpu.roll(rk, HEAD_DV // 2, axis=1) * sin) * HEAD_DV ** -0.5
    kb = kr.astype(BF16)
    st = [sret_ref[0, h] for h in range(N_HEADS)]
    q_state = jnp.concatenate(
        [_dot(qr[h * L:(h + 1) * L], st[h].astype(BF16)) for h in range(N_HEADS)], axis=0)
    att = [(qk * dec_g).astype(BF16) for qk, dec_g in zip(_pair_product(qr, kb), dec)]
    o = (jnp.concatenate([_dot(a, vg) for a, vg in zip(att, _groups(vb))], axis=0)
         + jnp.exp((t_col + 1.0) * lg) * q_state)
    mu = jnp.mean(o, axis=-1, keepdims=True)
    oc = o - mu
    y = oc * lax.rsqrt(jnp.mean(oc * oc, axis=-1, keepdims=True) + EPS)
    y = (y * norms_ref[3:4, :] + norms_ref[4:5, :]) * _silu(_stack(col(P_RET_G)))
    _store_heads(o_ref, 3 * BRANCH_WIDTH, y)
    k_end = (kr * jnp.exp((L - 1.0 - t_col) * lg)).astype(BF16)
    upd = _dot_tn(k_end, _block_diag(vb))
    for h in range(N_HEADS):
        gamma_l = math.exp(L * math.log(1.0 - 2.0 ** (-5.0 - h)))
        sret_ref[0, h] = gamma_l * st[h] + upd[:, h * HEAD_DV:(h + 1) * HEAD_DV]


N_LANES = 2


def _run_lanes(lanes):
    lanes = list(lanes)
    while lanes:
        for lane in list(lanes):
            if next(lane, StopIteration) is StopIteration:
                lanes.remove(lane)


def _write_states(k, carry, xe_ref, gla_out, hg_out, gdn_out, ret_out, cache_out):
    sgla_ref, shg_ref, sgdn_ref, sret_ref = carry
    for h in range(N_HEADS):
        gla_out[k, h] = sgla_ref[k, 0, h].T[0:GLA_DK]
        hg_out[k, h] = shg_ref[k, 0, h].T
    gdn_out[k] = sgdn_ref[k, 0]
    ret_out[k] = sret_ref[k, 0]
    cache_out[k] = xe_ref[k, CONV_BASE:SUBLANE, :]


def _mixer_prompt_kernel(layer, n_chunks, *refs):
    proj_refs, refs = refs[:N_LANES], refs[N_LANES:]
    consts, refs = refs[:10], refs[10:]
    o_ref, outs, (xe_ref, *carry) = refs[0], refs[1:6], refs[6:]
    c = pl.program_id(0)

    @pl.when(c == 0)
    def _():
        for ref in carry:
            ref[...] = jnp.zeros(ref.shape, F32)
        xe_ref[...] = jnp.zeros(xe_ref.shape, F32)

    _run_lanes(_lane(layer, proj_refs[k], *consts, o_ref.at[k], xe_ref.at[k],
                     *[s.at[k] for s in carry]) for k in range(N_LANES))

    @pl.when(c == n_chunks - 1)
    def _():
        for k in range(N_LANES):
            _write_states(k, carry, xe_ref, *outs)


def _mixer_sample_kernel(layer, proj_ref, sgla_in, shg_in, sgdn_in, sret_in, cgdn_in, *refs):
    consts, refs = refs[:10], refs[10:]
    o_ref, outs, (xe_ref, *carry) = refs[0], refs[1:6], refs[6:]
    sgla_ref, shg_ref, sgdn_ref, sret_ref = carry
    key_pad = jnp.zeros((HEAD_DV - GLA_DK, HEAD_DV), F32)
    for k in range(N_LANES):
        for h in range(N_HEADS):
            sgla_ref[k, 0, h] = jnp.concatenate([sgla_in[k, h], key_pad], axis=0).T
            shg_ref[k, 0, h] = shg_in[k, h].T
        sgdn_ref[k, 0] = sgdn_in[k]
        sret_ref[k, 0] = sret_in[k]
        xe_ref[k, CONV_BASE:SUBLANE, :] = cgdn_in[k]
    _run_lanes(_lane(layer, proj_ref.at[pl.ds(k * L, L)], *consts, o_ref.at[pl.ds(k * L, L)], xe_ref.at[k],
                     *[s.at[k] for s in carry]) for k in range(N_LANES))
    for k in range(N_LANES):
        _write_states(k, carry, xe_ref, *outs)


def _mixers(layer, proj, states, cache, cos_t, sin_t, mall, lvl, wgk, bgk, norms, lb_logits,
            convw, gdnp, n_prompt_streams, prompt_seq):
    t_total = proj.shape[0]
    chunks_per_stream = prompt_seq // L
    t_prompt = n_prompt_streams * prompt_seq
    n_sample = states[0].shape[1]
    past_block = PAST_LEN // L
    assert n_prompt_streams == N_LANES and n_sample % N_LANES == 0

    def layer_block(a):
        return pl.BlockSpec((None,) + a.shape[1:], lambda c: (layer,) + (0,) * (a.ndim - 1))

    const = lambda a: pl.BlockSpec(a.shape, lambda c: (0,) * a.ndim)
    per_stream = list(states) + [cache]
    lanes_out = lambda idx: [pl.BlockSpec((N_LANES,) + a.shape[2:],
                                          lambda c, a=a: (idx(c),) + (0,) * (a.ndim - 2))
                             for a in per_stream]
    out_shapes = lambda n: [jax.ShapeDtypeStruct((n,) + a.shape[2:], F32) for a in per_stream]
    carry = pltpu.VMEM((N_LANES, 1, N_HEADS, HEAD_DV, HEAD_DV), F32)
    scratch = [pltpu.VMEM((N_LANES, SUBLANE + L, GDN_QKV), F32), carry, carry, carry, carry]

    def consts(pos_block):
        specs = [pl.BlockSpec((L, HEAD_DV), lambda c: (pos_block(c), 0)),
                 pl.BlockSpec((L, HEAD_DV), lambda c: (pos_block(c), 0)),
                 const(mall), const(lvl), layer_block(wgk), layer_block(bgk),
                 layer_block(norms), const(lb_logits), layer_block(convw), layer_block(gdnp)]
        return specs, (cos_t, sin_t, mall, lvl, wgk, bgk, norms, lb_logits, convw, gdnp)

    c_specs, c_args = consts(lambda c: c)
    o_p, *st_p = pl.pallas_call(
        functools.partial(_mixer_prompt_kernel, layer, chunks_per_stream),
        grid=(chunks_per_stream,),
        in_specs=[*[pl.BlockSpec((L, N_PACK), lambda c, k=k: (k * chunks_per_stream + c, 0))
                    for k in range(N_LANES)], *c_specs],
        out_specs=[pl.BlockSpec((N_LANES, L, D_MODEL), lambda c: (0, c, 0)), *lanes_out(lambda c: 0)],
        out_shape=[jax.ShapeDtypeStruct((N_LANES, prompt_seq, D_MODEL), BF16), *out_shapes(N_LANES)],
        scratch_shapes=scratch,
        compiler_params=_compiler_params(("arbitrary",)),
        name="mixers_prompt",
    )(*[proj] * N_LANES, *c_args)

    c_specs, c_args = consts(lambda s: past_block)
    first_block = t_prompt // (N_LANES * L)
    o_s, *st_s = pl.pallas_call(
        functools.partial(_mixer_sample_kernel, layer),
        grid=(n_sample // N_LANES,),
        in_specs=[pl.BlockSpec((N_LANES * L, N_PACK), lambda s: (first_block + s, 0)),
                  *[pl.BlockSpec((None, N_LANES) + a.shape[2:], lambda s, a=a: (layer, s) + (0,) * (a.ndim - 2))
                    for a in per_stream],
                  *c_specs],
        out_specs=[pl.BlockSpec((N_LANES * L, D_MODEL), lambda s: (s, 0)), *lanes_out(lambda s: s)],
        out_shape=[jax.ShapeDtypeStruct((t_total - t_prompt, D_MODEL), BF16), *out_shapes(n_sample)],
        scratch_shapes=scratch,
        compiler_params=_compiler_params(("arbitrary",)),
        name="mixers_sample",
    )(proj, *per_stream, *c_args)
    return (o_p.reshape(t_prompt, D_MODEL), o_s), st_p, st_s


def _pad_heads(w, dk):
    lead = w.shape[:-1]
    w = w.reshape(lead + (N_HEADS, dk))
    w = jnp.pad(w, [(0, 0)] * len(lead) + [(0, 0), (0, HEAD_DV - dk)])
    return w.reshape(lead + (N_HEADS * HEAD_DV,))


def _pad_cols(w, width):
    return jnp.pad(w, [(0, 0)] * (w.ndim - 1) + [(0, width - w.shape[-1])])


def _mix_pieces():
    pieces = []
    for h in range(N_HEADS):
        pieces.append((P_GLA_Q + h * HEAD_DV, h * GLA_DK, GLA_DK, HEAD_DV))
        pieces.append((P_GLA_K + h * HEAD_DV, 256 + h * GLA_DK, GLA_DK, HEAD_DV))
    pieces += [(P_GLA_V, 512, 512, 512), (P_GLA_G, 1024, 512, 512), (P_GLA_LR, 1536, GLA_LOW_RANK, LANE),
               (P_HG_Q, 1552, 512, 512), (P_HG_F, 2064, 512, 512), (P_HG_I, 2576, 512, 512),
               (P_HG_G, 3088, 512, 512), (P_GDN_QKV, 3600, GDN_QKV, GDN_QKV), (P_GDN_Z, 5136, 512, 512),
               (P_GDN_BA, 5648, 2 * N_HEADS, LANE), (P_RET_Q, 5656, 512, 512), (P_RET_K, 6168, 512, 512),
               (P_RET_V, 6680, 512, 512), (P_RET_G, 7192, 512, 512)]
    return pieces


def _mix_units():
    units = [None] * (N_PACK // LANE)
    for dst, src, n, padded in _mix_pieces():
        for q in range(padded // LANE):
            units[dst // LANE + q] = (src + q * LANE, min(LANE, n - q * LANE))
    return units


def _merge_units(merge_tn):
    return [(N_MIX_IN + n * D_MODEL + j * merge_tn + q * LANE, LANE)
            for j in range(D_MODEL // merge_tn) for n in range(N_BRANCH) for q in range(merge_tn // LANE)]


def _pack_rows_kernel(per_step, src_ref, valid_ref, *refs):
    w_refs, o_ref = refs[:per_step], refs[per_step]
    s = pl.program_id(0)
    row = lax.broadcasted_iota(jnp.int32, (LANE, 1), 0)
    for q in range(per_step):
        x = jnp.where(row < valid_ref[s * per_step + q], w_refs[q][...], 0.0)
        o_ref[q * LANE:(q + 1) * LANE, :] = x.astype(o_ref.dtype)


def _pack_rows(w_t, layer, units, per_step, name):
    _, n_in, k = w_t.shape
    assert n_in % SUBLANE == 0 and all(u[0] % SUBLANE == 0 for u in units)
    w_t = w_t.reshape(-1, k)
    src = jnp.asarray([(layer * n_in + u[0]) // SUBLANE for u in units], jnp.int32)
    valid = jnp.asarray([u[1] for u in units], jnp.int32)
    grid_spec = pltpu.PrefetchScalarGridSpec(
        num_scalar_prefetch=2,
        grid=(len(units) // per_step,),
        in_specs=[pl.BlockSpec((pl.Element(LANE), pl.Element(k)),
                               lambda s, src, valid, q=q: (src[s * per_step + q] * SUBLANE, 0))
                  for q in range(per_step)],
        out_specs=pl.BlockSpec((per_step * LANE, k), lambda s, src, valid: (s, 0)))
    return pl.pallas_call(
        functools.partial(_pack_rows_kernel, per_step),
        grid_spec=grid_spec,
        out_shape=jax.ShapeDtypeStruct((len(units) * LANE, k), BF16),
        compiler_params=_compiler_params(("arbitrary",)),
        name=name,
    )(src, valid, *[w_t] * per_step)


def _cast_rows_kernel(n_valid, w_ref, o_ref):
    i = pl.program_id(1)

    @pl.when(i < n_valid)
    def _():
        o_ref[...] = w_ref[...].astype(o_ref.dtype)

    @pl.when(i >= n_valid)
    def _():
        o_ref[...] = jnp.zeros(o_ref.shape, o_ref.dtype)


def _cast_rows(w, rows_out, tk):
    depth, r, c = w.shape
    n_valid = r // tk
    assert r % tk == 0 and rows_out % tk == 0
    return pl.pallas_call(
        functools.partial(_cast_rows_kernel, n_valid),
        grid=(depth, rows_out // tk),
        in_specs=[pl.BlockSpec((None, tk, c), lambda l, i: (l, jnp.minimum(i, n_valid - 1), 0))],
        out_specs=pl.BlockSpec((None, tk, c), lambda l, i: (l, i, 0)),
        out_shape=jax.ShapeDtypeStruct((depth, rows_out, c), BF16),
        compiler_params=_compiler_params(("parallel", "arbitrary")),
        name="cast_rows",
    )(w)


def _split_ffn_in_kernel(w_ref, a_ref, u_ref):
    pad = jnp.zeros((a_ref.shape[0], D_FF_PAD - D_FF), BF16)
    a_ref[:, 0:D_FF] = w_ref[:, 0:D_FF].astype(BF16)
    a_ref[:, D_FF:D_FF_PAD] = pad
    u_ref[:, 0:D_FF] = w_ref[:, D_FF:2 * D_FF].astype(BF16)
    u_ref[:, D_FF:D_FF_PAD] = pad


def _split_ffn_in(w, tk):
    depth, d, _ = w.shape
    blk = lambda c: pl.BlockSpec((None, tk, c), lambda l, i: (l, i, 0))
    out = jax.ShapeDtypeStruct((depth, d, D_FF_PAD), BF16)
    return pl.pallas_call(
        _split_ffn_in_kernel,
        grid=(depth, d // tk),
        in_specs=[blk(2 * D_FF)],
        out_specs=[blk(D_FF_PAD), blk(D_FF_PAD)],
        out_shape=[out, out],
        compiler_params=_compiler_params(("parallel", "parallel")),
        name="split_ffn_in",
    )(w)


def _rope_tables(n_pos):
    pos = np.arange(n_pos, dtype=np.float64)
    inv = 1.0 / (ROPE_BASE ** (np.arange(0, HEAD_DV, 2, dtype=np.float64) / HEAD_DV))
    ang = pos[:, None] * inv[None, :]
    cos, sin = np.cos(ang), np.sin(ang)
    return (jnp.asarray(np.concatenate([cos, cos], axis=1), F32),
            jnp.asarray(np.concatenate([-sin, sin], axis=1), F32))


def kernel(x_prompt, x_sample, state_gla, state_hgrn, state_gdn, state_ret, cache_gdn_conv, cache_ffn_conv, norm_mix_g, w_in, gla_w_gk, gla_b_gk, gla_norm_g, hgrn_lb_logits, hgrn_norm_g, gdn_conv_w, gdn_a_log, gdn_dt_bias, gdn_norm_g, ret_norm_g, ret_norm_b, w_branch, w_out, norm_ffn_g, w_ffn_in, ffn_conv_w, ffn_conv_b, w_ffn_out, norm_final_g):
    nb, seq, d = x_prompt.shape
    ns, dseq, _ = x_sample.shape
    assert d == D_MODEL and dseq == L and seq % L == 0
    tp = nb * seq
    x = (x_prompt.reshape(tp, d), x_sample.reshape(ns * dseq, d))

    cos_t, sin_t = _rope_tables(max(seq, PAST_LEN + dseq))
    mall = jnp.asarray(_level_matrix(), BF16)
    lvl = jnp.asarray(_pair_levels())
    merge_tn = 512
    ffn_tn = 512

    wgk = jnp.pad(_pad_heads(gla_w_gk, GLA_DK), ((0, 0), (0, LANE - GLA_LOW_RANK), (0, 0))).astype(BF16)
    bgk = _pad_heads(gla_b_gk, GLA_DK)[:, None, :]
    norms = jnp.stack([gla_norm_g, hgrn_norm_g, gdn_norm_g, ret_norm_g, ret_norm_b]
                      + [jnp.zeros_like(gla_norm_g)] * 3, axis=1)
    head_lanes = ((0, 0), (N_HEADS, LANE - 2 * N_HEADS))
    gdnp = jnp.stack([jnp.pad(gdn_a_log, head_lanes), jnp.pad(gdn_dt_bias, head_lanes)], axis=1)
    w_branch_b = _cast_rows(w_branch.reshape(DEPTH, N_BRANCH * BRANCH_WIDTH, d), N_BRANCH * BRANCH_WIDTH,
                            256).reshape(w_branch.shape)
    w_out_b = _cast_rows(w_out, d, 256)
    wa, wu = _split_ffn_in(w_ffn_in, 128)
    w_dn = _cast_rows(w_ffn_out, D_FF, D_FF // 8)
    cw = _pad_cols(ffn_conv_w, D_FF_PAD)
    cb = _pad_cols(ffn_conv_b, D_FF_PAD)[:, None, :]
    cache_in = jnp.pad(cache_ffn_conv,
                       ((0, 0), (0, 0), (SUBLANE - (FFN_CONV - 1), 0), (0, D_FF_PAD - D_FF)))
    states = (state_gla, state_hgrn, state_gdn, state_ret)
    w_in_t = jnp.swapaxes(w_in, 1, 2)

    h = _rmsnorm(x[0], x[1], norm_mix_g[0], BF16)
    new_p, new_s = [], []
    for l in range(DEPTH):
        w_mix = _pack_rows(w_in_t, l, _mix_units(), 3, "pack_w_mix")
        w_merge = _pack_rows(w_in_t, l, _merge_units(merge_tn), 4, "pack_w_merge")
        proj = _matmul(h, w_mix, 2816, F32)
        o, st_p, st_s = _mixers(l, proj, states, cache_gdn_conv, cos_t, sin_t, mall, lvl, wgk, bgk, norms,
                                hgrn_lb_logits, gdn_conv_w, gdnp, nb, seq)
        y = _merge(h, o, w_merge, w_branch_b, l, merge_tn)
        x, h2 = _resid_matmul_norm(y, w_out_b, l, x, norm_ffn_g[l], tp, 512, False, "out_proj")
        g, c_ffn_p, c_ffn_s = _ffn_in(h2, wa, wu, cw, cb, cache_in, l, nb, seq, dseq, ffn_tn)
        if l + 1 < DEPTH:
            x, h = _resid_matmul_norm(g, w_dn, l, x, norm_mix_g[l + 1], tp, 256, False, "ffn_out")
        else:
            y_p, y_s = _resid_matmul_norm(g, w_dn, l, x, norm_final_g, tp, 256, True, "ffn_out_final")
        new_p.append(st_p + [c_ffn_p])
        new_s.append(st_s + [c_ffn_s])

    y_prompt = y_p.reshape(nb, seq, d)
    y_sample = y_s.reshape(ns, dseq, d)
    p_out = tuple(jnp.stack([st[i] for st in new_p]) for i in range(6))
    s_out = tuple(jnp.stack([st[i] for st in new_s]) for i in range(6))
    return (y_prompt, y_sample) + p_out + s_out
```

```python
import functools
import math

import numpy as np
import jax
import jax.numpy as jnp
from jax import lax
from jax.experimental import pallas as pl
from jax.experimental.pallas import tpu as pltpu

F32 = jnp.float32
BF16 = jnp.bfloat16

D_MODEL = 2048
DEPTH = 2
PAST_LEN = 4096
L = 64
N_BRANCH = 4
BRANCH_WIDTH = 512
N_HEADS = 4
HEAD_DV = 128
GLA_DK = 64
GLA_LOW_RANK = 16
GLA_GATE_NORM = 16.0
GDN_CONV = 4
GDN_QKV = 1536
ROPE_BASE = 10000.0
D_FF = 5504
D_FF_PAD = 5632
FFN_CONV = 3
EPS = 1e-6
N_MIX_IN = 7704

LANE = 128
SUBLANE = 8
VMEM_LIMIT = 56 * 1024 * 1024

NORM_TM = 1024
IN_PROJ_TM, IN_PROJ_TN = 512, 2816
MERGE_TM, MERGE_TN = 1024, 512
OUT_PROJ_TM = 512
FFN_IN_TM, FFN_IN_TN = 1024, 512
FFN_OUT_TM = 256
CAST_TK = 256
PACK_MIX_GROUPS, PACK_MERGE_GROUPS = 3, 4

P_GLA_Q, P_GLA_K, P_GLA_V, P_GLA_G, P_GLA_LR = 0, 512, 1024, 1536, 2048
P_HG_Q, P_HG_F, P_HG_I, P_HG_G = 2176, 2688, 3200, 3712
P_GDN_QKV, P_GDN_Z, P_GDN_BA = 4224, 5760, 6272
P_RET_Q, P_RET_K, P_RET_V, P_RET_G = 6400, 6912, 7424, 7936
N_PACK = 8448
N_LEVELS = 6
GROUP = 2
LOG2_E = math.log2(math.e)


def _dot(a, b):
    return jnp.dot(a, b, preferred_element_type=F32)


def _dot_nt(a, b):
    return lax.dot_general(a, b, (((1,), (1,)), ((), ())), preferred_element_type=F32)


def _dot_tn(a, b):
    return lax.dot_general(a, b, (((0,), (0,)), ((), ())), preferred_element_type=F32)


def _split3(x):
    hi = x.astype(BF16)
    r = x - hi.astype(F32)
    mid = r.astype(BF16)
    lo = (r - mid.astype(F32)).astype(BF16)
    return hi, mid, lo


def _split2(x):
    hi = x.astype(BF16)
    lo = (x - hi.astype(F32)).astype(BF16)
    return hi, lo


def _dot_hi(a, b):
    a_hi, a_lo = _split2(a)
    b_hi, b_lo = _split2(b)
    return _dot(a_hi, b_hi) + (_dot(a_hi, b_lo) + _dot(a_lo, b_hi))


def _sigmoid(x):
    return 1.0 / (1.0 + jnp.exp(-x))


def _silu(x):
    return x * _sigmoid(x)


def _softplus(x):
    return jnp.maximum(x, 0.0) + jnp.log(1.0 + jnp.exp(-jnp.abs(x)))


def _compiler_params(semantics):
    return pltpu.CompilerParams(dimension_semantics=semantics, vmem_limit_bytes=VMEM_LIMIT)


def _tile(n, pref):
    t = min(n, pref)
    while n % t:
        t //= 2
    return t


def _split_specs(tm, d, n_p):
    return (pl.BlockSpec((tm, d), lambda i: (jnp.minimum(i, n_p - 1), 0)),
            pl.BlockSpec((tm, d), lambda i: (jnp.maximum(i - n_p, 0), 0)))


def _rmsnorm_kernel(n_p, xp_ref, xs_ref, g_ref, h_ref):
    x = jnp.where(pl.program_id(0) < n_p, xp_ref[...], xs_ref[...])
    y = x * lax.rsqrt(jnp.mean(x * x, axis=-1, keepdims=True) + EPS)
    h_ref[...] = (y * g_ref[...]).astype(h_ref.dtype)


def _rmsnorm(xp, xs, g, out_dtype):
    d = xp.shape[1]
    m = xp.shape[0] + xs.shape[0]
    tm = _tile(math.gcd(xp.shape[0], xs.shape[0]), NORM_TM)
    n_p = xp.shape[0] // tm
    return pl.pallas_call(
        functools.partial(_rmsnorm_kernel, n_p),
        grid=(m // tm,),
        in_specs=[*_split_specs(tm, d, n_p), pl.BlockSpec((1, d), lambda i: (0, 0))],
        out_specs=pl.BlockSpec((tm, d), lambda i: (i, 0)),
        out_shape=jax.ShapeDtypeStruct((m, d), out_dtype),
        compiler_params=_compiler_params(("arbitrary",)),
        name="rmsnorm",
    )(xp, xs, g.reshape(1, d))


def _mm_kernel(a_ref, w_ref, o_ref):
    o_ref[...] = _dot_nt(a_ref[...], w_ref[...]).astype(o_ref.dtype)


def _matmul(a, w_t, tn, out_dtype):
    m, k = a.shape
    n = w_t.shape[0]
    tm = _tile(m, IN_PROJ_TM)
    return pl.pallas_call(
        _mm_kernel,
        grid=(n // tn, m // tm),
        in_specs=[pl.BlockSpec((tm, k), lambda j, i: (i, 0)),
                  pl.BlockSpec((tn, k), lambda j, i: (j, 0))],
        out_specs=pl.BlockSpec((tm, tn), lambda j, i: (i, j)),
        out_shape=jax.ShapeDtypeStruct((m, n), out_dtype),
        compiler_params=_compiler_params(("parallel", "parallel")),
        name="in_proj",
    )(a, w_t)


def _resid_kernel(n_p, split_in, final, a_ref, w_ref, *refs):
    n_x = 2 if split_in else 1
    x_refs, g_ref, out_refs = refs[:n_x], refs[n_x], refs[n_x + 1:]
    is_prompt = pl.program_id(0) < n_p
    x_old = jnp.where(is_prompt, x_refs[0][...], x_refs[1][...]) if split_in else x_refs[0][...]
    x = x_old + _dot(a_ref[...], w_ref[...])
    h = x * lax.rsqrt(jnp.mean(x * x, axis=-1, keepdims=True) + EPS) * g_ref[...]
    if final:
        hp_ref, hs_ref = out_refs

        @pl.when(is_prompt)
        def _():
            hp_ref[...] = h

        @pl.when(jnp.logical_not(is_prompt))
        def _():
            hs_ref[...] = h
    else:
        out_refs[0][...] = x
        out_refs[1][...] = h.astype(out_refs[1].dtype)


def _resid_matmul_norm(a, w, layer, x, g, m_p, tm, final, name):
    m, k = a.shape
    d = w.shape[2]
    tm = _tile(math.gcd(m_p, m - m_p), tm)
    n_p = m_p // tm
    row = lambda i: (i, 0)
    split_in = isinstance(x, tuple)
    x_specs = list(_split_specs(tm, d, n_p)) if split_in else [pl.BlockSpec((tm, d), row)]
    x_args = list(x) if split_in else [x]
    if final:
        out_specs = list(_split_specs(tm, d, n_p))
        out_shape = [jax.ShapeDtypeStruct((m_p, d), F32), jax.ShapeDtypeStruct((m - m_p, d), F32)]
    else:
        out_specs = [pl.BlockSpec((tm, d), row), pl.BlockSpec((tm, d), row)]
        out_shape = [jax.ShapeDtypeStruct((m, d), F32), jax.ShapeDtypeStruct((m, d), BF16)]
    return pl.pallas_call(
        functools.partial(_resid_kernel, n_p, split_in, final),
        grid=(m // tm,),
        in_specs=[pl.BlockSpec((tm, k), row),
                  pl.BlockSpec((None, k, d), lambda i: (layer, 0, 0), pipeline_mode=pl.Buffered(1)),
                  *x_specs,
                  pl.BlockSpec((1, d), lambda i: (0, 0))],
        out_specs=out_specs,
        out_shape=out_shape,
        compiler_params=_compiler_params(("arbitrary",)),
        name=name,
    )(a, w, *x_args, g.reshape(1, d))


def _merge_kernel(n_p, h_ref, op_ref, os_ref, wm_ref, wb_ref, y_ref):
    tn = y_ref.shape[1]
    gates = _dot_nt(h_ref[...], wm_ref[...])
    o = jnp.where(pl.program_id(1) < n_p, op_ref[...], os_ref[...])
    acc = None
    for n in range(N_BRANCH):
        br = _dot(o[:, n * BRANCH_WIDTH:(n + 1) * BRANCH_WIDTH], wb_ref[n])
        term = _sigmoid(gates[:, n * tn:(n + 1) * tn]) * br
        acc = term if acc is None else acc + term
    y_ref[...] = acc.astype(y_ref.dtype)


def _merge(h, o, wm_packed, wb, layer, tn):
    m, d = h.shape
    o_p, o_s = o
    tm = _tile(math.gcd(o_p.shape[0], o_s.shape[0]), MERGE_TM)
    n_p = o_p.shape[0] // tm
    return pl.pallas_call(
        functools.partial(_merge_kernel, n_p),
        grid=(d // tn, m // tm),
        in_specs=[pl.BlockSpec((tm, d), lambda j, i: (i, 0)),
                  pl.BlockSpec((tm, d), lambda j, i: (jnp.minimum(i, n_p - 1), 0)),
                  pl.BlockSpec((tm, d), lambda j, i: (jnp.maximum(i - n_p, 0), 0)),
                  pl.BlockSpec((N_BRANCH * tn, d), lambda j, i: (j, 0)),
                  pl.BlockSpec((None, N_BRANCH, BRANCH_WIDTH, tn), lambda j, i: (layer, 0, 0, j))],
        out_specs=pl.BlockSpec((tm, tn), lambda j, i: (i, j)),
        out_shape=jax.ShapeDtypeStruct((m, d), BF16),
        compiler_params=_compiler_params(("arbitrary", "arbitrary")),
        name="merge",
    )(h, o_p, o_s, wm_packed, wb)


def _ffn_gate(a, prev1, prev2, u, cw_ref, cb_ref):
    conv = cw_ref[2:3, :] * a + cw_ref[1:2, :] * prev1 + cw_ref[0:1, :] * prev2 + cb_ref[...]
    return (_silu(conv) * u)


def _ffn_in_kernel(n_p, tiles_per_stream, seq_s, h_ref, wa_ref, wu_ref, cw_ref, cb_ref, cin_ref,
                   g_ref, cache_p_ref, cache_s_ref, buf_ref):
    tm = h_ref.shape[0]
    i = pl.program_id(1)

    def up_project():
        h = h_ref[...]
        return _dot(h, wa_ref[...]), _dot(h, wu_ref[...])

    @pl.when(jnp.logical_and(i < n_p, i % tiles_per_stream == 0))
    def _():
        buf_ref[0:SUBLANE, :] = jnp.zeros((SUBLANE, buf_ref.shape[1]), F32)

    @pl.when(i < n_p)
    def _():
        a, u = up_project()
        buf_ref[SUBLANE:SUBLANE + tm, :] = a
        prev1 = buf_ref[SUBLANE - 1:SUBLANE - 1 + tm, :]
        prev2 = buf_ref[SUBLANE - 2:SUBLANE - 2 + tm, :]
        g_ref[...] = _ffn_gate(a, prev1, prev2, u, cw_ref, cb_ref).astype(g_ref.dtype)
        last2 = a[tm - 2:tm, :]
        cache_p_ref[0] = last2
        buf_ref[SUBLANE - 2:SUBLANE, :] = last2

    @pl.when(i >= n_p)
    def _():
        a, u = up_project()
        for s in range(tm // seq_s):
            rows = slice(s * seq_s, (s + 1) * seq_s)
            a_s = a[rows]
            buf_ref[0:SUBLANE, :] = cin_ref[s]
            buf_ref[SUBLANE:SUBLANE + seq_s, :] = a_s
            prev1 = buf_ref[SUBLANE - 1:SUBLANE - 1 + seq_s, :]
            prev2 = buf_ref[SUBLANE - 2:SUBLANE - 2 + seq_s, :]
            g_ref[rows, :] = _ffn_gate(a_s, prev1, prev2, u[rows], cw_ref, cb_ref).astype(g_ref.dtype)
            cache_s_ref[s] = a_s[seq_s - 2:seq_s, :]


def _ffn_in(h, wa, wu, cw, cb, cache_in, layer, n_prompt_streams, prompt_seq, seq_s, tn):
    m, d = h.shape
    m_p = n_prompt_streams * prompt_seq
    n_s_streams = (m - m_p) // seq_s
    tm = _tile(math.gcd(prompt_seq, m - m_p), FFN_IN_TM)
    tiles_per_stream = prompt_seq // tm
    n_p = m_p // tm
    spt = tm // seq_s
    dff = wa.shape[2]
    s_blk = lambda j, i: (jnp.maximum(i - n_p, 0), 0, j)
    w_blk = lambda j, i: (layer, 0, j)
    return pl.pallas_call(
        functools.partial(_ffn_in_kernel, n_p, tiles_per_stream, seq_s),
        grid=(dff // tn, m // tm),
        in_specs=[pl.BlockSpec((tm, d), lambda j, i: (i, 0)),
                  pl.BlockSpec((None, d, tn), w_blk),
                  pl.BlockSpec((None, d, tn), w_blk),
                  pl.BlockSpec((None, FFN_CONV, tn), w_blk),
                  pl.BlockSpec((None, 1, tn), w_blk),
                  pl.BlockSpec((None, spt, SUBLANE, tn), lambda j, i: (layer,) + s_blk(j, i))],
        out_specs=[pl.BlockSpec((tm, tn), lambda j, i: (i, j)),
                   pl.BlockSpec((1, FFN_CONV - 1, tn),
                                lambda j, i: (jnp.minimum(i, n_p - 1) // tiles_per_stream, 0, j)),
                   pl.BlockSpec((spt, FFN_CONV - 1, tn), s_blk)],
        out_shape=[jax.ShapeDtypeStruct((m, D_FF), BF16),
                   jax.ShapeDtypeStruct((n_prompt_streams, FFN_CONV - 1, D_FF), F32),
                   jax.ShapeDtypeStruct((n_s_streams, FFN_CONV - 1, D_FF), F32)],
        scratch_shapes=[pltpu.VMEM((SUBLANE + tm, tn), F32)],
        compiler_params=_compiler_params(("arbitrary", "arbitrary")),
        name="ffn_in",
    )(h, wa, wu, cw, cb, cache_in)


def _level_matrix():
    t = np.arange(L)[:, None]
    u = np.arange(L)[None, :]
    blocks = [(u <= t)]
    for l in range(N_LEVELS):
        n = 1 << l
        r = (t & ~(2 * n - 1)) + n - 1
        upper = (t & n) != 0
        blocks.append(np.where(upper, (u > r) & (u <= t), (u > t) & (u <= r)))
    blocks.append(u > t)
    return np.concatenate(blocks, axis=0).astype(np.float32)


def _pair_levels():
    r = np.arange(GROUP * L)
    x = r[:, None] ^ r[None, :]
    code = np.full(x.shape, -1, np.int32)
    code[x == 0] = 0
    for l in range(N_LEVELS):
        code[(x >> l) == 1] = l + 1
    code[r[None, :] > r[:, None]] = -1
    return code


def _stack(x):
    return jnp.concatenate([x[:, h * HEAD_DV:(h + 1) * HEAD_DV] for h in range(N_HEADS)], axis=0)


def _stack_cols(x, first):
    return jnp.concatenate([x[:, first + h:first + h + 1] for h in range(N_HEADS)], axis=0)


def _groups(x):
    gl = GROUP * L
    return [x[g * gl:(g + 1) * gl] for g in range(N_HEADS // GROUP)]


def _pair_product(a, b):
    return [_dot_nt(ag, bg) for ag, bg in zip(_groups(a), _groups(b))]


def _block_diag(x):
    z = jnp.zeros((L, HEAD_DV), x.dtype)
    rows = [jnp.concatenate([x[h * L:(h + 1) * L] if j == h else z for j in range(N_HEADS)], axis=1)
            for h in range(N_HEADS)]
    return jnp.concatenate(rows, axis=0)


def _store_heads(o_ref, off, y):
    for h in range(N_HEADS):
        o_ref[:, off + h * HEAD_DV:off + (h + 1) * HEAD_DV] = y[h * L:(h + 1) * L].astype(o_ref.dtype)


def _head_rms(o, gain, gate):
    y = o * lax.rsqrt(jnp.mean(o * o, axis=-1, keepdims=True) + EPS)
    return y * gain * _silu(gate)


def _gla_type(q, k, v, g, gate, gain, st_ref, mall, lvl, o_ref, o_off):
    e_all = jnp.exp2(_dot(mall, (g * LOG2_E).astype(BF16)))

    def e_rows(r):
        return _stack(e_all[r * L:(r + 1) * L])

    qs, ks, vs = _stack(q), _stack(k), _stack(v)
    att = [jnp.where(lvl == 0, a, 0.0) for a in _pair_product(qs.astype(BF16), ks.astype(BF16))]
    for l in range(N_LEVELS):
        yield
        e_l = e_rows(l + 1)
        a_l = _pair_product((qs * e_l).astype(BF16), (ks * e_l).astype(BF16))
        att = [jnp.where(lvl == l + 1, a, prev) for a, prev in zip(a_l, att)]
    yield
    st = [st_ref[0, h] for h in range(N_HEADS)]
    vb = vs.astype(BF16)
    q_decayed = (qs * e_rows(0)).astype(BF16)
    q_state = jnp.concatenate(
        [_dot_nt(q_decayed[h * L:(h + 1) * L], st[h].astype(BF16)) for h in range(N_HEADS)], axis=0)
    o = jnp.concatenate([_dot(a.astype(BF16), vg) for a, vg in zip(att, _groups(vb))], axis=0) + q_state
    _store_heads(o_ref, o_off, _head_rms(o, gain, _stack(gate)))
    upd = _dot_tn(_block_diag(vb), (ks * e_rows(N_LEVELS + 1)).astype(BF16))
    for h in range(N_HEADS):
        decay_end = e_all[L - 1:L, h * HEAD_DV:(h + 1) * HEAD_DV]
        st_ref[0, h] = st[h] * decay_end + upd[h * HEAD_DV:(h + 1) * HEAD_DV]


CONV_BASE = SUBLANE - (GDN_CONV - 1)


def _lane(layer, proj_ref, cos_ref, sin_ref, mall_ref, lvl_ref, wgk_ref, bgk_ref, norms_ref, lb_ref,
          convw_ref, gdnp_ref, o_ref, xe_ref, sgla_ref, shg_ref, sgdn_ref, sret_ref):
    base = CONV_BASE
    lvl = lvl_ref[...]
    incl = lvl >= 0
    mall = mall_ref[...]
    tri = mall[0:L]

    def col(off, width=512):
        return proj_ref[:, off:off + width]

    z = _dot(col(P_GLA_LR, LANE).astype(BF16), wgk_ref[...]) + bgk_ref[...]
    g_gla = (jnp.minimum(z, 0.0) - jnp.log(1.0 + jnp.exp(-jnp.abs(z)))) * (1.0 / GLA_GATE_NORM)
    gla = _gla_type(col(P_GLA_Q) * GLA_DK ** -0.5, col(P_GLA_K), col(P_GLA_V), g_gla, col(P_GLA_G),
                    norms_ref[0:1, :], sgla_ref, mall, lvl, o_ref, 0)
    if layer == 0:
        lb = jnp.zeros((1, 512), F32)
    else:
        logits = lb_ref[...]
        ex = jnp.exp(logits - jnp.max(logits, axis=0, keepdims=True))
        soft = ex / jnp.sum(ex, axis=0, keepdims=True)
        lb = jnp.sum(soft[1:layer + 1], axis=0, keepdims=True)
    f = lb + (1.0 - lb) * _sigmoid(col(P_HG_F))
    hgrn = _gla_type(_silu(col(P_HG_Q)), 1.0 - f, col(P_HG_I), jnp.log(f), col(P_HG_G),
                     norms_ref[1:2, :], shg_ref, mall, lvl, o_ref, BRANCH_WIDTH)
    ret = _retention(col, cos_ref, sin_ref, norms_ref, sret_ref, incl, o_ref)

    xe_ref[SUBLANE:SUBLANE + L, :] = col(P_GDN_QKV, GDN_QKV)
    conv = convw_ref[0:1, :] * xe_ref[base:base + L, :]
    for j in range(1, GDN_CONV):
        conv = conv + convw_ref[j:j + 1, :] * xe_ref[base + j:base + j + L, :]
    xe_ref[base:SUBLANE, :] = xe_ref[SUBLANE + L - (GDN_CONV - 1):SUBLANE + L, :]
    act = _silu(conv)
    ba = col(P_GDN_BA, LANE)
    beta = _stack_cols(_sigmoid(ba), 0)
    log_a = -jnp.exp(gdnp_ref[0:1, :]) * _softplus(ba + gdnp_ref[1:2, :])
    la_hi, la_mid, la_lo = _split3(log_a)
    b_all = _dot(tri, la_hi) + (_dot(tri, la_mid) + _dot(tri, la_lo))
    b_col = _stack_cols(b_all, N_HEADS)
    b_rows = b_all.T
    b_row = [jnp.concatenate([b_rows[N_HEADS + GROUP * g + i:N_HEADS + GROUP * g + i + 1, :]
                              for i in range(GROUP)], axis=1) for g in range(N_HEADS // GROUP)]
    b_end = jnp.concatenate(
        [jnp.broadcast_to(b_all[L - 1:L, N_HEADS + h:N_HEADS + h + 1], (L, 1)) for h in range(N_HEADS)], axis=0)
    dq, dk, dv = _stack(act[:, 0:512]), _stack(act[:, 512:1024]), _stack(act[:, 1024:1536])
    qn = dq * lax.rsqrt(jnp.sum(dq * dq, axis=-1, keepdims=True) + EPS) * HEAD_DV ** -0.5
    kn = dk * lax.rsqrt(jnp.sum(dk * dk, axis=-1, keepdims=True) + EPS)
    qb, kb = qn.astype(BF16), kn.astype(BF16)
    dec = [jnp.exp(jnp.where(incl, bc - br, -1e30)) for bc, br in zip(_groups(b_col), b_row)]
    e_b = jnp.exp(b_col)
    st = [sgdn_ref[0, h] for h in range(N_HEADS)]
    kq_state = [_dot(jnp.concatenate([kb[h * L:(h + 1) * L], qb[h * L:(h + 1) * L]], axis=0),
                     st[h].astype(BF16)) for h in range(N_HEADS)]
    k_state = jnp.concatenate([x[0:L] for x in kq_state], axis=0)
    q_state = jnp.concatenate([x[L:2 * L] for x in kq_state], axis=0)
    rhs = beta * (dv - e_b * k_state)
    p_groups = [-(jnp.where(lvl > 0, kk * dec_g, 0.0) * beta_g)
                for kk, dec_g, beta_g in zip(_pair_product(kb, kb), dec, _groups(beta))]
    u_groups = _groups(rhs)
    next(gla)
    next(hgrn)
    for j in range(N_LEVELS):
        yield
        u_groups = [u + _dot_hi(p, u) for p, u in zip(p_groups, u_groups)]
        if j + 1 < N_LEVELS:
            p_groups = [_dot_hi(p, p) for p in p_groups]
        next(gla)
        next(hgrn)
        next(ret, None)
    yield
    ub = jnp.concatenate(u_groups, axis=0).astype(BF16)
    att = [(qk * dec_g).astype(BF16) for qk, dec_g in zip(_pair_product(qb, kb), dec)]
    o = jnp.concatenate([_dot(a, ug) for a, ug in zip(att, _groups(ub))], axis=0) + e_b * q_state
    _store_heads(o_ref, 2 * BRANCH_WIDTH, _head_rms(o, norms_ref[2:3, :], _stack(col(P_GDN_Z))))
    upd = _dot_tn((kn * jnp.exp(b_end - b_col)).astype(BF16), _block_diag(ub))
    for h in range(N_HEADS):
        decay_end = jnp.exp(b_all[L - 1:L, N_HEADS + h:N_HEADS + h + 1])
        sgdn_ref[0, h] = decay_end * st[h] + upd[:, h * HEAD_DV:(h + 1) * HEAD_DV]

    for gen in (gla, hgrn, ret):
        for _ in gen:
            pass


def _retention(col, cos_ref, sin_ref, norms_ref, sret_ref, incl, o_ref):
    hl = N_HEADS * L
    cos = jnp.concatenate([cos_ref[...]] * N_HEADS, axis=0)
    sin = jnp.concatenate([sin_ref[...]] * N_HEADS, axis=0)
    rq, rk = _stack(col(P_RET_Q)), _stack(col(P_RET_K))
    vb = _stack(col(P_RET_V)).astype(BF16)
    row = lax.broadcasted_iota(jnp.int32, (hl, 1), 0)
    t_col = jnp.bitwise_and(row, L - 1).astype(F32)
    lg = jnp.zeros((hl, 1), F32)
    for h in range(N_HEADS):
        lg = jnp.where(jnp.right_shift(row, N_LEVELS) == h, math.log(1.0 - 2.0 ** (-5.0 - h)), lg)
    gl = GROUP * L
    t_minus_s = (lax.broadcasted_iota(jnp.int32, (gl, 1), 0)
                 - lax.broadcasted_iota(jnp.int32, (1, gl), 1)).astype(F32)
    dec = [jnp.exp(jnp.where(incl, t_minus_s * lg_g, -1e30)) for lg_g in _groups(lg)]
    qr = (rq * cos + pltpu.roll(rq, HEAD_DV // 2, axis=1) * sin).astype(BF16)
    kr = (rk * cos + pltpu.roll(rk, HEAD_DV // 2, axis=1) * sin) * HEAD_DV ** -0.5
    kb = kr.astype(BF16)
    yield
    st = [sret_ref[0, h] for h in range(N_HEADS)]
    q_state = jnp.concatenate(
        [_dot(qr[h * L:(h + 1) * L], st[h].astype(BF16)) for h in range(N_HEADS)], axis=0)
    att = [(qk * dec_g).astype(BF16) for qk, dec_g in zip(_pair_product(qr, kb), dec)]
    yield
    o = (jnp.concatenate([_dot(a, vg) for a, vg in zip(att, _groups(vb))], axis=0)
         + jnp.exp((t_col + 1.0) * lg) * q_state)
    yield
    mu = jnp.mean(o, axis=-1, keepdims=True)
    oc = o - mu
    y = oc * lax.rsqrt(jnp.mean(oc * oc, axis=-1, keepdims=True) + EPS)
    y = (y * norms_ref[3:4, :] + norms_ref[4:5, :]) * _silu(_stack(col(P_RET_G)))
    _store_heads(o_ref, 3 * BRANCH_WIDTH, y)
    yield
    k_end = (kr * jnp.exp((L - 1.0 - t_col) * lg)).astype(BF16)
    upd = _dot_tn(k_end, _block_diag(vb))
    for h in range(N_HEADS):
        gamma_l = math.exp(L * math.log(1.0 - 2.0 ** (-5.0 - h)))
        sret_ref[0, h] = gamma_l * st[h] + upd[:, h * HEAD_DV:(h + 1) * HEAD_DV]


N_LANES = 2


def _run_lanes(lanes):
    lanes = list(lanes)
    while lanes:
        for lane in list(lanes):
            if next(lane, StopIteration) is StopIteration:
                lanes.remove(lane)


def _write_states(k, carry, xe_ref, gla_out, hg_out, gdn_out, ret_out, cache_out):
    sgla_ref, shg_ref, sgdn_ref, sret_ref = carry
    for h in range(N_HEADS):
        gla_out[k, h] = sgla_ref[k, 0, h].T[0:GLA_DK]
        hg_out[k, h] = shg_ref[k, 0, h].T
    gdn_out[k] = sgdn_ref[k, 0]
    ret_out[k] = sret_ref[k, 0]
    cache_out[k] = xe_ref[k, CONV_BASE:SUBLANE, :]


def _mixer_prompt_kernel(layer, n_chunks, *refs):
    proj_refs, refs = refs[:N_LANES], refs[N_LANES:]
    consts, refs = refs[:10], refs[10:]
    o_ref, outs, (xe_ref, *carry) = refs[0], refs[1:6], refs[6:]
    c = pl.program_id(0)

    @pl.when(c == 0)
    def _():
        for ref in carry:
            ref[...] = jnp.zeros(ref.shape, F32)
        xe_ref[...] = jnp.zeros(xe_ref.shape, F32)

    _run_lanes(_lane(layer, proj_refs[k], *consts, o_ref.at[k], xe_ref.at[k],
                     *[s.at[k] for s in carry]) for k in range(N_LANES))

    @pl.when(c == n_chunks - 1)
    def _():
        for k in range(N_LANES):
            _write_states(k, carry, xe_ref, *outs)


def _mixer_sample_kernel(layer, proj_ref, sgla_in, shg_in, sgdn_in, sret_in, cgdn_in, *refs):
    consts, refs = refs[:10], refs[10:]
    o_ref, outs, (xe_ref, *carry) = refs[0], refs[1:6], refs[6:]
    sgla_ref, shg_ref, sgdn_ref, sret_ref = carry
    key_pad = jnp.zeros((HEAD_DV - GLA_DK, HEAD_DV), F32)
    for k in range(N_LANES):
        for h in range(N_HEADS):
            sgla_ref[k, 0, h] = jnp.concatenate([sgla_in[k, h], key_pad], axis=0).T
            shg_ref[k, 0, h] = shg_in[k, h].T
        sgdn_ref[k, 0] = sgdn_in[k]
        sret_ref[k, 0] = sret_in[k]
        xe_ref[k, CONV_BASE:SUBLANE, :] = cgdn_in[k]
    _run_lanes(_lane(layer, proj_ref.at[pl.ds(k * L, L)], *consts, o_ref.at[pl.ds(k * L, L)], xe_ref.at[k],
                     *[s.at[k] for s in carry]) for k in range(N_LANES))
    for k in range(N_LANES):
        _write_states(k, carry, xe_ref, *outs)


def _mixers(layer, proj, states, cache, cos_t, sin_t, mall, lvl, wgk, bgk, norms, lb_logits,
            convw, gdnp, n_prompt_streams, prompt_seq):
    t_total = proj.shape[0]
    chunks_per_stream = prompt_seq // L
    t_prompt = n_prompt_streams * prompt_seq
    n_sample = states[0].shape[1]
    past_block = PAST_LEN // L
    assert n_prompt_streams == N_LANES and n_sample % N_LANES == 0

    def layer_block(a):
        return pl.BlockSpec((None,) + a.shape[1:], lambda c: (layer,) + (0,) * (a.ndim - 1))

    const = lambda a: pl.BlockSpec(a.shape, lambda c: (0,) * a.ndim)
    per_stream = list(states) + [cache]
    lanes_out = lambda idx: [pl.BlockSpec((N_LANES,) + a.shape[2:],
                                          lambda c, a=a: (idx(c),) + (0,) * (a.ndim - 2))
                             for a in per_stream]
    out_shapes = lambda n: [jax.ShapeDtypeStruct((n,) + a.shape[2:], F32) for a in per_stream]
    carry = pltpu.VMEM((N_LANES, 1, N_HEADS, HEAD_DV, HEAD_DV), F32)
    scratch = [pltpu.VMEM((N_LANES, SUBLANE + L, GDN_QKV), F32), carry, carry, carry, carry]

    def consts(pos_block):
        specs = [pl.BlockSpec((L, HEAD_DV), lambda c: (pos_block(c), 0)),
                 pl.BlockSpec((L, HEAD_DV), lambda c: (pos_block(c), 0)),
                 const(mall), const(lvl), layer_block(wgk), layer_block(bgk),
                 layer_block(norms), const(lb_logits), layer_block(convw), layer_block(gdnp)]
        return specs, (cos_t, sin_t, mall, lvl, wgk, bgk, norms, lb_logits, convw, gdnp)

    c_specs, c_args = consts(lambda c: c)
    o_p, *st_p = pl.pallas_call(
        functools.partial(_mixer_prompt_kernel, layer, chunks_per_stream),
        grid=(chunks_per_stream,),
        in_specs=[*[pl.BlockSpec((L, N_PACK), lambda c, k=k: (k * chunks_per_stream + c, 0))
                    for k in range(N_LANES)], *c_specs],
        out_specs=[pl.BlockSpec((N_LANES, L, D_MODEL), lambda c: (0, c, 0)), *lanes_out(lambda c: 0)],
        out_shape=[jax.ShapeDtypeStruct((N_LANES, prompt_seq, D_MODEL), BF16), *out_shapes(N_LANES)],
        scratch_shapes=scratch,
        compiler_params=_compiler_params(("arbitrary",)),
        name="mixers_prompt",
    )(*[proj] * N_LANES, *c_args)

    c_specs, c_args = consts(lambda s: past_block)
    first_block = t_prompt // (N_LANES * L)
    o_s, *st_s = pl.pallas_call(
        functools.partial(_mixer_sample_kernel, layer),
        grid=(n_sample // N_LANES,),
        in_specs=[pl.BlockSpec((N_LANES * L, N_PACK), lambda s: (first_block + s, 0)),
                  *[pl.BlockSpec((None, N_LANES) + a.shape[2:], lambda s, a=a: (layer, s) + (0,) * (a.ndim - 2))
                    for a in per_stream],
                  *c_specs],
        out_specs=[pl.BlockSpec((N_LANES * L, D_MODEL), lambda s: (s, 0)), *lanes_out(lambda s: s)],
        out_shape=[jax.ShapeDtypeStruct((t_total - t_prompt, D_MODEL), BF16), *out_shapes(n_sample)],
        scratch_shapes=scratch,
        compiler_params=_compiler_params(("arbitrary",)),
        name="mixers_sample",
    )(proj, *per_stream, *c_args)
    return (o_p.reshape(t_prompt, D_MODEL), o_s), st_p, st_s


def _pad_heads(w, dk):
    lead = w.shape[:-1]
    w = w.reshape(lead + (N_HEADS, dk))
    w = jnp.pad(w, [(0, 0)] * len(lead) + [(0, 0), (0, HEAD_DV - dk)])
    return w.reshape(lead + (N_HEADS * HEAD_DV,))


def _pad_cols(w, width):
    return jnp.pad(w, [(0, 0)] * (w.ndim - 1) + [(0, width - w.shape[-1])])


def _mix_pieces():
    pieces = []
    for h in range(N_HEADS):
        pieces.append((P_GLA_Q + h * HEAD_DV, h * GLA_DK, GLA_DK, HEAD_DV))
        pieces.append((P_GLA_K + h * HEAD_DV, 256 + h * GLA_DK, GLA_DK, HEAD_DV))
    pieces += [(P_GLA_V, 512, 512, 512), (P_GLA_G, 1024, 512, 512), (P_GLA_LR, 1536, GLA_LOW_RANK, LANE),
               (P_HG_Q, 1552, 512, 512), (P_HG_F, 2064, 512, 512), (P_HG_I, 2576, 512, 512),
               (P_HG_G, 3088, 512, 512), (P_GDN_QKV, 3600, GDN_QKV, GDN_QKV), (P_GDN_Z, 5136, 512, 512),
               (P_GDN_BA, 5648, 2 * N_HEADS, LANE), (P_RET_Q, 5656, 512, 512), (P_RET_K, 6168, 512, 512),
               (P_RET_V, 6680, 512, 512), (P_RET_G, 7192, 512, 512)]
    return pieces


def _mix_units():
    units = [None] * (N_PACK // LANE)
    for dst, src, n, padded in _mix_pieces():
        for q in range(padded // LANE):
            units[dst // LANE + q] = (src + q * LANE, min(LANE, n - q * LANE))
    return units


def _merge_units(merge_tn):
    return [(N_MIX_IN + n * D_MODEL + j * merge_tn + q * LANE, LANE)
            for j in range(D_MODEL // merge_tn) for n in range(N_BRANCH) for q in range(merge_tn // LANE)]


def _pack_rows_kernel(per_step, src_ref, valid_ref, *refs):
    w_refs, o_ref = refs[:per_step], refs[per_step]
    s = pl.program_id(0)
    row = lax.broadcasted_iota(jnp.int32, (LANE, 1), 0)
    for q in range(per_step):
        x = jnp.where(row < valid_ref[s * per_step + q], w_refs[q][...], 0.0)
        o_ref[q * LANE:(q + 1) * LANE, :] = x.astype(o_ref.dtype)


def _pack_rows(w_t, layer, units, per_step, name):
    _, n_in, k = w_t.shape
    assert n_in % SUBLANE == 0 and all(u[0] % SUBLANE == 0 for u in units)
    w_t = w_t.reshape(-1, k)
    src = jnp.asarray([(layer * n_in + u[0]) // SUBLANE for u in units], jnp.int32)
    valid = jnp.asarray([u[1] for u in units], jnp.int32)
    grid_spec = pltpu.PrefetchScalarGridSpec(
        num_scalar_prefetch=2,
        grid=(len(units) // per_step,),
        in_specs=[pl.BlockSpec((pl.Element(LANE), pl.Element(k)),
                               lambda s, src, valid, q=q: (src[s * per_step + q] * SUBLANE, 0))
                  for q in range(per_step)],
        out_specs=pl.BlockSpec((per_step * LANE, k), lambda s, src, valid: (s, 0)))
    return pl.pallas_call(
        functools.partial(_pack_rows_kernel, per_step),
        grid_spec=grid_spec,
        out_shape=jax.ShapeDtypeStruct((len(units) * LANE, k), BF16),
        compiler_params=_compiler_params(("arbitrary",)),
        name=name,
    )(src, valid, *[w_t] * per_step)


def _cast_rows_kernel(n_valid, w_ref, o_ref):
    i = pl.program_id(1)

    @pl.when(i < n_valid)
    def _():
        o_ref[...] = w_ref[...].astype(o_ref.dtype)

    @pl.when(i >= n_valid)
    def _():
        o_ref[...] = jnp.zeros(o_ref.shape, o_ref.dtype)


def _cast_rows(w, rows_out, tk):
    depth, r, c = w.shape
    n_valid = r // tk
    assert r % tk == 0 and rows_out % tk == 0
    return pl.pallas_call(
        functools.partial(_cast_rows_kernel, n_valid),
        grid=(depth, rows_out // tk),
        in_specs=[pl.BlockSpec((None, tk, c), lambda l, i: (l, jnp.minimum(i, n_valid - 1), 0))],
        out_specs=pl.BlockSpec((None, tk, c), lambda l, i: (l, i, 0)),
        out_shape=jax.ShapeDtypeStruct((depth, rows_out, c), BF16),
        compiler_params=_compiler_params(("parallel", "arbitrary")),
        name="cast_rows",
    )(w)


def _split_ffn_in_kernel(w_ref, a_ref, u_ref):
    pad = jnp.zeros((a_ref.shape[0], D_FF_PAD - D_FF), BF16)
    a_ref[:, 0:D_FF] = w_ref[:, 0:D_FF].astype(BF16)
    a_ref[:, D_FF:D_FF_PAD] = pad
    u_ref[:, 0:D_FF] = w_ref[:, D_FF:2 * D_FF].astype(BF16)
    u_ref[:, D_FF:D_FF_PAD] = pad


def _split_ffn_in(w, tk):
    depth, d, _ = w.shape
    blk = lambda c: pl.BlockSpec((None, tk, c), lambda l, i: (l, i, 0))
    out = jax.ShapeDtypeStruct((depth, d, D_FF_PAD), BF16)
    return pl.pallas_call(
        _split_ffn_in_kernel,
        grid=(depth, d // tk),
        in_specs=[blk(2 * D_FF)],
        out_specs=[blk(D_FF_PAD), blk(D_FF_PAD)],
        out_shape=[out, out],
        compiler_params=_compiler_params(("parallel", "parallel")),
        name="split_ffn_in",
    )(w)


def _rope_tables(n_pos):
    pos = np.arange(n_pos, dtype=np.float64)
    inv = 1.0 / (ROPE_BASE ** (np.arange(0, HEAD_DV, 2, dtype=np.float64) / HEAD_DV))
    ang = pos[:, None] * inv[None, :]
    cos, sin = np.cos(ang), np.sin(ang)
    return (jnp.asarray(np.concatenate([cos, cos], axis=1), F32),
            jnp.asarray(np.concatenate([-sin, sin], axis=1), F32))


def kernel(x_prompt, x_sample, state_gla, state_hgrn, state_gdn, state_ret, cache_gdn_conv, cache_ffn_conv, norm_mix_g, w_in, gla_w_gk, gla_b_gk, gla_norm_g, hgrn_lb_logits, hgrn_norm_g, gdn_conv_w, gdn_a_log, gdn_dt_bias, gdn_norm_g, ret_norm_g, ret_norm_b, w_branch, w_out, norm_ffn_g, w_ffn_in, ffn_conv_w, ffn_conv_b, w_ffn_out, norm_final_g):
    nb, seq, d = x_prompt.shape
    ns, dseq, _ = x_sample.shape
    assert d == D_MODEL and dseq == L and seq % L == 0
    tp = nb * seq
    x = (x_prompt.reshape(tp, d), x_sample.reshape(ns * dseq, d))

    cos_t, sin_t = _rope_tables(max(seq, PAST_LEN + dseq))
    mall = jnp.asarray(_level_matrix(), BF16)
    lvl = jnp.asarray(_pair_levels())
    wgk = jnp.pad(_pad_heads(gla_w_gk, GLA_DK), ((0, 0), (0, LANE - GLA_LOW_RANK), (0, 0))).astype(BF16)
    bgk = _pad_heads(gla_b_gk, GLA_DK)[:, None, :]
    norms = jnp.stack([gla_norm_g, hgrn_norm_g, gdn_norm_g, ret_norm_g, ret_norm_b]
                      + [jnp.zeros_like(gla_norm_g)] * 3, axis=1)
    head_lanes = ((0, 0), (N_HEADS, LANE - 2 * N_HEADS))
    gdnp = jnp.stack([jnp.pad(gdn_a_log, head_lanes), jnp.pad(gdn_dt_bias, head_lanes)], axis=1)
    w_branch_b = _cast_rows(w_branch.reshape(DEPTH, N_BRANCH * BRANCH_WIDTH, d), N_BRANCH * BRANCH_WIDTH,
                            CAST_TK).reshape(w_branch.shape)
    w_out_b = _cast_rows(w_out, d, CAST_TK)
    wa, wu = _split_ffn_in(w_ffn_in, LANE)
    w_dn = _cast_rows(w_ffn_out, D_FF, D_FF // SUBLANE)
    cw = _pad_cols(ffn_conv_w, D_FF_PAD)
    cb = _pad_cols(ffn_conv_b, D_FF_PAD)[:, None, :]
    cache_in = jnp.pad(cache_ffn_conv,
                       ((0, 0), (0, 0), (SUBLANE - (FFN_CONV - 1), 0), (0, D_FF_PAD - D_FF)))
    states = (state_gla, state_hgrn, state_gdn, state_ret)
    w_in_t = jnp.swapaxes(w_in, 1, 2)

    h = _rmsnorm(x[0], x[1], norm_mix_g[0], BF16)
    new_p, new_s = [], []
    for l in range(DEPTH):
        w_mix = _pack_rows(w_in_t, l, _mix_units(), PACK_MIX_GROUPS, "pack_w_mix")
        w_merge = _pack_rows(w_in_t, l, _merge_units(MERGE_TN), PACK_MERGE_GROUPS, "pack_w_merge")
        proj = _matmul(h, w_mix, IN_PROJ_TN, F32)
        o, st_p, st_s = _mixers(l, proj, states, cache_gdn_conv, cos_t, sin_t, mall, lvl, wgk, bgk, norms,
                                hgrn_lb_logits, gdn_conv_w, gdnp, nb, seq)
        y = _merge(h, o, w_merge, w_branch_b, l, MERGE_TN)
        x, h2 = _resid_matmul_norm(y, w_out_b, l, x, norm_ffn_g[l], tp, OUT_PROJ_TM, False, "out_proj")
        g, c_ffn_p, c_ffn_s = _ffn_in(h2, wa, wu, cw, cb, cache_in, l, nb, seq, dseq, FFN_IN_TN)
        if l + 1 < DEPTH:
            x, h = _resid_matmul_norm(g, w_dn, l, x, norm_mix_g[l + 1], tp, FFN_OUT_TM, False, "ffn_out")
        else:
            y_p, y_s = _resid_matmul_norm(g, w_dn, l, x, norm_final_g, tp, FFN_OUT_TM, True, "ffn_out_final")
        new_p.append(st_p + [c_ffn_p])
        new_s.append(st_s + [c_ffn_s])

    y_prompt = y_p.reshape(nb, seq, d)
    y_sample = y_s.reshape(ns, dseq, d)
    p_out = tuple(jnp.stack([st[i] for st in new_p]) for i in range(6))
    s_out = tuple(jnp.stack([st[i] for st in new_s]) for i in range(6))
    return (y_prompt, y_sample) + p_out + s_out
```

```python
import functools
import math

import numpy as np
import jax
import jax.numpy as jnp
from jax import lax
from jax.experimental import pallas as pl
from jax.experimental.pallas import tpu as pltpu

F32 = jnp.float32
BF16 = jnp.bfloat16

D_MODEL = 2048
DEPTH = 2
PAST_LEN = 4096
L = 64
N_BRANCH = 4
BRANCH_WIDTH = 512
N_HEADS = 4
HEAD_DV = 128
GLA_DK = 64
GLA_LOW_RANK = 16
GLA_GATE_NORM = 16.0
GDN_CONV = 4
GDN_QKV = 1536
ROPE_BASE = 10000.0
D_FF = 5504
D_FF_PAD = 5632
FFN_CONV = 3
EPS = 1e-6
N_MIX_IN = 7704

LANE = 128
SUBLANE = 8
VMEM_LIMIT = 56 * 1024 * 1024

NORM_TM = 1024
IN_PROJ_TM, IN_PROJ_TN = 512, 2816
MERGE_TM, MERGE_TN = 1024, 512
OUT_PROJ_TM = 512
FFN_IN_TM, FFN_IN_TN = 1024, 512
FFN_OUT_TM = 256
CAST_TK = 256
PACK_MIX_GROUPS, PACK_MERGE_GROUPS = 3, 4

P_GLA_Q, P_GLA_K, P_GLA_V, P_GLA_G, P_GLA_LR = 0, 512, 1024, 1536, 2048
P_HG_Q, P_HG_F, P_HG_I, P_HG_G = 2176, 2688, 3200, 3712
P_GDN_QKV, P_GDN_Z, P_GDN_BA = 4224, 5760, 6272
P_RET_Q, P_RET_K, P_RET_V, P_RET_G = 6400, 6912, 7424, 7936
N_PACK = 8448
N_LEVELS = 6
GROUP = 2
LOG2_E = math.log2(math.e)


def _dot(a, b):
    return jnp.dot(a, b, preferred_element_type=F32)


def _dot_nt(a, b):
    return lax.dot_general(a, b, (((1,), (1,)), ((), ())), preferred_element_type=F32)


def _dot_tn(a, b):
    return lax.dot_general(a, b, (((0,), (0,)), ((), ())), preferred_element_type=F32)


def _split3(x):
    hi = x.astype(BF16)
    r = x - hi.astype(F32)
    mid = r.astype(BF16)
    lo = (r - mid.astype(F32)).astype(BF16)
    return hi, mid, lo


def _split2(x):
    hi = x.astype(BF16)
    lo = (x - hi.astype(F32)).astype(BF16)
    return hi, lo


def _dot_hi(a, b):
    a_hi, a_lo = _split2(a)
    b_hi, b_lo = _split2(b)
    return _dot(a_hi, b_hi) + (_dot(a_hi, b_lo) + _dot(a_lo, b_hi))


def _sigmoid(x):
    return 1.0 / (1.0 + jnp.exp(-x))


def _silu(x):
    return x * _sigmoid(x)


def _softplus(x):
    return jnp.maximum(x, 0.0) + jnp.log(1.0 + jnp.exp(-jnp.abs(x)))


def _compiler_params(semantics):
    return pltpu.CompilerParams(dimension_semantics=semantics, vmem_limit_bytes=VMEM_LIMIT)


def _tile(n, pref):
    t = min(n, pref)
    while n % t:
        t //= 2
    return t


def _split_specs(tm, d, n_p):
    return (pl.BlockSpec((tm, d), lambda i: (jnp.minimum(i, n_p - 1), 0)),
            pl.BlockSpec((tm, d), lambda i: (jnp.maximum(i - n_p, 0), 0)))


def _rmsnorm_kernel(n_p, xp_ref, xs_ref, g_ref, h_ref):
    x = jnp.where(pl.program_id(0) < n_p, xp_ref[...], xs_ref[...])
    y = x * lax.rsqrt(jnp.mean(x * x, axis=-1, keepdims=True) + EPS)
    h_ref[...] = (y * g_ref[...]).astype(h_ref.dtype)


def _rmsnorm(xp, xs, g, out_dtype):
    d = xp.shape[1]
    m = xp.shape[0] + xs.shape[0]
    tm = _tile(math.gcd(xp.shape[0], xs.shape[0]), NORM_TM)
    n_p = xp.shape[0] // tm
    return pl.pallas_call(
        functools.partial(_rmsnorm_kernel, n_p),
        grid=(m // tm,),
        in_specs=[*_split_specs(tm, d, n_p), pl.BlockSpec((1, d), lambda i: (0, 0))],
        out_specs=pl.BlockSpec((tm, d), lambda i: (i, 0)),
        out_shape=jax.ShapeDtypeStruct((m, d), out_dtype),
        compiler_params=_compiler_params(("arbitrary",)),
        name="rmsnorm",
    )(xp, xs, g.reshape(1, d))


def _mm_kernel(a_ref, w_ref, o_ref):
    o_ref[...] = _dot_nt(a_ref[...], w_ref[...]).astype(o_ref.dtype)


def _matmul(a, w_t, tn, out_dtype):
    m, k = a.shape
    n = w_t.shape[0]
    tm = _tile(m, IN_PROJ_TM)
    return pl.pallas_call(
        _mm_kernel,
        grid=(n // tn, m // tm),
        in_specs=[pl.BlockSpec((tm, k), lambda j, i: (i, 0)),
                  pl.BlockSpec((tn, k), lambda j, i: (j, 0))],
        out_specs=pl.BlockSpec((tm, tn), lambda j, i: (i, j)),
        out_shape=jax.ShapeDtypeStruct((m, n), out_dtype),
        compiler_params=_compiler_params(("parallel", "parallel")),
        name="in_proj",
    )(a, w_t)


def _resid_kernel(n_p, split_in, final, a_ref, w_ref, *refs):
    n_x = 2 if split_in else 1
    x_refs, g_ref, out_refs = refs[:n_x], refs[n_x], refs[n_x + 1:]

    def run(x_ref, x_out_ref, h_out_ref):
        x = x_ref[...] + _dot(a_ref[...], w_ref[...])
        h = x * lax.rsqrt(jnp.mean(x * x, axis=-1, keepdims=True) + EPS) * g_ref[...]
        if x_out_ref is not None:
            x_out_ref[...] = x
        h_out_ref[...] = h.astype(h_out_ref.dtype)

    if not (split_in or final):
        run(x_refs[0], out_refs[0], out_refs[1])
        return
    outs_p = (None, out_refs[0]) if final else tuple(out_refs)
    outs_s = (None, out_refs[1]) if final else tuple(out_refs)
    is_prompt = pl.program_id(0) < n_p
    pl.when(is_prompt)(lambda: run(x_refs[0], *outs_p))
    pl.when(jnp.logical_not(is_prompt))(lambda: run(x_refs[-1], *outs_s))


def _resid_matmul_norm(a, w, layer, x, g, m_p, tm, final, name):
    m, k = a.shape
    d = w.shape[2]
    tm = _tile(math.gcd(m_p, m - m_p), tm)
    n_p = m_p // tm
    row = lambda i: (i, 0)
    split_in = isinstance(x, tuple)
    x_specs = list(_split_specs(tm, d, n_p)) if split_in else [pl.BlockSpec((tm, d), row)]
    x_args = list(x) if split_in else [x]
    if final:
        out_specs = list(_split_specs(tm, d, n_p))
        out_shape = [jax.ShapeDtypeStruct((m_p, d), F32), jax.ShapeDtypeStruct((m - m_p, d), F32)]
    else:
        out_specs = [pl.BlockSpec((tm, d), row), pl.BlockSpec((tm, d), row)]
        out_shape = [jax.ShapeDtypeStruct((m, d), F32), jax.ShapeDtypeStruct((m, d), BF16)]
    return pl.pallas_call(
        functools.partial(_resid_kernel, n_p, split_in, final),
        grid=(m // tm,),
        in_specs=[pl.BlockSpec((tm, k), row),
                  pl.BlockSpec((None, k, d), lambda i: (layer, 0, 0), pipeline_mode=pl.Buffered(1)),
                  *x_specs,
                  pl.BlockSpec((1, d), lambda i: (0, 0))],
        out_specs=out_specs,
        out_shape=out_shape,
        compiler_params=_compiler_params(("arbitrary",)),
        name=name,
    )(a, w, *x_args, g.reshape(1, d))


def _merge_kernel(n_p, h_ref, op_ref, os_ref, wm_ref, wb_ref, y_ref):
    tn = y_ref.shape[1]

    def merge_from(o_ref):
        gates = _dot_nt(h_ref[...], wm_ref[...])
        acc = None
        for n in range(N_BRANCH):
            br = _dot(o_ref[:, n * BRANCH_WIDTH:(n + 1) * BRANCH_WIDTH], wb_ref[n])
            term = _sigmoid(gates[:, n * tn:(n + 1) * tn]) * br
            acc = term if acc is None else acc + term
        y_ref[...] = acc.astype(y_ref.dtype)

    is_prompt = pl.program_id(1) < n_p
    pl.when(is_prompt)(lambda: merge_from(op_ref))
    pl.when(jnp.logical_not(is_prompt))(lambda: merge_from(os_ref))


def _merge(h, o, wm_packed, wb, layer, tn):
    m, d = h.shape
    o_p, o_s = o
    tm = _tile(math.gcd(o_p.shape[0], o_s.shape[0]), MERGE_TM)
    n_p = o_p.shape[0] // tm
    return pl.pallas_call(
        functools.partial(_merge_kernel, n_p),
        grid=(d // tn, m // tm),
        in_specs=[pl.BlockSpec((tm, d), lambda j, i: (i, 0)),
                  pl.BlockSpec((tm, d), lambda j, i: (jnp.minimum(i, n_p - 1), 0)),
                  pl.BlockSpec((tm, d), lambda j, i: (jnp.maximum(i - n_p, 0), 0)),
                  pl.BlockSpec((N_BRANCH * tn, d), lambda j, i: (j, 0)),
                  pl.BlockSpec((None, N_BRANCH, BRANCH_WIDTH, tn), lambda j, i: (layer, 0, 0, j))],
        out_specs=pl.BlockSpec((tm, tn), lambda j, i: (i, j)),
        out_shape=jax.ShapeDtypeStruct((m, d), BF16),
        compiler_params=_compiler_params(("arbitrary", "arbitrary")),
        name="merge",
    )(h, o_p, o_s, wm_packed, wb)


def _ffn_gate(a, prev1, prev2, u, cw_ref, cb_ref):
    conv = cw_ref[2:3, :] * a + cw_ref[1:2, :] * prev1 + cw_ref[0:1, :] * prev2 + cb_ref[...]
    return (_silu(conv) * u)


def _ffn_in_kernel(n_p, tiles_per_stream, seq_s, h_ref, wa_ref, wu_ref, cw_ref, cb_ref, cin_ref,
                   g_ref, cache_p_ref, cache_s_ref, buf_ref):
    tm = h_ref.shape[0]
    i = pl.program_id(1)

    def up_project():
        h = h_ref[...]
        return _dot(h, wa_ref[...]), _dot(h, wu_ref[...])

    @pl.when(jnp.logical_and(i < n_p, i % tiles_per_stream == 0))
    def _():
        buf_ref[0:SUBLANE, :] = jnp.zeros((SUBLANE, buf_ref.shape[1]), F32)

    @pl.when(i < n_p)
    def _():
        a, u = up_project()
        buf_ref[SUBLANE:SUBLANE + tm, :] = a
        prev1 = buf_ref[SUBLANE - 1:SUBLANE - 1 + tm, :]
        prev2 = buf_ref[SUBLANE - 2:SUBLANE - 2 + tm, :]
        g_ref[...] = _ffn_gate(a, prev1, prev2, u, cw_ref, cb_ref).astype(g_ref.dtype)
        last2 = a[tm - 2:tm, :]
        cache_p_ref[0] = last2
        buf_ref[SUBLANE - 2:SUBLANE, :] = last2

    @pl.when(i >= n_p)
    def _():
        a, u = up_project()
        for s in range(tm // seq_s):
            rows = slice(s * seq_s, (s + 1) * seq_s)
            a_s = a[rows]
            buf_ref[0:SUBLANE, :] = cin_ref[s]
            buf_ref[SUBLANE:SUBLANE + seq_s, :] = a_s
            prev1 = buf_ref[SUBLANE - 1:SUBLANE - 1 + seq_s, :]
            prev2 = buf_ref[SUBLANE - 2:SUBLANE - 2 + seq_s, :]
            g_ref[rows, :] = _ffn_gate(a_s, prev1, prev2, u[rows], cw_ref, cb_ref).astype(g_ref.dtype)
            cache_s_ref[s] = a_s[seq_s - 2:seq_s, :]


def _ffn_in(h, wa, wu, cw, cb, cache_in, layer, n_prompt_streams, prompt_seq, seq_s, tn):
    m, d = h.shape
    m_p = n_prompt_streams * prompt_seq
    n_s_streams = (m - m_p) // seq_s
    tm = _tile(math.gcd(prompt_seq, m - m_p), FFN_IN_TM)
    tiles_per_stream = prompt_seq // tm
    n_p = m_p // tm
    spt = tm // seq_s
    dff = wa.shape[2]
    s_blk = lambda j, i: (jnp.maximum(i - n_p, 0), 0, j)
    w_blk = lambda j, i: (layer, 0, j)
    return pl.pallas_call(
        functools.partial(_ffn_in_kernel, n_p, tiles_per_stream, seq_s),
        grid=(dff // tn, m // tm),
        in_specs=[pl.BlockSpec((tm, d), lambda j, i: (i, 0)),
                  pl.BlockSpec((None, d, tn), w_blk),
                  pl.BlockSpec((None, d, tn), w_blk),
                  pl.BlockSpec((None, FFN_CONV, tn), w_blk),
                  pl.BlockSpec((None, 1, tn), w_blk),
                  pl.BlockSpec((None, spt, SUBLANE, tn), lambda j, i: (layer,) + s_blk(j, i))],
        out_specs=[pl.BlockSpec((tm, tn), lambda j, i: (i, j)),
                   pl.BlockSpec((1, FFN_CONV - 1, tn),
                                lambda j, i: (jnp.minimum(i, n_p - 1) // tiles_per_stream, 0, j)),
                   pl.BlockSpec((spt, FFN_CONV - 1, tn), s_blk)],
        out_shape=[jax.ShapeDtypeStruct((m, D_FF), BF16),
                   jax.ShapeDtypeStruct((n_prompt_streams, FFN_CONV - 1, D_FF), F32),
                   jax.ShapeDtypeStruct((n_s_streams, FFN_CONV - 1, D_FF), F32)],
        scratch_shapes=[pltpu.VMEM((SUBLANE + tm, tn), F32)],
        compiler_params=_compiler_params(("arbitrary", "arbitrary")),
        name="ffn_in",
    )(h, wa, wu, cw, cb, cache_in)


def _level_matrix():
    t = np.arange(L)[:, None]
    u = np.arange(L)[None, :]
    blocks = [(u <= t)]
    for l in range(N_LEVELS):
        n = 1 << l
        r = (t & ~(2 * n - 1)) + n - 1
        upper = (t & n) != 0
        blocks.append(np.where(upper, (u > r) & (u <= t), (u > t) & (u <= r)))
    blocks.append(u > t)
    return np.concatenate(blocks, axis=0).astype(np.float32)


def _pair_levels():
    r = np.arange(GROUP * L)
    x = r[:, None] ^ r[None, :]
    code = np.full(x.shape, -1, np.int32)
    code[x == 0] = 0
    for l in range(N_LEVELS):
        code[(x >> l) == 1] = l + 1
    code[r[None, :] > r[:, None]] = -1
    return code


def _stack(x):
    return jnp.concatenate([x[:, h * HEAD_DV:(h + 1) * HEAD_DV] for h in range(N_HEADS)], axis=0)


def _stack_cols(x, first):
    return jnp.concatenate([x[:, first + h:first + h + 1] for h in range(N_HEADS)], axis=0)


def _groups(x):
    gl = GROUP * L
    return [x[g * gl:(g + 1) * gl] for g in range(N_HEADS // GROUP)]


def _pair_product(a, b):
    return [_dot_nt(ag, bg) for ag, bg in zip(_groups(a), _groups(b))]


def _block_diag(x):
    z = jnp.zeros((L, HEAD_DV), x.dtype)
    rows = [jnp.concatenate([x[h * L:(h + 1) * L] if j == h else z for j in range(N_HEADS)], axis=1)
            for h in range(N_HEADS)]
    return jnp.concatenate(rows, axis=0)


def _store_heads(o_ref, off, y):
    for h in range(N_HEADS):
        o_ref[:, off + h * HEAD_DV:off + (h + 1) * HEAD_DV] = y[h * L:(h + 1) * L].astype(o_ref.dtype)


def _head_rms(o, gain, gate):
    y = o * lax.rsqrt(jnp.mean(o * o, axis=-1, keepdims=True) + EPS)
    return y * gain * _silu(gate)


def _gla_type(q, k, v, g, gate, gain, st_ref, mall, lvl, o_ref, o_off):
    e_all = jnp.exp2(_dot(mall, (g * LOG2_E).astype(BF16)))

    def e_rows(r):
        return _stack(e_all[r * L:(r + 1) * L])

    qs, ks, vs = _stack(q), _stack(k), _stack(v)
    att = [jnp.where(lvl == 0, a, 0.0) for a in _pair_product(qs.astype(BF16), ks.astype(BF16))]
    for l in range(N_LEVELS):
        yield
        e_l = e_rows(l + 1)
        a_l = _pair_product((qs * e_l).astype(BF16), (ks * e_l).astype(BF16))
        att = [jnp.where(lvl == l + 1, a, prev) for a, prev in zip(a_l, att)]
    yield
    st = [st_ref[0, h] for h in range(N_HEADS)]
    vb = vs.astype(BF16)
    q_decayed = (qs * e_rows(0)).astype(BF16)
    q_state = jnp.concatenate(
        [_dot_nt(q_decayed[h * L:(h + 1) * L], st[h].astype(BF16)) for h in range(N_HEADS)], axis=0)
    o = jnp.concatenate([_dot(a.astype(BF16), vg) for a, vg in zip(att, _groups(vb))], axis=0) + q_state
    _store_heads(o_ref, o_off, _head_rms(o, gain, _stack(gate)))
    upd = _dot_tn(_block_diag(vb), (ks * e_rows(N_LEVELS + 1)).astype(BF16))
    for h in range(N_HEADS):
        decay_end = e_all[L - 1:L, h * HEAD_DV:(h + 1) * HEAD_DV]
        st_ref[0, h] = st[h] * decay_end + upd[h * HEAD_DV:(h + 1) * HEAD_DV]


CONV_BASE = SUBLANE - (GDN_CONV - 1)


def _lane(layer, proj_ref, cos_ref, sin_ref, mall_ref, lvl_ref, wgk_ref, bgk_ref, norms_ref, lb_ref,
          convw_ref, gdnp_ref, o_ref, xe_ref, sgla_ref, shg_ref, sgdn_ref, sret_ref):
    base = CONV_BASE
    lvl = lvl_ref[...]
    incl = lvl >= 0
    mall = mall_ref[...]
    tri = mall[0:L]

    def col(off, width=512):
        return proj_ref[:, off:off + width]

    z = _dot(col(P_GLA_LR, LANE).astype(BF16), wgk_ref[...]) + bgk_ref[...]
    g_gla = (jnp.minimum(z, 0.0) - jnp.log(1.0 + jnp.exp(-jnp.abs(z)))) * (1.0 / GLA_GATE_NORM)
    gla = _gla_type(col(P_GLA_Q) * GLA_DK ** -0.5, col(P_GLA_K), col(P_GLA_V), g_gla, col(P_GLA_G),
                    norms_ref[0:1, :], sgla_ref, mall, lvl, o_ref, 0)
    if layer == 0:
        lb = jnp.zeros((1, 512), F32)
    else:
        logits = lb_ref[...]
        ex = jnp.exp(logits - jnp.max(logits, axis=0, keepdims=True))
        soft = ex / jnp.sum(ex, axis=0, keepdims=True)
        lb = jnp.sum(soft[1:layer + 1], axis=0, keepdims=True)
    f = lb + (1.0 - lb) * _sigmoid(col(P_HG_F))
    hgrn = _gla_type(_silu(col(P_HG_Q)), 1.0 - f, col(P_HG_I), jnp.log(f), col(P_HG_G),
                     norms_ref[1:2, :], shg_ref, mall, lvl, o_ref, BRANCH_WIDTH)
    ret = _retention(col, cos_ref, sin_ref, norms_ref, sret_ref, incl, o_ref)

    xe_ref[SUBLANE:SUBLANE + L, :] = col(P_GDN_QKV, GDN_QKV)
    conv = convw_ref[0:1, :] * xe_ref[base:base + L, :]
    for j in range(1, GDN_CONV):
        conv = conv + convw_ref[j:j + 1, :] * xe_ref[base + j:base + j + L, :]
    xe_ref[base:SUBLANE, :] = xe_ref[SUBLANE + L - (GDN_CONV - 1):SUBLANE + L, :]
    act = _silu(conv)
    ba = col(P_GDN_BA, LANE)
    beta = _stack_cols(_sigmoid(ba), 0)
    log_a = -jnp.exp(gdnp_ref[0:1, :]) * _softplus(ba + gdnp_ref[1:2, :])
    la_hi, la_mid, la_lo = _split3(log_a)
    b_all = _dot(tri, la_hi) + (_dot(tri, la_mid) + _dot(tri, la_lo))
    b_col = _stack_cols(b_all, N_HEADS)
    b_rows = b_all.T
    b_row = [jnp.concatenate([b_rows[N_HEADS + GROUP * g + i:N_HEADS + GROUP * g + i + 1, :]
                              for i in range(GROUP)], axis=1) for g in range(N_HEADS // GROUP)]
    b_end = jnp.concatenate(
        [jnp.broadcast_to(b_all[L - 1:L, N_HEADS + h:N_HEADS + h + 1], (L, 1)) for h in range(N_HEADS)], axis=0)
    dq, dk, dv = _stack(act[:, 0:512]), _stack(act[:, 512:1024]), _stack(act[:, 1024:1536])
    qn = dq * lax.rsqrt(jnp.sum(dq * dq, axis=-1, keepdims=True) + EPS) * HEAD_DV ** -0.5
    kn = dk * lax.rsqrt(jnp.sum(dk * dk, axis=-1, keepdims=True) + EPS)
    qb, kb = qn.astype(BF16), kn.astype(BF16)
    dec = [jnp.exp(jnp.where(incl, bc - br, -1e30)) for bc, br in zip(_groups(b_col), b_row)]
    e_b = jnp.exp(b_col)
    st = [sgdn_ref[0, h] for h in range(N_HEADS)]
    kq_state = [_dot(jnp.concatenate([kb[h * L:(h + 1) * L], qb[h * L:(h + 1) * L]], axis=0),
                     st[h].astype(BF16)) for h in range(N_HEADS)]
    k_state = jnp.concatenate([x[0:L] for x in kq_state], axis=0)
    q_state = jnp.concatenate([x[L:2 * L] for x in kq_state], axis=0)
    rhs = beta * (dv - e_b * k_state)
    p_groups = [-(jnp.where(lvl > 0, kk * dec_g, 0.0) * beta_g)
                for kk, dec_g, beta_g in zip(_pair_product(kb, kb), dec, _groups(beta))]
    u_groups = _groups(rhs)
    next(gla)
    next(hgrn)
    for j in range(N_LEVELS):
        yield
        u_groups = [u + _dot_hi(p, u) for p, u in zip(p_groups, u_groups)]
        if j + 1 < N_LEVELS:
            p_groups = [_dot_hi(p, p) for p in p_groups]
        next(gla)
        next(hgrn)
        next(ret, None)
    yield
    ub = jnp.concatenate(u_groups, axis=0).astype(BF16)
    att = [(qk * dec_g).astype(BF16) for qk, dec_g in zip(_pair_product(qb, kb), dec)]
    o = jnp.concatenate([_dot(a, ug) for a, ug in zip(att, _groups(ub))], axis=0) + e_b * q_state
    _store_heads(o_ref, 2 * BRANCH_WIDTH, _head_rms(o, norms_ref[2:3, :], _stack(col(P_GDN_Z))))
    upd = _dot_tn((kn * jnp.exp(b_end - b_col)).astype(BF16), _block_diag(ub))
    for h in range(N_HEADS):
        decay_end = jnp.exp(b_all[L - 1:L, N_HEADS + h:N_HEADS + h + 1])
        sgdn_ref[0, h] = decay_end * st[h] + upd[:, h * HEAD_DV:(h + 1) * HEAD_DV]

    for gen in (gla, hgrn, ret):
        for _ in gen:
            pass


def _retention(col, cos_ref, sin_ref, norms_ref, sret_ref, incl, o_ref):
    hl = N_HEADS * L
    cos = jnp.concatenate([cos_ref[...]] * N_HEADS, axis=0)
    sin = jnp.concatenate([sin_ref[...]] * N_HEADS, axis=0)
    rq, rk = _stack(col(P_RET_Q)), _stack(col(P_RET_K))
    vb = _stack(col(P_RET_V)).astype(BF16)
    row = lax.broadcasted_iota(jnp.int32, (hl, 1), 0)
    t_col = jnp.bitwise_and(row, L - 1).astype(F32)
    lg = jnp.zeros((hl, 1), F32)
    for h in range(N_HEADS):
        lg = jnp.where(jnp.right_shift(row, N_LEVELS) == h, math.log(1.0 - 2.0 ** (-5.0 - h)), lg)
    gl = GROUP * L
    t_minus_s = (lax.broadcasted_iota(jnp.int32, (gl, 1), 0)
                 - lax.broadcasted_iota(jnp.int32, (1, gl), 1)).astype(F32)
    dec = [jnp.exp(jnp.where(incl, t_minus_s * lg_g, -1e30)) for lg_g in _groups(lg)]
    qr = (rq * cos + pltpu.roll(rq, HEAD_DV // 2, axis=1) * sin).astype(BF16)
    kr = (rk * cos + pltpu.roll(rk, HEAD_DV // 2, axis=1) * sin) * HEAD_DV ** -0.5
    kb = kr.astype(BF16)
    yield
    st = [sret_ref[0, h] for h in range(N_HEADS)]
    q_state = jnp.concatenate(
        [_dot(qr[h * L:(h + 1) * L], st[h].astype(BF16)) for h in range(N_HEADS)], axis=0)
    att = [(qk * dec_g).astype(BF16) for qk, dec_g in zip(_pair_product(qr, kb), dec)]
    yield
    o = (jnp.concatenate([_dot(a, vg) for a, vg in zip(att, _groups(vb))], axis=0)
         + jnp.exp((t_col + 1.0) * lg) * q_state)
    yield
    mu = jnp.mean(o, axis=-1, keepdims=True)
    oc = o - mu
    y = oc * lax.rsqrt(jnp.mean(oc * oc, axis=-1, keepdims=True) + EPS)
    y = (y * norms_ref[3:4, :] + norms_ref[4:5, :]) * _silu(_stack(col(P_RET_G)))
    _store_heads(o_ref, 3 * BRANCH_WIDTH, y)
    yield
    k_end = (kr * jnp.exp((L - 1.0 - t_col) * lg)).astype(BF16)
    upd = _dot_tn(k_end, _block_diag(vb))
    for h in range(N_HEADS):
        gamma_l = math.exp(L * math.log(1.0 - 2.0 ** (-5.0 - h)))
        sret_ref[0, h] = gamma_l * st[h] + upd[:, h * HEAD_DV:(h + 1) * HEAD_DV]


N_LANES = 2


def _run_lanes(lanes):
    lanes = list(lanes)
    while lanes:
        for lane in list(lanes):
            if next(lane, StopIteration) is StopIteration:
                lanes.remove(lane)


def _write_states(k, carry, xe_ref, gla_out, hg_out, gdn_out, ret_out, cache_out):
    sgla_ref, shg_ref, sgdn_ref, sret_ref = carry
    for h in range(N_HEADS):
        gla_out[k, h] = sgla_ref[k, 0, h].T[0:GLA_DK]
        hg_out[k, h] = shg_ref[k, 0, h].T
    gdn_out[k] = sgdn_ref[k, 0]
    ret_out[k] = sret_ref[k, 0]
    cache_out[k] = xe_ref[k, CONV_BASE:SUBLANE, :]


def _mixer_prompt_kernel(layer, n_chunks, *refs):
    proj_refs, refs = refs[:N_LANES], refs[N_LANES:]
    consts, refs = refs[:10], refs[10:]
    o_ref, outs, (xe_ref, *carry) = refs[0], refs[1:6], refs[6:]
    c = pl.program_id(0)

    @pl.when(c == 0)
    def _():
        for ref in carry:
            ref[...] = jnp.zeros(ref.shape, F32)
        xe_ref[...] = jnp.zeros(xe_ref.shape, F32)

    _run_lanes(_lane(layer, proj_refs[k], *consts, o_ref.at[k], xe_ref.at[k],
                     *[s.at[k] for s in carry]) for k in range(N_LANES))

    @pl.when(c == n_chunks - 1)
    def _():
        for k in range(N_LANES):
            _write_states(k, carry, xe_ref, *outs)


def _mixer_sample_kernel(layer, proj_ref, sgla_in, shg_in, sgdn_in, sret_in, cgdn_in, *refs):
    consts, refs = refs[:10], refs[10:]
    o_ref, outs, (xe_ref, *carry) = refs[0], refs[1:6], refs[6:]
    sgla_ref, shg_ref, sgdn_ref, sret_ref = carry
    key_pad = jnp.zeros((HEAD_DV - GLA_DK, HEAD_DV), F32)
    for k in range(N_LANES):
        for h in range(N_HEADS):
            sgla_ref[k, 0, h] = jnp.concatenate([sgla_in[k, h], key_pad], axis=0).T
            shg_ref[k, 0, h] = shg_in[k, h].T
        sgdn_ref[k, 0] = sgdn_in[k]
        sret_ref[k, 0] = sret_in[k]
        xe_ref[k, CONV_BASE:SUBLANE, :] = cgdn_in[k]
    _run_lanes(_lane(layer, proj_ref.at[pl.ds(k * L, L)], *consts, o_ref.at[pl.ds(k * L, L)], xe_ref.at[k],
                     *[s.at[k] for s in carry]) for k in range(N_LANES))
    for k in range(N_LANES):
        _write_states(k, carry, xe_ref, *outs)


def _mixers(layer, proj, states, cache, cos_t, sin_t, mall, lvl, wgk, bgk, norms, lb_logits,
            convw, gdnp, n_prompt_streams, prompt_seq):
    t_total = proj.shape[0]
    chunks_per_stream = prompt_seq // L
    t_prompt = n_prompt_streams * prompt_seq
    n_sample = states[0].shape[1]
    past_block = PAST_LEN // L
    assert n_prompt_streams == N_LANES and n_sample % N_LANES == 0

    def layer_block(a):
        return pl.BlockSpec((None,) + a.shape[1:], lambda c: (layer,) + (0,) * (a.ndim - 1))

    const = lambda a: pl.BlockSpec(a.shape, lambda c: (0,) * a.ndim)
    per_stream = list(states) + [cache]
    lanes_out = lambda idx: [pl.BlockSpec((N_LANES,) + a.shape[2:],
                                          lambda c, a=a: (idx(c),) + (0,) * (a.ndim - 2))
                             for a in per_stream]
    out_shapes = lambda n: [jax.ShapeDtypeStruct((n,) + a.shape[2:], F32) for a in per_stream]
    carry = pltpu.VMEM((N_LANES, 1, N_HEADS, HEAD_DV, HEAD_DV), F32)
    scratch = [pltpu.VMEM((N_LANES, SUBLANE + L, GDN_QKV), F32), carry, carry, carry, carry]

    def consts(pos_block):
        specs = [pl.BlockSpec((L, HEAD_DV), lambda c: (pos_block(c), 0)),
                 pl.BlockSpec((L, HEAD_DV), lambda c: (pos_block(c), 0)),
                 const(mall), const(lvl), layer_block(wgk), layer_block(bgk),
                 layer_block(norms), const(lb_logits), layer_block(convw), layer_block(gdnp)]
        return specs, (cos_t, sin_t, mall, lvl, wgk, bgk, norms, lb_logits, convw, gdnp)

    c_specs, c_args = consts(lambda c: c)
    o_p, *st_p = pl.pallas_call(
        functools.partial(_mixer_prompt_kernel, layer, chunks_per_stream),
        grid=(chunks_per_stream,),
        in_specs=[*[pl.BlockSpec((L, N_PACK), lambda c, k=k: (k * chunks_per_stream + c, 0))
                    for k in range(N_LANES)], *c_specs],
        out_specs=[pl.BlockSpec((N_LANES, L, D_MODEL), lambda c: (0, c, 0)), *lanes_out(lambda c: 0)],
        out_shape=[jax.ShapeDtypeStruct((N_LANES, prompt_seq, D_MODEL), BF16), *out_shapes(N_LANES)],
        scratch_shapes=scratch,
        compiler_params=_compiler_params(("arbitrary",)),
        name="mixers_prompt",
    )(*[proj] * N_LANES, *c_args)

    c_specs, c_args = consts(lambda s: past_block)
    first_block = t_prompt // (N_LANES * L)
    o_s, *st_s = pl.pallas_call(
        functools.partial(_mixer_sample_kernel, layer),
        grid=(n_sample // N_LANES,),
        in_specs=[pl.BlockSpec((N_LANES * L, N_PACK), lambda s: (first_block + s, 0)),
                  *[pl.BlockSpec((None, N_LANES) + a.shape[2:], lambda s, a=a: (layer, s) + (0,) * (a.ndim - 2))
                    for a in per_stream],
                  *c_specs],
        out_specs=[pl.BlockSpec((N_LANES * L, D_MODEL), lambda s: (s, 0)), *lanes_out(lambda s: s)],
        out_shape=[jax.ShapeDtypeStruct((t_total - t_prompt, D_MODEL), BF16), *out_shapes(n_sample)],
        scratch_shapes=scratch,
        compiler_params=_compiler_params(("arbitrary",)),
        name="mixers_sample",
    )(proj, *per_stream, *c_args)
    return (o_p.reshape(t_prompt, D_MODEL), o_s), st_p, st_s


def _pad_heads(w, dk):
    lead = w.shape[:-1]
    w = w.reshape(lead + (N_HEADS, dk))
    w = jnp.pad(w, [(0, 0)] * len(lead) + [(0, 0), (0, HEAD_DV - dk)])
    return w.reshape(lead + (N_HEADS * HEAD_DV,))


def _pad_cols(w, width):
    return jnp.pad(w, [(0, 0)] * (w.ndim - 1) + [(0, width - w.shape[-1])])


def _mix_pieces():
    pieces = []
    for h in range(N_HEADS):
        pieces.append((P_GLA_Q + h * HEAD_DV, h * GLA_DK, GLA_DK, HEAD_DV))
        pieces.append((P_GLA_K + h * HEAD_DV, 256 + h * GLA_DK, GLA_DK, HEAD_DV))
    pieces += [(P_GLA_V, 512, 512, 512), (P_GLA_G, 1024, 512, 512), (P_GLA_LR, 1536, GLA_LOW_RANK, LANE),
               (P_HG_Q, 1552, 512, 512), (P_HG_F, 2064, 512, 512), (P_HG_I, 2576, 512, 512),
               (P_HG_G, 3088, 512, 512), (P_GDN_QKV, 3600, GDN_QKV, GDN_QKV), (P_GDN_Z, 5136, 512, 512),
               (P_GDN_BA, 5648, 2 * N_HEADS, LANE), (P_RET_Q, 5656, 512, 512), (P_RET_K, 6168, 512, 512),
               (P_RET_V, 6680, 512, 512), (P_RET_G, 7192, 512, 512)]
    return pieces


def _mix_units():
    units = [None] * (N_PACK // LANE)
    for dst, src, n, padded in _mix_pieces():
        for q in range(padded // LANE):
            units[dst // LANE + q] = (src + q * LANE, min(LANE, n - q * LANE))
    return units


def _merge_units(merge_tn):
    return [(N_MIX_IN + n * D_MODEL + j * merge_tn + q * LANE, LANE)
            for j in range(D_MODEL // merge_tn) for n in range(N_BRANCH) for q in range(merge_tn // LANE)]


def _pack_rows_kernel(per_step, src_ref, valid_ref, *refs):
    w_refs, o_ref = refs[:per_step], refs[per_step]
    s = pl.program_id(0)
    row = lax.broadcasted_iota(jnp.int32, (LANE, 1), 0)
    for q in range(per_step):
        x = jnp.where(row < valid_ref[s * per_step + q], w_refs[q][...], 0.0)
        o_ref[q * LANE:(q + 1) * LANE, :] = x.astype(o_ref.dtype)


def _pack_rows(w_t, layer, units, per_step, name):
    _, n_in, k = w_t.shape
    assert n_in % SUBLANE == 0 and all(u[0] % SUBLANE == 0 for u in units)
    w_t = w_t.reshape(-1, k)
    src = jnp.asarray([(layer * n_in + u[0]) // SUBLANE for u in units], jnp.int32)
    valid = jnp.asarray([u[1] for u in units], jnp.int32)
    grid_spec = pltpu.PrefetchScalarGridSpec(
        num_scalar_prefetch=2,
        grid=(len(units) // per_step,),
        in_specs=[pl.BlockSpec((pl.Element(LANE), pl.Element(k)),
                               lambda s, src, valid, q=q: (src[s * per_step + q] * SUBLANE, 0))
                  for q in range(per_step)],
        out_specs=pl.BlockSpec((per_step * LANE, k), lambda s, src, valid: (s, 0)))
    return pl.pallas_call(
        functools.partial(_pack_rows_kernel, per_step),
        grid_spec=grid_spec,
        out_shape=jax.ShapeDtypeStruct((len(units) * LANE, k), BF16),
        compiler_params=_compiler_params(("arbitrary",)),
        name=name,
    )(src, valid, *[w_t] * per_step)


def _cast_rows_kernel(n_valid, w_ref, o_ref):
    i = pl.program_id(1)

    @pl.when(i < n_valid)
    def _():
        o_ref[...] = w_ref[...].astype(o_ref.dtype)

    @pl.when(i >= n_valid)
    def _():
        o_ref[...] = jnp.zeros(o_ref.shape, o_ref.dtype)


def _cast_rows(w, rows_out, tk):
    depth, r, c = w.shape
    n_valid = r // tk
    assert r % tk == 0 and rows_out % tk == 0
    return pl.pallas_call(
        functools.partial(_cast_rows_kernel, n_valid),
        grid=(depth, rows_out // tk),
        in_specs=[pl.BlockSpec((None, tk, c), lambda l, i: (l, jnp.minimum(i, n_valid - 1), 0))],
        out_specs=pl.BlockSpec((None, tk, c), lambda l, i: (l, i, 0)),
        out_shape=jax.ShapeDtypeStruct((depth, rows_out, c), BF16),
        compiler_params=_compiler_params(("parallel", "arbitrary")),
        name="cast_rows",
    )(w)


def _split_ffn_in_kernel(w_ref, a_ref, u_ref):
    pad = jnp.zeros((a_ref.shape[0], D_FF_PAD - D_FF), BF16)
    a_ref[:, 0:D_FF] = w_ref[:, 0:D_FF].astype(BF16)
    a_ref[:, D_FF:D_FF_PAD] = pad
    u_ref[:, 0:D_FF] = w_ref[:, D_FF:2 * D_FF].astype(BF16)
    u_ref[:, D_FF:D_FF_PAD] = pad


def _split_ffn_in(w, tk):
    depth, d, _ = w.shape
    blk = lambda c: pl.BlockSpec((None, tk, c), lambda l, i: (l, i, 0))
    out = jax.ShapeDtypeStruct((depth, d, D_FF_PAD), BF16)
    return pl.pallas_call(
        _split_ffn_in_kernel,
        grid=(depth, d // tk),
        in_specs=[blk(2 * D_FF)],
        out_specs=[blk(D_FF_PAD), blk(D_FF_PAD)],
        out_shape=[out, out],
        compiler_params=_compiler_params(("parallel", "parallel")),
        name="split_ffn_in",
    )(w)


def _rope_tables(n_pos):
    pos = np.arange(n_pos, dtype=np.float64)
    inv = 1.0 / (ROPE_BASE ** (np.arange(0, HEAD_DV, 2, dtype=np.float64) / HEAD_DV))
    ang = pos[:, None] * inv[None, :]
    cos, sin = np.cos(ang), np.sin(ang)
    return (jnp.asarray(np.concatenate([cos, cos], axis=1), F32),
            jnp.asarray(np.concatenate([-sin, sin], axis=1), F32))


def kernel(x_prompt, x_sample, state_gla, state_hgrn, state_gdn, state_ret, cache_gdn_conv, cache_ffn_conv, norm_mix_g, w_in, gla_w_gk, gla_b_gk, gla_norm_g, hgrn_lb_logits, hgrn_norm_g, gdn_conv_w, gdn_a_log, gdn_dt_bias, gdn_norm_g, ret_norm_g, ret_norm_b, w_branch, w_out, norm_ffn_g, w_ffn_in, ffn_conv_w, ffn_conv_b, w_ffn_out, norm_final_g):
    nb, seq, d = x_prompt.shape
    ns, dseq, _ = x_sample.shape
    assert d == D_MODEL and dseq == L and seq % L == 0
    tp = nb * seq
    x = (x_prompt.reshape(tp, d), x_sample.reshape(ns * dseq, d))

    cos_t, sin_t = _rope_tables(max(seq, PAST_LEN + dseq))
    mall = jnp.asarray(_level_matrix(), BF16)
    lvl = jnp.asarray(_pair_levels())
    wgk = jnp.pad(_pad_heads(gla_w_gk, GLA_DK), ((0, 0), (0, LANE - GLA_LOW_RANK), (0, 0))).astype(BF16)
    bgk = _pad_heads(gla_b_gk, GLA_DK)[:, None, :]
    norms = jnp.stack([gla_norm_g, hgrn_norm_g, gdn_norm_g, ret_norm_g, ret_norm_b]
                      + [jnp.zeros_like(gla_norm_g)] * 3, axis=1)
    head_lanes = ((0, 0), (N_HEADS, LANE - 2 * N_HEADS))
    gdnp = jnp.stack([jnp.pad(gdn_a_log, head_lanes), jnp.pad(gdn_dt_bias, head_lanes)], axis=1)
    w_branch_b = _cast_rows(w_branch.reshape(DEPTH, N_BRANCH * BRANCH_WIDTH, d), N_BRANCH * BRANCH_WIDTH,
                            CAST_TK).reshape(w_branch.shape)
    w_out_b = _cast_rows(w_out, d, CAST_TK)
    wa, wu = _split_ffn_in(w_ffn_in, LANE)
    w_dn = _cast_rows(w_ffn_out, D_FF, D_FF // SUBLANE)
    cw = _pad_cols(ffn_conv_w, D_FF_PAD)
    cb = _pad_cols(ffn_conv_b, D_FF_PAD)[:, None, :]
    cache_in = jnp.pad(cache_ffn_conv,
                       ((0, 0), (0, 0), (SUBLANE - (FFN_CONV - 1), 0), (0, D_FF_PAD - D_FF)))
    states = (state_gla, state_hgrn, state_gdn, state_ret)
    w_in_t = jnp.swapaxes(w_in, 1, 2)

    h = _rmsnorm(x[0], x[1], norm_mix_g[0], BF16)
    new_p, new_s = [], []
    for l in range(DEPTH):
        w_mix = _pack_rows(w_in_t, l, _mix_units(), PACK_MIX_GROUPS, "pack_w_mix")
        w_merge = _pack_rows(w_in_t, l, _merge_units(MERGE_TN), PACK_MERGE_GROUPS, "pack_w_merge")
        proj = _matmul(h, w_mix, IN_PROJ_TN, F32)
        o, st_p, st_s = _mixers(l, proj, states, cache_gdn_conv, cos_t, sin_t, mall, lvl, wgk, bgk, norms,
                                hgrn_lb_logits, gdn_conv_w, gdnp, nb, seq)
        y = _merge(h, o, w_merge, w_branch_b, l, MERGE_TN)
        x, h2 = _resid_matmul_norm(y, w_out_b, l, x, norm_ffn_g[l], tp, OUT_PROJ_TM, False, "out_proj")
        g, c_ffn_p, c_ffn_s = _ffn_in(h2, wa, wu, cw, cb, cache_in, l, nb, seq, dseq, FFN_IN_TN)
        if l + 1 < DEPTH:
            x, h = _resid_matmul_norm(g, w_dn, l, x, norm_mix_g[l + 1], tp, FFN_OUT_TM, False, "ffn_out")
        else:
            y_p, y_s = _resid_matmul_norm(g, w_dn, l, x, norm_final_g, tp, FFN_OUT_TM, True, "ffn_out_final")
        new_p.append(st_p + [c_ffn_p])
        new_s.append(st_s + [c_ffn_s])

    y_prompt = y_p.reshape(nb, seq, d)
    y_sample = y_s.reshape(ns, dseq, d)
    p_out = tuple(jnp.stack([st[i] for st in new_p]) for i in range(6))
    s_out = tuple(jnp.stack([st[i] for st in new_s]) for i in range(6))
    return (y_prompt, y_sample) + p_out + s_out
```

```python
import functools
import math

import numpy as np
import jax
import jax.numpy as jnp
from jax import lax
from jax.experimental import pallas as pl
from jax.experimental.pallas import tpu as pltpu

F32 = jnp.float32
BF16 = jnp.bfloat16

D_MODEL = 2048
DEPTH = 2
PAST_LEN = 4096
L = 64
N_BRANCH = 4
BRANCH_WIDTH = 512
N_HEADS = 4
HEAD_DV = 128
GLA_DK = 64
GLA_LOW_RANK = 16
GLA_GATE_NORM = 16.0
GDN_CONV = 4
GDN_QKV = 1536
ROPE_BASE = 10000.0
D_FF = 5504
D_FF_PAD = 5632
FFN_CONV = 3
EPS = 1e-6
N_MIX_IN = 7704

LANE = 128
SUBLANE = 8
VMEM_LIMIT = 56 * 1024 * 1024

NORM_TM = 1024
IN_PROJ_TM, IN_PROJ_TN = 512, 2816
MERGE_TM, MERGE_TN = 1024, 512
OUT_PROJ_TM = 512
FFN_IN_TM, FFN_IN_TN = 1024, 512
FFN_OUT_TM = 256
CAST_TK = 256
PACK_MIX_GROUPS, PACK_MERGE_GROUPS = 3, 4

P_GLA_Q, P_GLA_K, P_GLA_V, P_GLA_G, P_GLA_LR = 0, 512, 1024, 1536, 2048
P_HG_Q, P_HG_F, P_HG_I, P_HG_G = 2176, 2688, 3200, 3712
P_GDN_QKV, P_GDN_Z, P_GDN_BA = 4224, 5760, 6272
P_RET_Q, P_RET_K, P_RET_V, P_RET_G = 6400, 6912, 7424, 7936
N_PACK = 8448
N_LEVELS = 6
GROUP = 2
LOG2_E = math.log2(math.e)


def _dot(a, b):
    return jnp.dot(a, b, preferred_element_type=F32)


def _dot_nt(a, b):
    return lax.dot_general(a, b, (((1,), (1,)), ((), ())), preferred_element_type=F32)


def _dot_tn(a, b):
    return lax.dot_general(a, b, (((0,), (0,)), ((), ())), preferred_element_type=F32)


def _split3(x):
    hi = x.astype(BF16)
    r = x - hi.astype(F32)
    mid = r.astype(BF16)
    lo = (r - mid.astype(F32)).astype(BF16)
    return hi, mid, lo


def _split2(x):
    hi = x.astype(BF16)
    lo = (x - hi.astype(F32)).astype(BF16)
    return hi, lo


def _dot_hi(a, b):
    a_hi, a_lo = _split2(a)
    b_hi, b_lo = _split2(b)
    return _dot(a_hi, b_hi) + (_dot(a_hi, b_lo) + _dot(a_lo, b_hi))


def _sigmoid(x):
    return 1.0 / (1.0 + jnp.exp(-x))


def _silu(x):
    return x * _sigmoid(x)


def _softplus(x):
    return jnp.maximum(x, 0.0) + jnp.log(1.0 + jnp.exp(-jnp.abs(x)))


def _compiler_params(semantics):
    return pltpu.CompilerParams(dimension_semantics=semantics, vmem_limit_bytes=VMEM_LIMIT)


def _tile(n, pref):
    t = min(n, pref)
    while n % t:
        t //= 2
    return t


def _split_specs(tm, d, n_p):
    return (pl.BlockSpec((tm, d), lambda i: (jnp.minimum(i, n_p - 1), 0)),
            pl.BlockSpec((tm, d), lambda i: (jnp.maximum(i - n_p, 0), 0)))


def _rmsnorm_kernel(n_p, xp_ref, xs_ref, g_ref, h_ref):
    x = jnp.where(pl.program_id(0) < n_p, xp_ref[...], xs_ref[...])
    y = x * lax.rsqrt(jnp.mean(x * x, axis=-1, keepdims=True) + EPS)
    h_ref[...] = (y * g_ref[...]).astype(h_ref.dtype)


def _rmsnorm(xp, xs, g, out_dtype):
    d = xp.shape[1]
    m = xp.shape[0] + xs.shape[0]
    tm = _tile(math.gcd(xp.shape[0], xs.shape[0]), NORM_TM)
    n_p = xp.shape[0] // tm
    return pl.pallas_call(
        functools.partial(_rmsnorm_kernel, n_p),
        grid=(m // tm,),
        in_specs=[*_split_specs(tm, d, n_p), pl.BlockSpec((1, d), lambda i: (0, 0))],
        out_specs=pl.BlockSpec((tm, d), lambda i: (i, 0)),
        out_shape=jax.ShapeDtypeStruct((m, d), out_dtype),
        compiler_params=_compiler_params(("arbitrary",)),
        name="rmsnorm",
    )(xp, xs, g.reshape(1, d))


def _mm_kernel(a_ref, w_ref, o_ref):
    o_ref[...] = _dot_nt(a_ref[...], w_ref[...]).astype(o_ref.dtype)


def _matmul(a, w_t, tn, out_dtype):
    m, k = a.shape
    n = w_t.shape[0]
    tm = _tile(m, IN_PROJ_TM)
    return pl.pallas_call(
        _mm_kernel,
        grid=(n // tn, m // tm),
        in_specs=[pl.BlockSpec((tm, k), lambda j, i: (i, 0)),
                  pl.BlockSpec((tn, k), lambda j, i: (j, 0))],
        out_specs=pl.BlockSpec((tm, tn), lambda j, i: (i, j)),
        out_shape=jax.ShapeDtypeStruct((m, n), out_dtype),
        compiler_params=_compiler_params(("parallel", "parallel")),
        name="in_proj",
    )(a, w_t)


def _resid_kernel(n_p, split_in, final, a_ref, w_ref, *refs):
    n_x = 2 if split_in else 1
    x_refs, g_ref, out_refs = refs[:n_x], refs[n_x], refs[n_x + 1:]

    def run(x_ref, x_out_ref, h_out_ref):
        x = x_ref[...] + _dot(a_ref[...], w_ref[...])
        h = x * lax.rsqrt(jnp.mean(x * x, axis=-1, keepdims=True) + EPS) * g_ref[...]
        if x_out_ref is not None:
            x_out_ref[...] = x
        h_out_ref[...] = h.astype(h_out_ref.dtype)

    if not (split_in or final):
        run(x_refs[0], out_refs[0], out_refs[1])
        return
    outs_p = (None, out_refs[0]) if final else tuple(out_refs)
    outs_s = (None, out_refs[1]) if final else tuple(out_refs)
    is_prompt = pl.program_id(0) < n_p
    pl.when(is_prompt)(lambda: run(x_refs[0], *outs_p))
    pl.when(jnp.logical_not(is_prompt))(lambda: run(x_refs[-1], *outs_s))


def _resid_matmul_norm(a, w, layer, x, g, m_p, tm, final, name):
    m, k = a.shape
    d = w.shape[2]
    tm = _tile(math.gcd(m_p, m - m_p), tm)
    n_p = m_p // tm
    row = lambda i: (i, 0)
    split_in = isinstance(x, tuple)
    x_specs = list(_split_specs(tm, d, n_p)) if split_in else [pl.BlockSpec((tm, d), row)]
    x_args = list(x) if split_in else [x]
    if final:
        out_specs = list(_split_specs(tm, d, n_p))
        out_shape = [jax.ShapeDtypeStruct((m_p, d), F32), jax.ShapeDtypeStruct((m - m_p, d), F32)]
    else:
        out_specs = [pl.BlockSpec((tm, d), row), pl.BlockSpec((tm, d), row)]
        out_shape = [jax.ShapeDtypeStruct((m, d), F32), jax.ShapeDtypeStruct((m, d), BF16)]
    return pl.pallas_call(
        functools.partial(_resid_kernel, n_p, split_in, final),
        grid=(m // tm,),
        in_specs=[pl.BlockSpec((tm, k), row),
                  pl.BlockSpec((None, k, d), lambda i: (layer, 0, 0), pipeline_mode=pl.Buffered(1)),
                  *x_specs,
                  pl.BlockSpec((1, d), lambda i: (0, 0))],
        out_specs=out_specs,
        out_shape=out_shape,
        compiler_params=_compiler_params(("arbitrary",)),
        name=name,
    )(a, w, *x_args, g.reshape(1, d))


def _merge_kernel(n_p, h_ref, op_ref, os_ref, wm_ref, wb_ref, y_ref):
    tn = y_ref.shape[1]

    def merge_from(o_ref):
        gates = _dot_nt(h_ref[...], wm_ref[...])
        acc = None
        for n in range(N_BRANCH):
            br = _dot(o_ref[:, n * BRANCH_WIDTH:(n + 1) * BRANCH_WIDTH], wb_ref[n])
            term = _sigmoid(gates[:, n * tn:(n + 1) * tn]) * br
            acc = term if acc is None else acc + term
        y_ref[...] = acc.astype(y_ref.dtype)

    is_prompt = pl.program_id(1) < n_p
    pl.when(is_prompt)(lambda: merge_from(op_ref))
    pl.when(jnp.logical_not(is_prompt))(lambda: merge_from(os_ref))


def _merge(h, o, wm_packed, wb, layer, tn):
    m, d = h.shape
    o_p, o_s = o
    tm = _tile(math.gcd(o_p.shape[0], o_s.shape[0]), MERGE_TM)
    n_p = o_p.shape[0] // tm
    return pl.pallas_call(
        functools.partial(_merge_kernel, n_p),
        grid=(d // tn, m // tm),
        in_specs=[pl.BlockSpec((tm, d), lambda j, i: (i, 0)),
                  pl.BlockSpec((tm, d), lambda j, i: (jnp.minimum(i, n_p - 1), 0)),
                  pl.BlockSpec((tm, d), lambda j, i: (jnp.maximum(i - n_p, 0), 0)),
                  pl.BlockSpec((N_BRANCH * tn, d), lambda j, i: (j, 0)),
                  pl.BlockSpec((None, N_BRANCH, BRANCH_WIDTH, tn), lambda j, i: (layer, 0, 0, j))],
        out_specs=pl.BlockSpec((tm, tn), lambda j, i: (i, j)),
        out_shape=jax.ShapeDtypeStruct((m, d), BF16),
        compiler_params=_compiler_params(("arbitrary", "arbitrary")),
        name="merge",
    )(h, o_p, o_s, wm_packed, wb)


def _ffn_gate(a, prev1, prev2, u, cw_ref, cb_ref):
    conv = cw_ref[2:3, :] * a + cw_ref[1:2, :] * prev1 + cw_ref[0:1, :] * prev2 + cb_ref[...]
    return (_silu(conv) * u)


def _ffn_in_kernel(n_p, tiles_per_stream, seq_s, h_ref, wa_ref, wu_ref, cw_ref, cb_ref, cin_ref,
                   g_ref, cache_p_ref, cache_s_ref, buf_ref):
    tm = h_ref.shape[0]
    i = pl.program_id(1)

    def up_project():
        h = h_ref[...]
        return _dot(h, wa_ref[...]), _dot(h, wu_ref[...])

    @pl.when(jnp.logical_and(i < n_p, i % tiles_per_stream == 0))
    def _():
        buf_ref[0:SUBLANE, :] = jnp.zeros((SUBLANE, buf_ref.shape[1]), F32)

    @pl.when(i < n_p)
    def _():
        a, u = up_project()
        buf_ref[SUBLANE:SUBLANE + tm, :] = a
        prev1 = buf_ref[SUBLANE - 1:SUBLANE - 1 + tm, :]
        prev2 = buf_ref[SUBLANE - 2:SUBLANE - 2 + tm, :]
        g_ref[...] = _ffn_gate(a, prev1, prev2, u, cw_ref, cb_ref).astype(g_ref.dtype)
        last2 = a[tm - 2:tm, :]
        cache_p_ref[0] = last2
        buf_ref[SUBLANE - 2:SUBLANE, :] = last2

    @pl.when(i >= n_p)
    def _():
        a, u = up_project()
        for s in range(tm // seq_s):
            rows = slice(s * seq_s, (s + 1) * seq_s)
            a_s = a[rows]
            buf_ref[0:SUBLANE, :] = cin_ref[s]
            buf_ref[SUBLANE:SUBLANE + seq_s, :] = a_s
            prev1 = buf_ref[SUBLANE - 1:SUBLANE - 1 + seq_s, :]
            prev2 = buf_ref[SUBLANE - 2:SUBLANE - 2 + seq_s, :]
            g_ref[rows, :] = _ffn_gate(a_s, prev1, prev2, u[rows], cw_ref, cb_ref).astype(g_ref.dtype)
            cache_s_ref[s] = a_s[seq_s - 2:seq_s, :]


def _ffn_in(h, wa, wu, cw, cb, cache_in, layer, n_prompt_streams, prompt_seq, seq_s, tn):
    m, d = h.shape
    m_p = n_prompt_streams * prompt_seq
    n_s_streams = (m - m_p) // seq_s
    tm = _tile(math.gcd(prompt_seq, m - m_p), FFN_IN_TM)
    tiles_per_stream = prompt_seq // tm
    n_p = m_p // tm
    spt = tm // seq_s
    dff = wa.shape[2]
    s_blk = lambda j, i: (jnp.maximum(i - n_p, 0), 0, j)
    w_blk = lambda j, i: (layer, 0, j)
    return pl.pallas_call(
        functools.partial(_ffn_in_kernel, n_p, tiles_per_stream, seq_s),
        grid=(dff // tn, m // tm),
        in_specs=[pl.BlockSpec((tm, d), lambda j, i: (i, 0)),
                  pl.BlockSpec((None, d, tn), w_blk),
                  pl.BlockSpec((None, d, tn), w_blk),
                  pl.BlockSpec((None, FFN_CONV, tn), w_blk),
                  pl.BlockSpec((None, 1, tn), w_blk),
                  pl.BlockSpec((None, spt, SUBLANE, tn), lambda j, i: (layer,) + s_blk(j, i))],
        out_specs=[pl.BlockSpec((tm, tn), lambda j, i: (i, j)),
                   pl.BlockSpec((1, FFN_CONV - 1, tn),
                                lambda j, i: (jnp.minimum(i, n_p - 1) // tiles_per_stream, 0, j)),
                   pl.BlockSpec((spt, FFN_CONV - 1, tn), s_blk)],
        out_shape=[jax.ShapeDtypeStruct((m, D_FF), BF16),
                   jax.ShapeDtypeStruct((n_prompt_streams, FFN_CONV - 1, D_FF), F32),
                   jax.ShapeDtypeStruct((n_s_streams, FFN_CONV - 1, D_FF), F32)],
        scratch_shapes=[pltpu.VMEM((SUBLANE + tm, tn), F32)],
        compiler_params=_compiler_params(("arbitrary", "arbitrary")),
        name="ffn_in",
    )(h, wa, wu, cw, cb, cache_in)


def _level_matrix():
    t = np.arange(L)[:, None]
    u = np.arange(L)[None, :]
    blocks = [(u <= t)]
    for l in range(N_LEVELS):
        n = 1 << l
        r = (t & ~(2 * n - 1)) + n - 1
        upper = (t & n) != 0
        blocks.append(np.where(upper, (u > r) & (u <= t), (u > t) & (u <= r)))
    blocks.append(u > t)
    return np.concatenate(blocks, axis=0).astype(np.float32)


def _pair_levels():
    r = np.arange(GROUP * L)
    x = r[:, None] ^ r[None, :]
    code = np.full(x.shape, -1, np.int32)
    code[x == 0] = 0
    for l in range(N_LEVELS):
        code[(x >> l) == 1] = l + 1
    code[r[None, :] > r[:, None]] = -1
    return code


def _stack(x):
    return jnp.concatenate([x[:, h * HEAD_DV:(h + 1) * HEAD_DV] for h in range(N_HEADS)], axis=0)


def _stack_cols(x, first):
    return jnp.concatenate([x[:, first + h:first + h + 1] for h in range(N_HEADS)], axis=0)


def _groups(x):
    gl = GROUP * L
    return [x[g * gl:(g + 1) * gl] for g in range(N_HEADS // GROUP)]


def _pair_product(a, b):
    return [_dot_nt(ag, bg) for ag, bg in zip(_groups(a), _groups(b))]


def _block_diag(x):
    z = jnp.zeros((L, HEAD_DV), x.dtype)
    rows = [jnp.concatenate([x[h * L:(h + 1) * L] if j == h else z for j in range(N_HEADS)], axis=1)
            for h in range(N_HEADS)]
    return jnp.concatenate(rows, axis=0)


def _store_heads(o_ref, off, y):
    for h in range(N_HEADS):
        o_ref[:, off + h * HEAD_DV:off + (h + 1) * HEAD_DV] = y[h * L:(h + 1) * L].astype(o_ref.dtype)


def _head_rms(o, gain, gate):
    y = o * lax.rsqrt(jnp.mean(o * o, axis=-1, keepdims=True) + EPS)
    return y * gain * _silu(gate)


def _gla_type(q, k, v, g, gate, gain, st_ref, mall, lvl, o_ref, o_off):
    e_all = jnp.exp2(_dot(mall, (g() * LOG2_E).astype(BF16)))

    def e_rows(r):
        return _stack(e_all[r * L:(r + 1) * L])

    qs, ks = _stack(q()), _stack(k())
    att = [jnp.where(lvl == 0, a, 0.0) for a in _pair_product(qs.astype(BF16), ks.astype(BF16))]
    for l in range(N_LEVELS):
        yield
        e_l = e_rows(l + 1)
        a_l = _pair_product((qs * e_l).astype(BF16), (ks * e_l).astype(BF16))
        att = [jnp.where(lvl == l + 1, a, prev) for a, prev in zip(a_l, att)]
    yield
    st = [st_ref[0, h] for h in range(N_HEADS)]
    vb = _stack(v()).astype(BF16)
    q_decayed = (qs * e_rows(0)).astype(BF16)
    q_state = jnp.concatenate(
        [_dot_nt(q_decayed[h * L:(h + 1) * L], st[h].astype(BF16)) for h in range(N_HEADS)], axis=0)
    o = jnp.concatenate([_dot(a.astype(BF16), vg) for a, vg in zip(att, _groups(vb))], axis=0) + q_state
    _store_heads(o_ref, o_off, _head_rms(o, gain, _stack(gate())))
    upd = _dot_tn(_block_diag(vb), (ks * e_rows(N_LEVELS + 1)).astype(BF16))
    for h in range(N_HEADS):
        decay_end = e_all[L - 1:L, h * HEAD_DV:(h + 1) * HEAD_DV]
        st_ref[0, h] = st[h] * decay_end + upd[h * HEAD_DV:(h + 1) * HEAD_DV]


CONV_BASE = SUBLANE - (GDN_CONV - 1)


def _lane(layer, proj_ref, cos_ref, sin_ref, mall_ref, lvl_ref, wgk_ref, bgk_ref, norms_ref, lb_ref,
          convw_ref, gdnp_ref, o_ref, xe_ref, sgla_ref, shg_ref, sgdn_ref, sret_ref):
    base = CONV_BASE
    lvl = lvl_ref[...]
    incl = lvl >= 0
    mall = mall_ref[...]
    tri = mall[0:L]

    def col(off, width=512):
        return proj_ref[:, off:off + width]

    def g_gla():
        z = _dot(col(P_GLA_LR, LANE).astype(BF16), wgk_ref[...]) + bgk_ref[...]
        return (jnp.minimum(z, 0.0) - jnp.log(1.0 + jnp.exp(-jnp.abs(z)))) * (1.0 / GLA_GATE_NORM)

    gla = _gla_type(lambda: col(P_GLA_Q) * GLA_DK ** -0.5, lambda: col(P_GLA_K), lambda: col(P_GLA_V),
                    g_gla, lambda: col(P_GLA_G), norms_ref[0:1, :], sgla_ref, mall, lvl, o_ref, 0)
    if layer == 0:
        lb = jnp.zeros((1, 512), F32)
    else:
        logits = lb_ref[...]
        ex = jnp.exp(logits - jnp.max(logits, axis=0, keepdims=True))
        soft = ex / jnp.sum(ex, axis=0, keepdims=True)
        lb = jnp.sum(soft[1:layer + 1], axis=0, keepdims=True)
    forget = []

    def f():
        if not forget:
            forget.append(lb + (1.0 - lb) * _sigmoid(col(P_HG_F)))
        return forget[0]

    hgrn = _gla_type(lambda: _silu(col(P_HG_Q)), lambda: 1.0 - f(), lambda: col(P_HG_I), lambda: jnp.log(f()),
                     lambda: col(P_HG_G), norms_ref[1:2, :], shg_ref, mall, lvl, o_ref, BRANCH_WIDTH)
    ret = _retention(col, cos_ref, sin_ref, norms_ref, sret_ref, incl, o_ref)

    xe_ref[SUBLANE:SUBLANE + L, :] = col(P_GDN_QKV, GDN_QKV)
    conv = convw_ref[0:1, :] * xe_ref[base:base + L, :]
    for j in range(1, GDN_CONV):
        conv = conv + convw_ref[j:j + 1, :] * xe_ref[base + j:base + j + L, :]
    xe_ref[base:SUBLANE, :] = xe_ref[SUBLANE + L - (GDN_CONV - 1):SUBLANE + L, :]
    act = _silu(conv)
    ba = col(P_GDN_BA, LANE)
    beta = _stack_cols(_sigmoid(ba), 0)
    log_a = -jnp.exp(gdnp_ref[0:1, :]) * _softplus(ba + gdnp_ref[1:2, :])
    la_hi, la_mid, la_lo = _split3(log_a)
    b_all = _dot(tri, la_hi) + (_dot(tri, la_mid) + _dot(tri, la_lo))
    b_col = _stack_cols(b_all, N_HEADS)
    b_rows = b_all.T
    b_row = [jnp.concatenate([b_rows[N_HEADS + GROUP * g + i:N_HEADS + GROUP * g + i + 1, :]
                              for i in range(GROUP)], axis=1) for g in range(N_HEADS // GROUP)]
    b_end = jnp.concatenate(
        [jnp.broadcast_to(b_all[L - 1:L, N_HEADS + h:N_HEADS + h + 1], (L, 1)) for h in range(N_HEADS)], axis=0)
    dq, dk, dv = _stack(act[:, 0:512]), _stack(act[:, 512:1024]), _stack(act[:, 1024:1536])
    qn = dq * lax.rsqrt(jnp.sum(dq * dq, axis=-1, keepdims=True) + EPS) * HEAD_DV ** -0.5
    kn = dk * lax.rsqrt(jnp.sum(dk * dk, axis=-1, keepdims=True) + EPS)
    qb, kb = qn.astype(BF16), kn.astype(BF16)
    dec = [jnp.exp(jnp.where(incl, bc - br, -1e30)) for bc, br in zip(_groups(b_col), b_row)]
    e_b = jnp.exp(b_col)
    st = [sgdn_ref[0, h] for h in range(N_HEADS)]
    kq_state = [_dot(jnp.concatenate([kb[h * L:(h + 1) * L], qb[h * L:(h + 1) * L]], axis=0),
                     st[h].astype(BF16)) for h in range(N_HEADS)]
    k_state = jnp.concatenate([x[0:L] for x in kq_state], axis=0)
    q_state = jnp.concatenate([x[L:2 * L] for x in kq_state], axis=0)
    rhs = beta * (dv - e_b * k_state)
    p_groups = [-(jnp.where(lvl > 0, kk * dec_g, 0.0) * beta_g)
                for kk, dec_g, beta_g in zip(_pair_product(kb, kb), dec, _groups(beta))]
    u_groups = _groups(rhs)
    next(gla)
    next(hgrn)
    for j in range(N_LEVELS):
        yield
        u_groups = [u + _dot_hi(p, u) for p, u in zip(p_groups, u_groups)]
        if j + 1 < N_LEVELS:
            p_groups = [_dot_hi(p, p) for p in p_groups]
        next(gla)
        next(hgrn)
        next(ret, None)
    yield
    ub = jnp.concatenate(u_groups, axis=0).astype(BF16)
    att = [(qk * dec_g).astype(BF16) for qk, dec_g in zip(_pair_product(qb, kb), dec)]
    o = jnp.concatenate([_dot(a, ug) for a, ug in zip(att, _groups(ub))], axis=0) + e_b * q_state
    _store_heads(o_ref, 2 * BRANCH_WIDTH, _head_rms(o, norms_ref[2:3, :], _stack(col(P_GDN_Z))))
    upd = _dot_tn((kn * jnp.exp(b_end - b_col)).astype(BF16), _block_diag(ub))
    for h in range(N_HEADS):
        decay_end = jnp.exp(b_all[L - 1:L, N_HEADS + h:N_HEADS + h + 1])
        sgdn_ref[0, h] = decay_end * st[h] + upd[:, h * HEAD_DV:(h + 1) * HEAD_DV]

    for gen in (gla, hgrn, ret):
        for _ in gen:
            pass


def _retention(col, cos_ref, sin_ref, norms_ref, sret_ref, incl, o_ref):
    hl = N_HEADS * L
    cos = jnp.concatenate([cos_ref[...]] * N_HEADS, axis=0)
    sin = jnp.concatenate([sin_ref[...]] * N_HEADS, axis=0)
    rq, rk = _stack(col(P_RET_Q)), _stack(col(P_RET_K))
    vb = _stack(col(P_RET_V)).astype(BF16)
    row = lax.broadcasted_iota(jnp.int32, (hl, 1), 0)
    t_col = jnp.bitwise_and(row, L - 1).astype(F32)
    lg = jnp.zeros((hl, 1), F32)
    for h in range(N_HEADS):
        lg = jnp.where(jnp.right_shift(row, N_LEVELS) == h, math.log(1.0 - 2.0 ** (-5.0 - h)), lg)
    gl = GROUP * L
    t_minus_s = (lax.broadcasted_iota(jnp.int32, (gl, 1), 0)
                 - lax.broadcasted_iota(jnp.int32, (1, gl), 1)).astype(F32)
    dec = [jnp.exp(jnp.where(incl, t_minus_s * lg_g, -1e30)) for lg_g in _groups(lg)]
    qr = (rq * cos + pltpu.roll(rq, HEAD_DV // 2, axis=1) * sin).astype(BF16)
    kr = (rk * cos + pltpu.roll(rk, HEAD_DV // 2, axis=1) * sin) * HEAD_DV ** -0.5
    kb = kr.astype(BF16)
    yield
    st = [sret_ref[0, h] for h in range(N_HEADS)]
    q_state = jnp.concatenate(
        [_dot(qr[h * L:(h + 1) * L], st[h].astype(BF16)) for h in range(N_HEADS)], axis=0)
    att = [(qk * dec_g).astype(BF16) for qk, dec_g in zip(_pair_product(qr, kb), dec)]
    yield
    o = (jnp.concatenate([_dot(a, vg) for a, vg in zip(att, _groups(vb))], axis=0)
         + jnp.exp((t_col + 1.0) * lg) * q_state)
    yield
    mu = jnp.mean(o, axis=-1, keepdims=True)
    oc = o - mu
    y = oc * lax.rsqrt(jnp.mean(oc * oc, axis=-1, keepdims=True) + EPS)
    y = (y * norms_ref[3:4, :] + norms_ref[4:5, :]) * _silu(_stack(col(P_RET_G)))
    _store_heads(o_ref, 3 * BRANCH_WIDTH, y)
    yield
    k_end = (kr * jnp.exp((L - 1.0 - t_col) * lg)).astype(BF16)
    upd = _dot_tn(k_end, _block_diag(vb))
    for h in range(N_HEADS):
        gamma_l = math.exp(L * math.log(1.0 - 2.0 ** (-5.0 - h)))
        sret_ref[0, h] = gamma_l * st[h] + upd[:, h * HEAD_DV:(h + 1) * HEAD_DV]


N_LANES = 2


def _run_lanes(lanes):
    lanes = list(lanes)
    while lanes:
        for lane in list(lanes):
            if next(lane, StopIteration) is StopIteration:
                lanes.remove(lane)


def _write_states(k, carry, xe_ref, gla_out, hg_out, gdn_out, ret_out, cache_out):
    sgla_ref, shg_ref, sgdn_ref, sret_ref = carry
    for h in range(N_HEADS):
        gla_out[k, h] = sgla_ref[k, 0, h].T[0:GLA_DK]
        hg_out[k, h] = shg_ref[k, 0, h].T
    gdn_out[k] = sgdn_ref[k, 0]
    ret_out[k] = sret_ref[k, 0]
    cache_out[k] = xe_ref[k, CONV_BASE:SUBLANE, :]


def _mixer_prompt_kernel(layer, n_chunks, *refs):
    proj_refs, refs = refs[:N_LANES], refs[N_LANES:]
    consts, refs = refs[:10], refs[10:]
    o_ref, outs, (xe_ref, *carry) = refs[0], refs[1:6], refs[6:]
    c = pl.program_id(0)

    @pl.when(c == 0)
    def _():
        for ref in carry:
            ref[...] = jnp.zeros(ref.shape, F32)
        xe_ref[...] = jnp.zeros(xe_ref.shape, F32)

    _run_lanes(_lane(layer, proj_refs[k], *consts, o_ref.at[k], xe_ref.at[k],
                     *[s.at[k] for s in carry]) for k in range(N_LANES))

    @pl.when(c == n_chunks - 1)
    def _():
        for k in range(N_LANES):
            _write_states(k, carry, xe_ref, *outs)


def _mixer_sample_kernel(layer, proj_ref, sgla_in, shg_in, sgdn_in, sret_in, cgdn_in, *refs):
    consts, refs = refs[:10], refs[10:]
    o_ref, outs, (xe_ref, *carry) = refs[0], refs[1:6], refs[6:]
    sgla_ref, shg_ref, sgdn_ref, sret_ref = carry
    key_pad = jnp.zeros((HEAD_DV - GLA_DK, HEAD_DV), F32)
    for k in range(N_LANES):
        for h in range(N_HEADS):
            sgla_ref[k, 0, h] = jnp.concatenate([sgla_in[k, h], key_pad], axis=0).T
            shg_ref[k, 0, h] = shg_in[k, h].T
        sgdn_ref[k, 0] = sgdn_in[k]
        sret_ref[k, 0] = sret_in[k]
        xe_ref[k, CONV_BASE:SUBLANE, :] = cgdn_in[k]
    _run_lanes(_lane(layer, proj_ref.at[pl.ds(k * L, L)], *consts, o_ref.at[pl.ds(k * L, L)], xe_ref.at[k],
                     *[s.at[k] for s in carry]) for k in range(N_LANES))
    for k in range(N_LANES):
        _write_states(k, carry, xe_ref, *outs)


def _mixers(layer, proj, states, cache, cos_t, sin_t, mall, lvl, wgk, bgk, norms, lb_logits,
            convw, gdnp, n_prompt_streams, prompt_seq):
    t_total = proj.shape[0]
    chunks_per_stream = prompt_seq // L
    t_prompt = n_prompt_streams * prompt_seq
    n_sample = states[0].shape[1]
    past_block = PAST_LEN // L
    assert n_prompt_streams == N_LANES and n_sample % N_LANES == 0

    def layer_block(a):
        return pl.BlockSpec((None,) + a.shape[1:], lambda c: (layer,) + (0,) * (a.ndim - 1))

    const = lambda a: pl.BlockSpec(a.shape, lambda c: (0,) * a.ndim)
    per_stream = list(states) + [cache]
    lanes_out = lambda idx: [pl.BlockSpec((N_LANES,) + a.shape[2:],
                                          lambda c, a=a: (idx(c),) + (0,) * (a.ndim - 2))
                             for a in per_stream]
    out_shapes = lambda n: [jax.ShapeDtypeStruct((n,) + a.shape[2:], F32) for a in per_stream]
    carry = pltpu.VMEM((N_LANES, 1, N_HEADS, HEAD_DV, HEAD_DV), F32)
    scratch = [pltpu.VMEM((N_LANES, SUBLANE + L, GDN_QKV), F32), carry, carry, carry, carry]

    def consts(pos_block):
        specs = [pl.BlockSpec((L, HEAD_DV), lambda c: (pos_block(c), 0)),
                 pl.BlockSpec((L, HEAD_DV), lambda c: (pos_block(c), 0)),
                 const(mall), const(lvl), layer_block(wgk), layer_block(bgk),
                 layer_block(norms), const(lb_logits), layer_block(convw), layer_block(gdnp)]
        return specs, (cos_t, sin_t, mall, lvl, wgk, bgk, norms, lb_logits, convw, gdnp)

    c_specs, c_args = consts(lambda c: c)
    o_p, *st_p = pl.pallas_call(
        functools.partial(_mixer_prompt_kernel, layer, chunks_per_stream),
        grid=(chunks_per_stream,),
        in_specs=[*[pl.BlockSpec((L, N_PACK), lambda c, k=k: (k * chunks_per_stream + c, 0))
                    for k in range(N_LANES)], *c_specs],
        out_specs=[pl.BlockSpec((N_LANES, L, D_MODEL), lambda c: (0, c, 0)), *lanes_out(lambda c: 0)],
        out_shape=[jax.ShapeDtypeStruct((N_LANES, prompt_seq, D_MODEL), BF16), *out_shapes(N_LANES)],
        scratch_shapes=scratch,
        compiler_params=_compiler_params(("arbitrary",)),
        name="mixers_prompt",
    )(*[proj] * N_LANES, *c_args)

    c_specs, c_args = consts(lambda s: past_block)
    first_block = t_prompt // (N_LANES * L)
    o_s, *st_s = pl.pallas_call(
        functools.partial(_mixer_sample_kernel, layer),
        grid=(n_sample // N_LANES,),
        in_specs=[pl.BlockSpec((N_LANES * L, N_PACK), lambda s: (first_block + s, 0)),
                  *[pl.BlockSpec((None, N_LANES) + a.shape[2:], lambda s, a=a: (layer, s) + (0,) * (a.ndim - 2))
                    for a in per_stream],
                  *c_specs],
        out_specs=[pl.BlockSpec((N_LANES * L, D_MODEL), lambda s: (s, 0)), *lanes_out(lambda s: s)],
        out_shape=[jax.ShapeDtypeStruct((t_total - t_prompt, D_MODEL), BF16), *out_shapes(n_sample)],
        scratch_shapes=scratch,
        compiler_params=_compiler_params(("arbitrary",)),
        name="mixers_sample",
    )(proj, *per_stream, *c_args)
    return (o_p.reshape(t_prompt, D_MODEL), o_s), st_p, st_s


def _pad_heads(w, dk):
    lead = w.shape[:-1]
    w = w.reshape(lead + (N_HEADS, dk))
    w = jnp.pad(w, [(0, 0)] * len(lead) + [(0, 0), (0, HEAD_DV - dk)])
    return w.reshape(lead + (N_HEADS * HEAD_DV,))


def _pad_cols(w, width):
    return jnp.pad(w, [(0, 0)] * (w.ndim - 1) + [(0, width - w.shape[-1])])


def _mix_pieces():
    pieces = []
    for h in range(N_HEADS):
        pieces.append((P_GLA_Q + h * HEAD_DV, h * GLA_DK, GLA_DK, HEAD_DV))
        pieces.append((P_GLA_K + h * HEAD_DV, 256 + h * GLA_DK, GLA_DK, HEAD_DV))
    pieces += [(P_GLA_V, 512, 512, 512), (P_GLA_G, 1024, 512, 512), (P_GLA_LR, 1536, GLA_LOW_RANK, LANE),
               (P_HG_Q, 1552, 512, 512), (P_HG_F, 2064, 512, 512), (P_HG_I, 2576, 512, 512),
               (P_HG_G, 3088, 512, 512), (P_GDN_QKV, 3600, GDN_QKV, GDN_QKV), (P_GDN_Z, 5136, 512, 512),
               (P_GDN_BA, 5648, 2 * N_HEADS, LANE), (P_RET_Q, 5656, 512, 512), (P_RET_K, 6168, 512, 512),
               (P_RET_V, 6680, 512, 512), (P_RET_G, 7192, 512, 512)]
    return pieces


def _mix_units():
    units = [None] * (N_PACK // LANE)
    for dst, src, n, padded in _mix_pieces():
        for q in range(padded // LANE):
            units[dst // LANE + q] = (src + q * LANE, min(LANE, n - q * LANE))
    return units


def _merge_units(merge_tn):
    return [(N_MIX_IN + n * D_MODEL + j * merge_tn + q * LANE, LANE)
            for j in range(D_MODEL // merge_tn) for n in range(N_BRANCH) for q in range(merge_tn // LANE)]


def _pack_rows_kernel(per_step, src_ref, valid_ref, *refs):
    w_refs, o_ref = refs[:per_step], refs[per_step]
    s = pl.program_id(0)
    row = lax.broadcasted_iota(jnp.int32, (LANE, 1), 0)
    for q in range(per_step):
        x = jnp.where(row < valid_ref[s * per_step + q], w_refs[q][...], 0.0)
        o_ref[q * LANE:(q + 1) * LANE, :] = x.astype(o_ref.dtype)


def _pack_rows(w_t, layer, units, per_step, name):
    _, n_in, k = w_t.shape
    assert n_in % SUBLANE == 0 and all(u[0] % SUBLANE == 0 for u in units)
    w_t = w_t.reshape(-1, k)
    src = jnp.asarray([(layer * n_in + u[0]) // SUBLANE for u in units], jnp.int32)
    valid = jnp.asarray([u[1] for u in units], jnp.int32)
    grid_spec = pltpu.PrefetchScalarGridSpec(
        num_scalar_prefetch=2,
        grid=(len(units) // per_step,),
        in_specs=[pl.BlockSpec((pl.Element(LANE), pl.Element(k)),
                               lambda s, src, valid, q=q: (src[s * per_step + q] * SUBLANE, 0))
                  for q in range(per_step)],
        out_specs=pl.BlockSpec((per_step * LANE, k), lambda s, src, valid: (s, 0)))
    return pl.pallas_call(
        functools.partial(_pack_rows_kernel, per_step),
        grid_spec=grid_spec,
        out_shape=jax.ShapeDtypeStruct((len(units) * LANE, k), BF16),
        compiler_params=_compiler_params(("arbitrary",)),
        name=name,
    )(src, valid, *[w_t] * per_step)


def _cast_rows_kernel(n_valid, w_ref, o_ref):
    i = pl.program_id(1)

    @pl.when(i < n_valid)
    def _():
        o_ref[...] = w_ref[...].astype(o_ref.dtype)

    @pl.when(i >= n_valid)
    def _():
        o_ref[...] = jnp.zeros(o_ref.shape, o_ref.dtype)


def _cast_rows(w, rows_out, tk):
    depth, r, c = w.shape
    n_valid = r // tk
    assert r % tk == 0 and rows_out % tk == 0
    return pl.pallas_call(
        functools.partial(_cast_rows_kernel, n_valid),
        grid=(depth, rows_out // tk),
        in_specs=[pl.BlockSpec((None, tk, c), lambda l, i: (l, jnp.minimum(i, n_valid - 1), 0))],
        out_specs=pl.BlockSpec((None, tk, c), lambda l, i: (l, i, 0)),
        out_shape=jax.ShapeDtypeStruct((depth, rows_out, c), BF16),
        compiler_params=_compiler_params(("parallel", "arbitrary")),
        name="cast_rows",
    )(w)


def _split_ffn_in_kernel(w_ref, a_ref, u_ref):
    pad = jnp.zeros((a_ref.shape[0], D_FF_PAD - D_FF), BF16)
    a_ref[:, 0:D_FF] = w_ref[:, 0:D_FF].astype(BF16)
    a_ref[:, D_FF:D_FF_PAD] = pad
    u_ref[:, 0:D_FF] = w_ref[:, D_FF:2 * D_FF].astype(BF16)
    u_ref[:, D_FF:D_FF_PAD] = pad


def _split_ffn_in(w, tk):
    depth, d, _ = w.shape
    blk = lambda c: pl.BlockSpec((None, tk, c), lambda l, i: (l, i, 0))
    out = jax.ShapeDtypeStruct((depth, d, D_FF_PAD), BF16)
    return pl.pallas_call(
        _split_ffn_in_kernel,
        grid=(depth, d // tk),
        in_specs=[blk(2 * D_FF)],
        out_specs=[blk(D_FF_PAD), blk(D_FF_PAD)],
        out_shape=[out, out],
        compiler_params=_compiler_params(("parallel", "parallel")),
        name="split_ffn_in",
    )(w)


def _rope_tables(n_pos):
    pos = np.arange(n_pos, dtype=np.float64)
    inv = 1.0 / (ROPE_BASE ** (np.arange(0, HEAD_DV, 2, dtype=np.float64) / HEAD_DV))
    ang = pos[:, None] * inv[None, :]
    cos, sin = np.cos(ang), np.sin(ang)
    return (jnp.asarray(np.concatenate([cos, cos], axis=1), F32),
            jnp.asarray(np.concatenate([-sin, sin], axis=1), F32))


def kernel(x_prompt, x_sample, state_gla, state_hgrn, state_gdn, state_ret, cache_gdn_conv, cache_ffn_conv, norm_mix_g, w_in, gla_w_gk, gla_b_gk, gla_norm_g, hgrn_lb_logits, hgrn_norm_g, gdn_conv_w, gdn_a_log, gdn_dt_bias, gdn_norm_g, ret_norm_g, ret_norm_b, w_branch, w_out, norm_ffn_g, w_ffn_in, ffn_conv_w, ffn_conv_b, w_ffn_out, norm_final_g):
    nb, seq, d = x_prompt.shape
    ns, dseq, _ = x_sample.shape
    assert d == D_MODEL and dseq == L and seq % L == 0
    tp = nb * seq
    x = (x_prompt.reshape(tp, d), x_sample.reshape(ns * dseq, d))

    cos_t, sin_t = _rope_tables(max(seq, PAST_LEN + dseq))
    mall = jnp.asarray(_level_matrix(), BF16)
    lvl = jnp.asarray(_pair_levels())
    wgk = jnp.pad(_pad_heads(gla_w_gk, GLA_DK), ((0, 0), (0, LANE - GLA_LOW_RANK), (0, 0))).astype(BF16)
    bgk = _pad_heads(gla_b_gk, GLA_DK)[:, None, :]
    norms = jnp.stack([gla_norm_g, hgrn_norm_g, gdn_norm_g, ret_norm_g, ret_norm_b]
                      + [jnp.zeros_like(gla_norm_g)] * 3, axis=1)
    head_lanes = ((0, 0), (N_HEADS, LANE - 2 * N_HEADS))
    gdnp = jnp.stack([jnp.pad(gdn_a_log, head_lanes), jnp.pad(gdn_dt_bias, head_lanes)], axis=1)
    w_branch_b = _cast_rows(w_branch.reshape(DEPTH, N_BRANCH * BRANCH_WIDTH, d), N_BRANCH * BRANCH_WIDTH,
                            CAST_TK).reshape(w_branch.shape)
    w_out_b = _cast_rows(w_out, d, CAST_TK)
    wa, wu = _split_ffn_in(w_ffn_in, LANE)
    w_dn = _cast_rows(w_ffn_out, D_FF, D_FF // SUBLANE)
    cw = _pad_cols(ffn_conv_w, D_FF_PAD)
    cb = _pad_cols(ffn_conv_b, D_FF_PAD)[:, None, :]
    cache_in = jnp.pad(cache_ffn_conv,
                       ((0, 0), (0, 0), (SUBLANE - (FFN_CONV - 1), 0), (0, D_FF_PAD - D_FF)))
    states = (state_gla, state_hgrn, state_gdn, state_ret)
    w_in_t = jnp.swapaxes(w_in, 1, 2)

    h = _rmsnorm(x[0], x[1], norm_mix_g[0], BF16)
    new_p, new_s = [], []
    for l in range(DEPTH):
        w_mix = _pack_rows(w_in_t, l, _mix_units(), PACK_MIX_GROUPS, "pack_w_mix")
        w_merge = _pack_rows(w_in_t, l, _merge_units(MERGE_TN), PACK_MERGE_GROUPS, "pack_w_merge")
        proj = _matmul(h, w_mix, IN_PROJ_TN, F32)
        o, st_p, st_s = _mixers(l, proj, states, cache_gdn_conv, cos_t, sin_t, mall, lvl, wgk, bgk, norms,
                                hgrn_lb_logits, gdn_conv_w, gdnp, nb, seq)
        y = _merge(h, o, w_merge, w_branch_b, l, MERGE_TN)
        x, h2 = _resid_matmul_norm(y, w_out_b, l, x, norm_ffn_g[l], tp, OUT_PROJ_TM, False, "out_proj")
        g, c_ffn_p, c_ffn_s = _ffn_in(h2, wa, wu, cw, cb, cache_in, l, nb, seq, dseq, FFN_IN_TN)
        if l + 1 < DEPTH:
            x, h = _resid_matmul_norm(g, w_dn, l, x, norm_mix_g[l + 1], tp, FFN_OUT_TM, False, "ffn_out")
        else:
            y_p, y_s = _resid_matmul_norm(g, w_dn, l, x, norm_final_g, tp, FFN_OUT_TM, True, "ffn_out_final")
        new_p.append(st_p + [c_ffn_p])
        new_s.append(st_s + [c_ffn_s])

    y_prompt = y_p.reshape(nb, seq, d)
    y_sample = y_s.reshape(ns, dseq, d)
    p_out = tuple(jnp.stack([st[i] for st in new_p]) for i in range(6))
    s_out = tuple(jnp.stack([st[i] for st in new_s]) for i in range(6))
    return (y_prompt, y_sample) + p_out + s_out
```
